```python
import math
import jax, jax.numpy as jnp
from jax import lax
import numpy as np

D_MODEL = 2048
BATCH = 2
SEQ = 16384
DEPTH = 1
DEC_BATCH = 8
DEC_SEQ = 2048
PAST_LEN = 128

MLA_HEADS = 8
Q_LORA = 512
KV_LORA = 512
QK_NOPE = 128
QK_ROPE = 64
V_HEAD = 128
ROPE_THETA = 10000.0
Q_BLOCK = 128
GDN_HEADS = 8
GDN_DK = 128
GDN_DV = 128
GDN_QKV = GDN_HEADS * (2 * GDN_DK + GDN_DV)
CONV_K = 5
CHUNK = 64
D_ATTN = MLA_HEADS * V_HEAD
D_GDN = GDN_HEADS * GDN_DV
D_MIX = D_ATTN + D_GDN
IN_SPLITS = (Q_LORA, KV_LORA, QK_ROPE, GDN_QKV, D_GDN, 2 * GDN_HEADS, 2 * GDN_HEADS)
N_IN = Q_LORA + KV_LORA + QK_ROPE + GDN_QKV + D_GDN + 4 * GDN_HEADS
N_GROUPS = 8
EXPERTS_PER_GROUP = 8
N_EXPERTS = N_GROUPS * EXPERTS_PER_GROUP
TOP_K = 2
D_EXPERT = 512
MOE_BLOCK = 256
EPS = 1e-6

kernel_name = "hybrid_mla_gdn_hmoe_encoder"


def rmsnorm(x, g):
    xf = x.astype(jnp.float32)
    y = xf * lax.rsqrt(jnp.mean(xf * xf, axis=-1, keepdims=True) + EPS) * g.astype(jnp.float32)
    return y.astype(x.dtype)


def l2norm(x):
    return x * lax.rsqrt(jnp.sum(x * x, axis=-1, keepdims=True) + EPS)


def split_points(sizes):
    return [int(v) for v in np.cumsum(sizes)[:-1]]


def rope_tables(seq, dtype):
    inv_freq = ROPE_THETA ** (-jnp.arange(0, QK_ROPE, 2, dtype=jnp.float32) / QK_ROPE)
    ang = jnp.arange(seq, dtype=jnp.float32)[:, None] * inv_freq[None, :]
    return jnp.cos(ang).astype(dtype), jnp.sin(ang).astype(dtype)


def apply_rope(x, cos, sin):
    x1, x2 = jnp.split(x, 2, axis=-1)
    return jnp.concatenate([x1 * cos - x2 * sin, x2 * cos + x1 * sin], axis=-1)


def mla(c_q, c_kv, k_pe, g_q, g_kv, w_uq, w_ukv):
    b, s, _ = c_q.shape
    cos, sin = rope_tables(s, c_q.dtype)
    scale = (QK_NOPE + QK_ROPE) ** -0.5
    q = jnp.einsum('bsr,rn->bsn', rmsnorm(c_q, g_q), w_uq).reshape(b, s, MLA_HEADS, QK_NOPE + QK_ROPE) * scale
    q_nope = q[..., :QK_NOPE]
    q_pe = apply_rope(q[..., QK_NOPE:], cos[:, None, :], sin[:, None, :])
    kv = jnp.einsum('bsr,rn->bsn', rmsnorm(c_kv, g_kv), w_ukv).reshape(b, s, MLA_HEADS, QK_NOPE + V_HEAD)
    k_nope, v = kv[..., :QK_NOPE], kv[..., QK_NOPE:]
    k_pe = apply_rope(k_pe, cos, sin)
    nb = s // Q_BLOCK

    def to_blocks(t):
        return jnp.swapaxes(t.reshape(b, nb, Q_BLOCK, *t.shape[2:]), 0, 1)

    def attend(blk):
        qn, qp = blk
        sc = (jnp.einsum('bqhd,bkhd->bhqk', qn, k_nope, preferred_element_type=jnp.float32)
              + jnp.einsum('bqhr,bkr->bhqk', qp, k_pe, preferred_element_type=jnp.float32))
        p = jax.nn.softmax(sc, axis=-1).astype(v.dtype)
        return jnp.einsum('bhqk,bkhd->bqhd', p, v)

    o = lax.map(attend, (to_blocks(q_nope), to_blocks(q_pe)))
    return jnp.swapaxes(o, 0, 1).reshape(b, s, D_ATTN)


def centred_conv(x, w):
    c = x.shape[-1]
    return lax.conv_general_dilated(
        x, w[:, None, :].astype(x.dtype), window_strides=(1,),
        padding=[(CONV_K // 2, CONV_K // 2)],
        dimension_numbers=('NWC', 'WIO', 'NWC'), feature_group_count=c)


def unit_lower_inverse(lower):
    eye = jnp.eye(CHUNK, dtype=lower.dtype)
    nil = -lower
    inv = eye + nil
    power = nil
    for _ in range(int(math.log2(CHUNK)) - 1):
        power = power @ power
        inv = inv + inv @ power
    return inv


def gated_delta_chunked(q, k, v, g, beta):
    b, s, h, dk = q.shape
    n = s // CHUNK

    def chunks(t):
        return t.reshape(b, n, CHUNK, h, -1).transpose(0, 3, 1, 2, 4)

    q, k, v = chunks(q), chunks(k), chunks(v)
    g = g.reshape(b, n, CHUNK, h).transpose(0, 3, 1, 2)
    beta = beta.reshape(b, n, CHUNK, h).transpose(0, 3, 1, 2)
    gc = jnp.cumsum(g, axis=-1)
    idx = jnp.arange(CHUNK)
    incl = idx[:, None] >= idx[None, :]
    decay = jnp.exp(jnp.where(incl, gc[..., :, None] - gc[..., None, :], -jnp.inf))
    k_beta = k * beta[..., None]
    lower = jnp.where(idx[:, None] > idx[None, :],
                      jnp.einsum('bhnid,bhnjd->bhnij', k_beta, k) * decay, 0.0)
    t_inv = unit_lower_inverse(lower)
    u = t_inv @ (v * beta[..., None])
    w = t_inv @ (k_beta * jnp.exp(gc)[..., None])
    intra = jnp.einsum('bhnid,bhnjd->bhnij', q, k) * decay
    q_dec = q * jnp.exp(gc)[..., None]
    g_last = gc[..., -1]
    k_dec = k * jnp.exp(g_last[..., None] - gc)[..., None]
    xs = tuple(jnp.moveaxis(t, 2, 0) for t in (u, w, intra, q_dec, k_dec, g_last))

    def step(state, xs_i):
        u_i, w_i, a_i, qd_i, kd_i, gl_i = xs_i
        v_new = u_i - w_i @ state
        o_i = qd_i @ state + a_i @ v_new
        state = state * jnp.exp(gl_i)[..., None, None] + jnp.swapaxes(kd_i, -1, -2) @ v_new
        return state, o_i

    state0 = jnp.zeros((b, h, dk, v.shape[-1]), q.dtype)
    _, o = lax.scan(step, state0, xs)
    return o.transpose(1, 0, 3, 2, 4).reshape(b, s, h, -1)


def gdn(qkv, z, a, bt, conv_w, a_log, dt_bias, g_out):
    b, s, _ = qkv.shape
    qkv = jax.nn.silu(centred_conv(qkv, conv_w)).astype(jnp.float32)
    q, k, v = jnp.split(qkv, [GDN_HEADS * GDN_DK, 2 * GDN_HEADS * GDN_DK], axis=-1)
    q = l2norm(q.reshape(b, s, GDN_HEADS, GDN_DK)) * (GDN_DK ** -0.5)
    k = l2norm(k.reshape(b, s, GDN_HEADS, GDN_DK))
    v = v.reshape(b, s, GDN_HEADS, GDN_DV)
    beta = jax.nn.sigmoid(bt.astype(jnp.float32)).reshape(b, s, 2, GDN_HEADS)
    g = -jnp.exp(a_log.astype(jnp.float32)) * jax.nn.softplus(
        a.astype(jnp.float32).reshape(b, s, 2, GDN_HEADS) + dt_bias.astype(jnp.float32))
    o_fwd = gated_delta_chunked(q, k, v, g[:, :, 0], beta[:, :, 0])
    rev = lambda t: jnp.flip(t, axis=1)
    o_bwd = rev(gated_delta_chunked(rev(q), rev(k), rev(v), rev(g[:, :, 1]), rev(beta[:, :, 1])))
    o = rmsnorm(o_fwd + o_bwd, g_out) * jax.nn.silu(z.astype(jnp.float32).reshape(b, s, GDN_HEADS, GDN_DV))
    return o.reshape(b, s, D_GDN).astype(z.dtype)


def hier_moe(x, w_rg, b_rg, w_re, b_re, w_gate, w_up, w_down):
    b, s, d = x.shape
    t = b * s
    xt = x.reshape(t, d)
    group_logits = jnp.einsum('td,dg->tg', xt, w_rg, preferred_element_type=jnp.float32) + b_rg.astype(jnp.float32)
    group_prob = jax.nn.softmax(group_logits, axis=-1)
    grp = jnp.argmax(group_logits, axis=-1).astype(jnp.int32)
    grp_p = jnp.take_along_axis(group_prob, grp[:, None], axis=1)
    exp_logits = (jnp.einsum('td,de->te', xt, w_re, preferred_element_type=jnp.float32)
                  + b_re.astype(jnp.float32)).reshape(t, N_GROUPS, EXPERTS_PER_GROUP)
    in_grp = jnp.take_along_axis(exp_logits, grp[:, None, None], axis=1)[:, 0]
    top_p, top_i = lax.top_k(jax.nn.softmax(in_grp, axis=-1), TOP_K)
    gate = grp_p * top_p / jnp.sum(top_p, axis=-1, keepdims=True)
    eid = (grp[:, None] * EXPERTS_PER_GROUP + top_i.astype(jnp.int32)).reshape(-1)
    m = t * TOP_K
    order = jnp.argsort(eid)
    counts = jnp.zeros((N_EXPERTS,), jnp.int32).at[eid].add(1)
    starts = jnp.cumsum(counts) - counts
    padded = (counts + MOE_BLOCK - 1) // MOE_BLOCK * MOE_BLOCK
    pend = jnp.cumsum(padded)
    pstart = pend - padded
    sorted_e = eid[order]
    dest = pstart[sorted_e] + (jnp.arange(m, dtype=jnp.int32) - starts[sorted_e])
    n_blocks = -(-m // MOE_BLOCK) + N_EXPERTS
    p_rows = n_blocks * MOE_BLOCK
    slot_at = jnp.full((p_rows,), m, jnp.int32).at[dest].set(order.astype(jnp.int32))
    tok_of_slot = jnp.concatenate([jnp.arange(m, dtype=jnp.int32) // TOP_K, jnp.array([t], jnp.int32)])
    x_pad = jnp.concatenate([xt, jnp.zeros((1, d), xt.dtype)], axis=0)
    xb = x_pad[tok_of_slot[slot_at]].reshape(n_blocks, MOE_BLOCK, d)
    block_e = jnp.minimum(jnp.searchsorted(pend, jnp.arange(n_blocks, dtype=jnp.int32) * MOE_BLOCK, side='right'),
                          N_EXPERTS - 1)

    def expert_block(args):
        xb_i, e = args
        hdn = jax.nn.silu(xb_i @ w_gate[e]) * (xb_i @ w_up[e])
        return hdn @ w_down[e]

    yb = lax.map(expert_block, (xb, block_e)).reshape(p_rows, d)
    pos = jnp.zeros((m,), jnp.int32).at[order].set(dest)
    y = jnp.sum(yb[pos].reshape(t, TOP_K, d) * gate[..., None].astype(yb.dtype), axis=1)
    return y.reshape(b, s, d)


def encoder(x, norm_mix, w_in, g_q_lora, g_kv_lora, w_uq, w_ukv, g_attn_out, conv_w, a_log, dt_bias,
            g_gdn_out, w_out, norm_ffn, w_router_group, b_router_group, w_router_expert, b_router_expert,
            w_gate, w_up, w_down, norm_final):
    for l in range(DEPTH):
        n = rmsnorm(x, norm_mix[l])
        proj = jnp.einsum('bsd,dn->bsn', n, w_in[l])
        c_q, c_kv, k_pe, qkv, z, a, bt = jnp.split(proj, split_points(IN_SPLITS), axis=-1)
        attn = rmsnorm(mla(c_q, c_kv, k_pe, g_q_lora[l], g_kv_lora[l], w_uq[l], w_ukv[l]), g_attn_out[l])
        lin = gdn(qkv, z, a, bt, conv_w[l], a_log[l], dt_bias[l], g_gdn_out[l])
        x = x + jnp.einsum('bsm,md->bsd', jnp.concatenate([attn, lin], axis=-1), w_out[l])
        x = x + hier_moe(rmsnorm(x, norm_ffn[l]), w_router_group[l], b_router_group[l], w_router_expert[l],
                         b_router_expert[l], w_gate[l], w_up[l], w_down[l])
    return rmsnorm(x, norm_final)


def setup_inputs(seed: int = 0) -> dict:
    key = jax.random.key(seed)
    ks = jax.random.split(key, 24)
    f32 = jnp.float32
    L = DEPTH

    def nrm(k, shape, scale):
        return jax.random.normal(k, shape, f32) * scale

    def gain(k, shape):
        return 1.0 + 0.02 * jax.random.normal(k, shape, f32)

    dt = jnp.exp(jax.random.uniform(ks[9], (L, 2, GDN_HEADS), f32, math.log(1e-3), math.log(1e-1)))
    return {
        "x_prompt": nrm(ks[0], (BATCH, SEQ, D_MODEL), 1.0),
        "x_sample": nrm(ks[1], (DEC_BATCH, DEC_SEQ, D_MODEL), 1.0),
        "norm_mix": gain(ks[2], (L, D_MODEL)),
        "w_in": nrm(ks[3], (L, D_MODEL, N_IN), D_MODEL ** -0.5),
        "g_q_lora": gain(ks[4], (L, Q_LORA)),
        "g_kv_lora": gain(ks[5], (L, KV_LORA)),
        "w_uq": nrm(ks[6], (L, Q_LORA, MLA_HEADS * (QK_NOPE + QK_ROPE)), Q_LORA ** -0.5),
        "w_ukv": nrm(ks[7], (L, KV_LORA, MLA_HEADS * (QK_NOPE + V_HEAD)), KV_LORA ** -0.5),
        "g_attn_out": gain(ks[8], (L, D_ATTN)),
        "conv_w": nrm(ks[10], (L, CONV_K, GDN_QKV), CONV_K ** -0.5),
        "a_log": jnp.log(jax.random.uniform(ks[11], (L, 2, GDN_HEADS), f32, 1.0, 16.0)),
        "dt_bias": dt + jnp.log(-jnp.expm1(-dt)),
        "g_gdn_out": gain(ks[12], (L, GDN_DV)),
        "w_out": nrm(ks[13], (L, D_MIX, D_MODEL), D_MIX ** -0.5),
        "norm_ffn": gain(ks[14], (L, D_MODEL)),
        "w_router_group": nrm(ks[15], (L, D_MODEL, N_GROUPS), D_MODEL ** -0.5),
        "b_router_group": nrm(ks[16], (L, N_GROUPS), 0.01),
        "w_router_expert": nrm(ks[17], (L, D_MODEL, N_EXPERTS), D_MODEL ** -0.5),
        "b_router_expert": nrm(ks[18], (L, N_EXPERTS), 0.01),
        "w_gate": nrm(ks[19], (L, N_EXPERTS, D_MODEL, D_EXPERT), D_MODEL ** -0.5),
        "w_up": nrm(ks[20], (L, N_EXPERTS, D_MODEL, D_EXPERT), D_MODEL ** -0.5),
        "w_down": nrm(ks[21], (L, N_EXPERTS, D_EXPERT, D_MODEL), D_EXPERT ** -0.5),
        "norm_final": gain(ks[22], (D_MODEL,)),
    }


def reference(x_prompt, x_sample, norm_mix, w_in, g_q_lora, g_kv_lora, w_uq, w_ukv, g_attn_out, conv_w,
              a_log, dt_bias, g_gdn_out, w_out, norm_ffn, w_router_group, b_router_group, w_router_expert,
              b_router_expert, w_gate, w_up, w_down, norm_final):
    params = (norm_mix, w_in, g_q_lora, g_kv_lora, w_uq, w_ukv, g_attn_out, conv_w, a_log, dt_bias,
              g_gdn_out, w_out, norm_ffn, w_router_group, b_router_group, w_router_expert, b_router_expert,
              w_gate, w_up, w_down, norm_final)
    y_prompt = encoder(x_prompt, *params)
    y_sample = encoder(x_sample, *params)
    return (y_prompt, y_sample)
```

```python
import functools
import math

import jax
import jax.numpy as jnp
import numpy as np
from jax import lax
from jax.experimental import pallas as pl
from jax.experimental.pallas import tpu as pltpu

F32 = jnp.float32
BF16 = jnp.bfloat16

D_MODEL = 2048
MLA_HEADS = 8
Q_LORA = 512
KV_LORA = 512
QK_NOPE = 128
QK_ROPE = 64
V_HEAD = 128
ROPE_THETA = 10000.0
GDN_HEADS = 8
GDN_DK = 128
GDN_DV = 128
GDN_QKV = GDN_HEADS * (2 * GDN_DK + GDN_DV)
CONV_K = 5
CHUNK = 64
D_ATTN = MLA_HEADS * V_HEAD
D_GDN = GDN_HEADS * GDN_DV
N_GROUPS = 8
EXPERTS_PER_GROUP = 8
N_EXPERTS = N_GROUPS * EXPERTS_PER_GROUP
TOP_K = 2
D_EXPERT = 512
EPS = 1e-6

LANE = 128
QK_PAD = 256
VMEM_LIMIT = 56 * 1024 * 1024


def _params(*sem):
    return pltpu.CompilerParams(dimension_semantics=sem, vmem_limit_bytes=VMEM_LIMIT)


def _resident(shape):
    return pl.BlockSpec(shape, lambda *_: (0,) * len(shape), pipeline_mode=pl.Buffered(1))


def _rms(x, g):
    return x * lax.rsqrt(jnp.mean(x * x, axis=-1, keepdims=True) + EPS) * g


def _dot(a, b):
    return jnp.dot(a, b, preferred_element_type=F32)


def _dot_nt(a, b):
    return lax.dot_general(a, b, (((1,), (1,)), ((), ())), preferred_element_type=F32)


def _inproj_kernel(x_ref, g_ref, w_lat_ref, w_sm_ref, w_smt_ref, w_qkv_ref, w_z_ref,
                   lat_ref, sm_ref, smt_ref, qkv_ref, z_ref):
    xn = _rms(x_ref[...], g_ref[...]).astype(BF16)
    lat_ref[...] = _dot(xn, w_lat_ref[...]).astype(BF16)
    sm_ref[...] = _dot(xn, w_sm_ref[...])
    smt_ref[...] = _dot_nt(w_smt_ref[...], xn)
    qkv_ref[...] = _dot(xn, w_qkv_ref[...]).astype(BF16)
    z_ref[...] = _dot(xn, w_z_ref[...]).astype(BF16)


def _inproj(x2d, g, w_lat, w_sm, w_smt, w_qkv, w_z, tm):
    t, d = x2d.shape
    n_lat, n_sm, n_qkv, n_z = w_lat.shape[1], w_sm.shape[1], w_qkv.shape[1], w_z.shape[1]
    row = lambda n: pl.BlockSpec((tm, n), lambda i: (i, 0))
    return pl.pallas_call(
        _inproj_kernel,
        grid=(t // tm,),
        in_specs=[row(d), _resident((1, d)), _resident(w_lat.shape), _resident(w_sm.shape),
                  _resident(w_smt.shape), _resident(w_qkv.shape), _resident(w_z.shape)],
        out_specs=[row(n_lat), row(n_sm), pl.BlockSpec((n_sm, tm), lambda i: (0, i)), row(n_qkv), row(n_z)],
        out_shape=[jax.ShapeDtypeStruct((t, n_lat), BF16), jax.ShapeDtypeStruct((t, n_sm), F32),
                   jax.ShapeDtypeStruct((n_sm, t), F32), jax.ShapeDtypeStruct((t, n_qkv), BF16),
                   jax.ShapeDtypeStruct((t, n_z), BF16)],
        compiler_params=_params("arbitrary"),
        name="inproj",
    )(x2d, g, w_lat, w_sm, w_smt, w_qkv, w_z)


def _mla_up_kernel(lat_ref, smt_ref, gq_ref, gkv_ref, wq_ref, wqs_ref, wkt_ref, wv_ref,
                   cq_ref, sq_ref, ckt_ref, skt_ref, q_ref, kt_ref, v_ref):
    lat = lat_ref[...].astype(F32)
    cqn = _rms(lat[:, :Q_LORA], gq_ref[...]).astype(BF16)
    ckvn = _rms(lat[:, Q_LORA:], gkv_ref[...]).astype(BF16)
    q_all = _dot(cqn, wq_ref[...])
    q_sw = _dot(cqn, wqs_ref[...])
    cq, sq = cq_ref[...], sq_ref[...]
    for h in range(MLA_HEADS):
        lo = h * QK_PAD
        q_ref[:, lo:lo + LANE] = q_all[:, lo:lo + LANE].astype(BF16)
        pe = q_all[:, lo + LANE:lo + QK_PAD] * cq + q_sw[:, h * LANE:(h + 1) * LANE] * sq
        q_ref[:, lo + LANE:lo + QK_PAD] = pe.astype(BF16)
    knt = _dot_nt(wkt_ref[...], ckvn)
    half = QK_ROPE // 2
    x1 = smt_ref[0:half, :]
    x2 = smt_ref[half:QK_ROPE, :]
    ckt, skt = ckt_ref[...], skt_ref[...]
    r1 = (x1 * ckt - x2 * skt).astype(BF16)
    r2 = (x2 * ckt + x1 * skt).astype(BF16)
    zero = jnp.zeros((QK_PAD - QK_NOPE - QK_ROPE, r1.shape[1]), BF16)
    for h in range(MLA_HEADS):
        kt_ref[0, h, 0, 0:QK_NOPE, :] = knt[h * QK_NOPE:(h + 1) * QK_NOPE, :].astype(BF16)
        kt_ref[0, h, 0, QK_NOPE:QK_NOPE + half, :] = r1
        kt_ref[0, h, 0, QK_NOPE + half:QK_NOPE + QK_ROPE, :] = r2
        kt_ref[0, h, 0, QK_NOPE + QK_ROPE:, :] = zero
    v_ref[...] = _dot(ckvn, wv_ref[...]).astype(BF16)


def _mla_up(lat, smt, gq, gkv, wq, wqs, wkt, wv, cq, sq, ckt, skt, b, s, ts):
    t = b * s
    nj = s // ts
    hq = MLA_HEADS * QK_PAD
    return pl.pallas_call(
        _mla_up_kernel,
        grid=(b, nj),
        in_specs=[pl.BlockSpec((ts, lat.shape[1]), lambda bi, j: (bi * nj + j, 0)),
                  pl.BlockSpec((smt.shape[0], ts), lambda bi, j: (0, bi * nj + j)),
                  _resident(gq.shape), _resident(gkv.shape), _resident(wq.shape), _resident(wqs.shape),
                  _resident(wkt.shape), _resident(wv.shape),
                  pl.BlockSpec((ts, LANE), lambda bi, j: (j, 0)),
                  pl.BlockSpec((ts, LANE), lambda bi, j: (j, 0)),
                  pl.BlockSpec((QK_ROPE // 2, ts), lambda bi, j: (0, j)),
                  pl.BlockSpec((QK_ROPE // 2, ts), lambda bi, j: (0, j))],
        out_specs=[pl.BlockSpec((ts, hq), lambda bi, j: (bi * nj + j, 0)),
                   pl.BlockSpec((1, MLA_HEADS, 1, QK_PAD, ts), lambda bi, j: (bi, 0, j, 0, 0)),
                   pl.BlockSpec((ts, D_ATTN), lambda bi, j: (bi * nj + j, 0))],
        out_shape=[jax.ShapeDtypeStruct((t, hq), BF16),
                   jax.ShapeDtypeStruct((b, MLA_HEADS, nj, QK_PAD, ts), BF16),
                   jax.ShapeDtypeStruct((t, D_ATTN), BF16)],
        compiler_params=_params("arbitrary", "arbitrary"),
        name="mla_up",
    )(lat, smt, gq, gkv, wq, wqs, wkt, wv, cq, sq, ckt, skt)


def _attn_kernel(q_ref, kt_ref, v_ref, o_ref, m_ref, l_ref, acc_ref, *, nk, tk):
    q = q_ref[...]
    m_ref[...] = jnp.full(m_ref.shape, -jnp.inf, F32)
    l_ref[...] = jnp.zeros(l_ref.shape, F32)
    acc_ref[...] = jnp.zeros(acc_ref.shape, F32)

    def body(j, carry):
        s = _dot(q, kt_ref[0, 0, j])
        m_old = m_ref[...]
        m_new = jnp.maximum(m_old, jnp.max(s, axis=-1, keepdims=True))
        alpha = jnp.exp2(m_old - m_new)
        p = jnp.exp2(s - m_new)
        l_ref[...] = alpha * l_ref[...] + jnp.sum(p, axis=-1, keepdims=True)
        v = v_ref[pl.ds(pl.multiple_of(j * tk, tk), tk), :]
        acc_ref[...] = alpha * acc_ref[...] + _dot(p.astype(BF16), v)
        m_ref[...] = m_new
        return carry

    lax.fori_loop(0, nk, body, 0)
    o_ref[...] = (acc_ref[...] / l_ref[...]).astype(o_ref.dtype)


def _attention(q, kt, v, b, s, tq):
    nk, tk = kt.shape[2], kt.shape[4]
    nq = s // tq
    return pl.pallas_call(
        functools.partial(_attn_kernel, nk=nk, tk=tk),
        grid=(b, MLA_HEADS, nq),
        in_specs=[pl.BlockSpec((tq, QK_PAD), lambda bi, h, i: (bi * nq + i, h)),
                  pl.BlockSpec((1, 1, nk, QK_PAD, tk), lambda bi, h, i: (bi, h, 0, 0, 0)),
                  pl.BlockSpec((s, V_HEAD), lambda bi, h, i: (bi, h))],
        out_specs=pl.BlockSpec((tq, V_HEAD), lambda bi, h, i: (bi * nq + i, h)),
        out_shape=jax.ShapeDtypeStruct((b * s, D_ATTN), BF16),
        scratch_shapes=[pltpu.VMEM((tq, 1), F32), pltpu.VMEM((tq, 1), F32), pltpu.VMEM((tq, V_HEAD), F32)],
        compiler_params=_params("arbitrary", "arbitrary", "arbitrary"),
        name="attention",
    )(q, kt, v)


def _rope_tables(s):
    inv_freq = ROPE_THETA ** (-jnp.arange(0, QK_ROPE, 2, dtype=F32) / QK_ROPE)
    ang = jnp.arange(s, dtype=F32)[:, None] * inv_freq[None, :]
    return jnp.cos(ang), jnp.sin(ang)


def _prep_mla_weights(w_uq, w_ukv):
    half = QK_ROPE // 2
    scale = (QK_NOPE + QK_ROPE) ** -0.5 * math.log2(math.e)
    wq = (w_uq * scale).reshape(Q_LORA, MLA_HEADS, QK_NOPE + QK_ROPE)
    pad = jnp.zeros((Q_LORA, MLA_HEADS, QK_PAD - QK_NOPE - QK_ROPE), F32)
    wq_pad = jnp.concatenate([wq, pad], axis=-1).reshape(Q_LORA, MLA_HEADS * QK_PAD)
    x1, x2 = wq[..., QK_NOPE:QK_NOPE + half], wq[..., QK_NOPE + half:]
    wq_sw = jnp.concatenate([x2, x1, jnp.zeros((Q_LORA, MLA_HEADS, LANE - QK_ROPE), F32)], axis=-1)
    wq_sw = wq_sw.reshape(Q_LORA, MLA_HEADS * LANE)
    wkv = w_ukv.reshape(KV_LORA, MLA_HEADS, QK_NOPE + V_HEAD)
    wkt = wkv[..., :QK_NOPE].reshape(KV_LORA, MLA_HEADS * QK_NOPE).T
    wv = wkv[..., QK_NOPE:].reshape(KV_LORA, MLA_HEADS * V_HEAD)
    return wq_pad.astype(BF16), wq_sw.astype(BF16), wkt.astype(BF16), wv.astype(BF16)


def _mla(lat, smt, g_q, g_kv, w_uq, w_ukv, b, s):
    ts = min(512, s)
    tq = min(512, s)
    wq, wqs, wkt, wv = _prep_mla_weights(w_uq, w_ukv)
    cos, sin = _rope_tables(s)
    zeros = jnp.zeros((s, LANE - QK_ROPE), F32)
    cq = jnp.concatenate([cos, cos, zeros], axis=-1)
    sq = jnp.concatenate([-sin, sin, zeros], axis=-1)
    q, kt, v = _mla_up(lat, smt, g_q.reshape(1, -1), g_kv.reshape(1, -1), wq, wqs, wkt, wv,
                       cq, sq, cos.T, sin.T, b, s, ts)
    return _attention(q, kt, v, b, s, tq)


HALO = 16
GATE_A = QK_ROPE
GATE_BT = QK_ROPE + 2 * GDN_HEADS
NG = 2 * GDN_HEADS


def _chunk_masks(n, rev):
    ri = lax.broadcasted_iota(jnp.int32, (n, n), 0)
    ci = lax.broadcasted_iota(jnp.int32, (n, n), 1)
    same = (ri // CHUNK) == (ci // CHUNK)
    if rev:
        return same, same & (ri <= ci), same & (ri < ci), ri == ci
    return same, same & (ri >= ci), same & (ri > ci), ri == ci


def _dot_exact(a, b):
    return jnp.dot(a, b, preferred_element_type=F32, precision=lax.Precision.HIGHEST)


def _dot_nt_exact(a, b):
    return lax.dot_general(a, b, (((1,), (1,)), ((), ())), preferred_element_type=F32,
                           precision=lax.Precision.HIGHEST)


def _softplus(x):
    return jnp.maximum(x, 0.0) + jnp.log(1.0 + jnp.exp(-jnp.abs(x)))


def _gdn_prep_kernel(qkv_ref, prev_ref, next_ref, sm_ref, smt_ref, cw_ref, alog_ref, dtb_ref,
                     alogt_ref, dtbt_ref, q_ref, k_ref, v_ref, kt_ref, gc_ref, beta_ref, gct_ref, glt_ref,
                     xs_ref, *, ts):
    j = pl.program_id(1)
    nj = pl.num_programs(1)
    xs_ref[0:HALO, :] = jnp.where(j > 0, prev_ref[...].astype(F32), 0.0)
    xs_ref[HALO:HALO + ts, :] = qkv_ref[...].astype(F32)
    xs_ref[HALO + ts:, :] = jnp.where(j < nj - 1, next_ref[...].astype(F32), 0.0)
    acc = None
    for tap in range(CONV_K):
        lo = HALO - CONV_K // 2 + tap
        term = xs_ref[lo:lo + ts, :] * cw_ref[tap:tap + 1, :]
        acc = term if acc is None else acc + term
    act = acc * jax.nn.sigmoid(acc)
    eye = (lax.broadcasted_iota(jnp.int32, (GDN_DK, GDN_DK), 0)
           == lax.broadcasted_iota(jnp.int32, (GDN_DK, GDN_DK), 1)).astype(BF16)
    hk = GDN_HEADS * GDN_DK
    for h in range(GDN_HEADS):
        qh = act[:, h * GDN_DK:(h + 1) * GDN_DK]
        kh = act[:, hk + h * GDN_DK:hk + (h + 1) * GDN_DK]
        qh = qh * (lax.rsqrt(jnp.sum(qh * qh, axis=-1, keepdims=True) + EPS) * GDN_DK ** -0.5)
        kh = (kh * lax.rsqrt(jnp.sum(kh * kh, axis=-1, keepdims=True) + EPS)).astype(BF16)
        q_ref[:, h * GDN_DK:(h + 1) * GDN_DK] = qh.astype(BF16)
        k_ref[:, h * GDN_DK:(h + 1) * GDN_DK] = kh
        kt_ref[h] = _dot_nt(eye, kh).astype(BF16)
    v_ref[...] = act[:, 2 * hk:].astype(BF16)

    sm = sm_ref[...]
    g = -jnp.exp(alog_ref[...]) * _softplus(sm[:, GATE_A:GATE_A + NG] + dtb_ref[...])
    beta_ref[...] = jax.nn.sigmoid(sm[:, GATE_BT:GATE_BT + NG])
    _, incl_f, _, _ = _chunk_masks(ts, False)
    _, incl_b, _, _ = _chunk_masks(ts, True)
    tri_f = incl_f.astype(F32)
    tri_b = incl_b.astype(F32)
    is_fwd = lax.broadcasted_iota(jnp.int32, (ts, NG), 1) < GDN_HEADS
    gc_ref[...] = jnp.where(is_fwd, _dot_exact(tri_f, g), _dot_exact(tri_b, g))
    smt = smt_ref[...]
    gt = -jnp.exp(alogt_ref[...]) * _softplus(smt[GATE_A:GATE_A + NG, :] + dtbt_ref[...])
    is_fwd_t = lax.broadcasted_iota(jnp.int32, (NG, ts), 0) < GDN_HEADS
    gct_ref[...] = jnp.where(is_fwd_t, _dot_nt_exact(gt, tri_f), _dot_nt_exact(gt, tri_b))
    same, _, _, _ = _chunk_masks(ts, False)
    glt_ref[...] = _dot_exact(gt, same.astype(F32))


def _gdn_prep(qkv, sm, smt, conv_w, a_log, dt_bias, b, s, ts):
    t = b * s
    nj = s // ts
    c = qkv.shape[1]
    hb = ts // HALO
    d = GDN_HEADS * GDN_DK
    tile = lambda n: pl.BlockSpec((ts, n), lambda bi, j: (bi * nj + j, 0))
    tile_t = lambda n: pl.BlockSpec((n, ts), lambda bi, j: (0, bi * nj + j))
    return pl.pallas_call(
        functools.partial(_gdn_prep_kernel, ts=ts),
        grid=(b, nj),
        in_specs=[tile(c),
                  pl.BlockSpec((HALO, c), lambda bi, j: (jnp.maximum((bi * nj + j) * hb - 1, 0), 0)),
                  pl.BlockSpec((HALO, c), lambda bi, j: (jnp.minimum((bi * nj + j + 1) * hb, t // HALO - 1), 0)),
                  tile(sm.shape[1]), tile_t(smt.shape[0]),
                  _resident(conv_w.shape), _resident((1, NG)), _resident((1, NG)),
                  _resident((NG, 1)), _resident((NG, 1))],
        out_specs=[tile(d), tile(d), tile(d),
                   pl.BlockSpec((GDN_HEADS, GDN_DK, ts), lambda bi, j: (0, 0, bi * nj + j)),
                   tile(NG), tile(NG), tile_t(NG), tile_t(NG)],
        out_shape=[jax.ShapeDtypeStruct((t, d), BF16)] * 3
        + [jax.ShapeDtypeStruct((GDN_HEADS, GDN_DK, t), BF16),
           jax.ShapeDtypeStruct((t, NG), F32), jax.ShapeDtypeStruct((t, NG), F32),
           jax.ShapeDtypeStruct((NG, t), F32), jax.ShapeDtypeStruct((NG, t), F32)],
        scratch_shapes=[pltpu.VMEM((ts + 2 * HALO, c), F32)],
        compiler_params=_params("arbitrary", "arbitrary"),
        name="gdn_prep",
    )(qkv, qkv, qkv, sm, smt, conv_w, a_log.reshape(1, NG), dt_bias.reshape(1, NG),
      a_log.reshape(NG, 1), dt_bias.reshape(NG, 1))


STEP = 2 * CHUNK


def _gdn_scan_kernel(q_ref, k_ref, v_ref, kt_ref, gc_ref, beta_ref, gct_ref, glt_ref, o_ref, s_ref, *, rev):
    @pl.when(pl.program_id(1) == 0)
    def _():
        s_ref[...] = jnp.zeros(s_ref.shape, F32)

    _, incl, strict, diag = _chunk_masks(STEP, rev)
    eye = diag.astype(F32)
    lane = lax.broadcasted_iota(jnp.int32, (1, STEP), 1)
    zeros = jnp.zeros((CHUNK, GDN_DV), F32)
    col0 = GDN_HEADS if rev else 0
    for h in range(GDN_HEADS):
        sl = slice(h * GDN_DK, (h + 1) * GDN_DK)
        q, k, v, kt = q_ref[:, sl], k_ref[:, sl], v_ref[:, sl], kt_ref[h]
        c = col0 + h
        gcol = gc_ref[:, c:c + 1]
        bcol = beta_ref[:, c:c + 1]
        grow = gct_ref[c:c + 1, :]
        glrow = glt_ref[c:c + 1, :]
        decay = jnp.exp(jnp.where(incl, gcol - grow, -jnp.inf))
        egc = jnp.exp(gcol)
        kb = k.astype(F32) * bcol
        aq = _dot(jnp.concatenate([kb.astype(BF16), q], axis=0), kt)
        nil = -jnp.where(strict, aq[:STEP] * decay, 0.0)
        intra = (aq[STEP:] * decay).astype(BF16)
        inv = eye + nil
        power = nil
        for _ in range(int(math.log2(CHUNK)) - 1):
            pb = power.astype(BF16)
            power = _dot(pb, pb)
            inv = inv + _dot(inv.astype(BF16), power.astype(BF16))
        rhs = jnp.concatenate([(v.astype(F32) * bcol).astype(BF16), (kb * egc).astype(BF16)], axis=1)
        uw = _dot(inv.astype(BF16), rhs)
        qd = (q.astype(F32) * egc).astype(BF16)
        kdt = (kt.astype(F32) * jnp.exp(glrow - grow)).astype(BF16)
        state = s_ref[h]
        for ch in ((1, 0) if rev else (0, 1)):
            r = slice(ch * CHUNK, (ch + 1) * CHUNK)
            wq = jnp.concatenate([uw[r, GDN_DV:].astype(BF16), qd[r]], axis=0)
            ws = _dot(wq, state.astype(BF16))
            v_new = uw[r, :GDN_DV] - ws[:CHUNK]
            vpad = jnp.concatenate([v_new, zeros] if ch == 0 else [zeros, v_new], axis=0).astype(BF16)
            res = _dot(jnp.concatenate([intra[r], kdt], axis=0), vpad)
            o_ref[r, sl] = (ws[CHUNK:] + res[:CHUNK]).astype(o_ref.dtype)
            gl = jnp.sum(jnp.where(lane == ch * CHUNK, glrow, 0.0), axis=-1, keepdims=True)
            state = state * jnp.exp(gl) + res[CHUNK:]
        s_ref[h] = state


def _gdn_scan(q, k, v, kt, gc, beta, gct, glt, b, s, rev):
    t = b * s
    n = s // STEP
    d = GDN_HEADS * GDN_DK
    if rev:
        blk = lambda bi, i: bi * n + (n - 1 - i)
    else:
        blk = lambda bi, i: bi * n + i
    tile = lambda w: pl.BlockSpec((STEP, w), lambda bi, i: (blk(bi, i), 0))
    tile_t = lambda w: pl.BlockSpec((w, STEP), lambda bi, i: (0, blk(bi, i)))
    return pl.pallas_call(
        functools.partial(_gdn_scan_kernel, rev=rev),
        grid=(b, n),
        in_specs=[tile(d), tile(d), tile(d),
                  pl.BlockSpec((GDN_HEADS, GDN_DK, STEP), lambda bi, i: (0, 0, blk(bi, i))),
                  tile(NG), tile(NG), tile_t(NG), tile_t(NG)],
        out_specs=tile(d),
        out_shape=jax.ShapeDtypeStruct((t, d), BF16),
        scratch_shapes=[pltpu.VMEM((GDN_HEADS, GDN_DK, GDN_DV), F32)],
        compiler_params=_params("arbitrary", "arbitrary"),
        name="gdn_scan_bwd" if rev else "gdn_scan_fwd",
    )(q, k, v, kt, gc, beta, gct, glt)


def _gdn(qkv, sm, smt, conv_w, a_log, dt_bias, b, s):
    ts = min(256, s)
    q, k, v, kt, gc, beta, gct, glt = _gdn_prep(qkv, sm, smt, conv_w, a_log, dt_bias, b, s, ts)
    o_f = _gdn_scan(q, k, v, kt, gc, beta, gct, glt, b, s, False)
    o_b = _gdn_scan(q, k, v, kt, gc, beta, gct, glt, b, s, True)
    return o_f, o_b


def _outproj_kernel(x_ref, att_ref, of_ref, ob_ref, z_ref, ga_ref, gg_ref, wo_ref, nf_ref, wr_ref, br_ref,
                    y_ref, yn_ref, lg_ref):
    att = _rms(att_ref[...].astype(F32), ga_ref[...]).astype(BF16)
    o = of_ref[...].astype(F32) + ob_ref[...].astype(F32)
    z = z_ref[...].astype(F32)
    gg = gg_ref[...]
    lin = []
    for h in range(GDN_HEADS):
        sl = slice(h * GDN_DV, (h + 1) * GDN_DV)
        zh = z[:, sl]
        lin.append((_rms(o[:, sl], gg) * (zh * jax.nn.sigmoid(zh))).astype(BF16))
    lin = jnp.concatenate(lin, axis=-1)
    y = x_ref[...] + _dot(att, wo_ref[:D_ATTN, :]) + _dot(lin, wo_ref[D_ATTN:, :])
    y_ref[...] = y
    yn = _rms(y, nf_ref[...]).astype(BF16)
    yn_ref[...] = yn
    lg_ref[...] = _dot(yn, wr_ref[...]) + br_ref[...]


def _outproj(x2d, att, o_f, o_b, z, g_attn, g_gdn, w_out, norm_ffn, w_r, b_r, tm):
    t, d = x2d.shape
    row = lambda n: pl.BlockSpec((tm, n), lambda i: (i, 0))
    return pl.pallas_call(
        _outproj_kernel,
        grid=(t // tm,),
        in_specs=[row(d), row(D_ATTN), row(D_GDN), row(D_GDN), row(D_GDN), _resident((1, D_ATTN)),
                  _resident((1, GDN_DV)), _resident(w_out.shape), _resident((1, d)), _resident(w_r.shape),
                  _resident((1, LANE))],
        out_specs=[row(d), row(d), row(LANE)],
        out_shape=[jax.ShapeDtypeStruct((t, d), F32), jax.ShapeDtypeStruct((t, d), BF16),
                   jax.ShapeDtypeStruct((t, LANE), F32)],
        compiler_params=_params("arbitrary"),
        name="outproj",
    )(x2d, att, o_f, o_b, z, g_attn.reshape(1, -1), g_gdn.reshape(1, -1), w_out, norm_ffn.reshape(1, -1),
      w_r, b_r)


def _router_kernel(lg_ref, eid_ref, gate_ref):
    lg = lg_ref[...]
    lane = lax.broadcasted_iota(jnp.int32, lg.shape, 1)
    neg = -jnp.inf

    def first_max(x):
        mx = jnp.max(x, axis=-1, keepdims=True)
        return mx, jnp.min(jnp.where(x == mx, lane, LANE), axis=-1, keepdims=True)

    gl = jnp.where(lane < N_GROUPS, lg, neg)
    gmax, grp = first_max(gl)
    grp_p = 1.0 / jnp.sum(jnp.exp(gl - gmax), axis=-1, keepdims=True)
    lo = N_GROUPS + grp * EXPERTS_PER_GROUP
    el = jnp.where((lane >= lo) & (lane < lo + EXPERTS_PER_GROUP), lg, neg)
    l1, i1 = first_max(el)
    l2, i2 = first_max(jnp.where(lane == i1, neg, el))
    e = jnp.exp(l2 - l1)
    g1 = grp_p / (1.0 + e)
    eid_ref[...] = jnp.where(lane == 0, i1 - N_GROUPS, jnp.where(lane == 1, i2 - N_GROUPS, 0))
    gate_ref[...] = jnp.where(lane == 0, g1, jnp.where(lane == 1, g1 * e, 0.0))


def _router(logits, tm):
    t = logits.shape[0]
    row = pl.BlockSpec((tm, LANE), lambda i: (i, 0))
    return pl.pallas_call(
        _router_kernel,
        grid=(t // tm,),
        in_specs=[row],
        out_specs=[row, row],
        out_shape=[jax.ShapeDtypeStruct((t, LANE), jnp.int32), jax.ShapeDtypeStruct((t, LANE), F32)],
        compiler_params=_params("arbitrary"),
        name="router",
    )(logits)


MOE_BLOCK = 256


def _moe_kernel(be_ref, nu_ref, xb_ref, gate_ref, wg_ref, wu_ref, wd_ref, yb_ref):
    @pl.when(pl.program_id(0) < nu_ref[0])
    def _():
        x = xb_ref[...]
        a = _dot(x, wg_ref[0])
        hdn = (a * jax.nn.sigmoid(a) * _dot(x, wu_ref[0])).astype(BF16)
        yb_ref[...] = (_dot(hdn, wd_ref[0]) * gate_ref[...]).astype(yb_ref.dtype)


def _moe_blocks(block_e, n_used, xb, gate_rows, w_gate, w_up, w_down):
    p, d = xb.shape
    nb = p // MOE_BLOCK
    clamp = lambda i, nu: jnp.minimum(i, nu[0] - 1)
    grid_spec = pltpu.PrefetchScalarGridSpec(
        num_scalar_prefetch=2,
        grid=(nb,),
        in_specs=[pl.BlockSpec((MOE_BLOCK, d), lambda i, be, nu: (clamp(i, nu), 0)),
                  pl.BlockSpec((MOE_BLOCK, 1), lambda i, be, nu: (clamp(i, nu), 0)),
                  pl.BlockSpec((1, d, D_EXPERT), lambda i, be, nu: (be[clamp(i, nu)], 0, 0)),
                  pl.BlockSpec((1, d, D_EXPERT), lambda i, be, nu: (be[clamp(i, nu)], 0, 0)),
                  pl.BlockSpec((1, D_EXPERT, d), lambda i, be, nu: (be[clamp(i, nu)], 0, 0))],
        out_specs=pl.BlockSpec((MOE_BLOCK, d), lambda i, be, nu: (clamp(i, nu), 0)),
    )
    return pl.pallas_call(
        _moe_kernel,
        grid_spec=grid_spec,
        out_shape=jax.ShapeDtypeStruct((p, d), BF16),
        compiler_params=_params("arbitrary"),
        name="moe_experts",
    )(block_e, n_used, xb, gate_rows, w_gate, w_up, w_down)


def _final_kernel(y_ref, e_ref, g_ref, o_ref):
    d = y_ref.shape[1]
    e = e_ref[...].astype(F32)
    o_ref[...] = _rms(y_ref[...] + e[:, :d] + e[:, d:], g_ref[...])


def _final(y, ye, g, tm):
    t, d = y.shape
    return pl.pallas_call(
        _final_kernel,
        grid=(t // tm,),
        in_specs=[pl.BlockSpec((tm, d), lambda i: (i, 0)), pl.BlockSpec((tm, TOP_K * d), lambda i: (i, 0)),
                  _resident((1, d))],
        out_specs=pl.BlockSpec((tm, d), lambda i: (i, 0)),
        out_shape=jax.ShapeDtypeStruct((t, d), F32),
        compiler_params=_params("arbitrary"),
        name="final_norm",
    )(y, ye, g.reshape(1, -1))


def _route_plan(eid):
    m = eid.shape[0]
    order = jnp.argsort(eid).astype(jnp.int32)
    counts = jnp.zeros((N_EXPERTS,), jnp.int32).at[eid].add(1)
    starts = jnp.cumsum(counts) - counts
    padded = (counts + MOE_BLOCK - 1) // MOE_BLOCK * MOE_BLOCK
    pend = jnp.cumsum(padded)
    pstart = pend - padded
    sorted_e = eid[order]
    dest = pstart[sorted_e] + (jnp.arange(m, dtype=jnp.int32) - starts[sorted_e])
    n_blocks = -(-m // MOE_BLOCK) + N_EXPERTS
    slot_at = jnp.full((n_blocks * MOE_BLOCK,), m, jnp.int32).at[dest].set(order)
    block_e = jnp.minimum(jnp.searchsorted(pend, jnp.arange(n_blocks, dtype=jnp.int32) * MOE_BLOCK, side='right'),
                          N_EXPERTS - 1).astype(jnp.int32)
    pos = jnp.zeros((m,), jnp.int32).at[order].set(dest)
    n_used = (pend[-1] // MOE_BLOCK).astype(jnp.int32).reshape(1)
    return slot_at, block_e, pos, n_used


def _encoder_front(x, p, wts):
    b, s, d = x.shape
    x2d = x.reshape(b * s, d)
    tm = min(512, b * s)
    lat, sm, smt, qkv, z = _inproj(x2d, p["norm_mix"].reshape(1, -1), wts["w_lat"], wts["w_sm"], wts["w_smt"],
                                   wts["w_qkv"], wts["w_z"], tm)
    att = _mla(lat, smt, p["g_q_lora"], p["g_kv_lora"], p["w_uq"], p["w_ukv"], b, s)
    o_f, o_b = _gdn(qkv, sm, smt, p["conv_w"], p["a_log"], p["dt_bias"], b, s)
    return _outproj(x2d, att, o_f, o_b, z, p["g_attn_out"], p["g_gdn_out"], wts["w_out"], p["norm_ffn"],
                    wts["w_r"], wts["b_r"], tm)


def _prep_weights(p):
    w_in = p["w_in"]
    o = np.cumsum([0, Q_LORA, KV_LORA, QK_ROPE, GDN_QKV, D_GDN, NG, NG])
    d = w_in.shape[0]
    w_sm = jnp.concatenate([w_in[:, o[2]:o[3]], w_in[:, o[5]:o[7]], jnp.zeros((d, LANE - QK_ROPE - 2 * NG), F32)],
                           axis=1).astype(BF16)
    w_r = jnp.concatenate([p["w_router_group"], p["w_router_expert"],
                           jnp.zeros((d, LANE - N_GROUPS - N_EXPERTS), F32)], axis=1).astype(BF16)
    b_r = jnp.concatenate([p["b_router_group"], p["b_router_expert"],
                           jnp.zeros((LANE - N_GROUPS - N_EXPERTS,), F32)]).reshape(1, LANE)
    return dict(w_lat=w_in[:, :o[2]].astype(BF16), w_sm=w_sm, w_smt=w_sm.T, w_qkv=w_in[:, o[3]:o[4]].astype(BF16),
                w_z=w_in[:, o[4]:o[5]].astype(BF16), w_out=p["w_out"].astype(BF16), w_r=w_r, b_r=b_r,
                w_gate=p["w_gate"].astype(BF16), w_up=p["w_up"].astype(BF16), w_down=p["w_down"].astype(BF16))


def _encode(xs, p):
    wts = _prep_weights(p)
    fronts = [_encoder_front(x, p, wts) for x in xs]
    yn = jnp.concatenate([f[1] for f in fronts], axis=0)
    logits = jnp.concatenate([f[2] for f in fronts], axis=0)
    t = yn.shape[0]
    eid_l, gate_l = _router(logits, min(512, t))
    eid = eid_l[:, :TOP_K].reshape(-1)
    gate = gate_l[:, :TOP_K].reshape(-1)
    slot_at, block_e, pos, n_used = _route_plan(eid)
    m = eid.shape[0]
    tok = jnp.minimum(slot_at, m - 1) // TOP_K
    xb = jnp.take(yn, tok, axis=0)
    gate_rows = jnp.where(slot_at < m, jnp.take(gate, jnp.minimum(slot_at, m - 1)), 0.0).reshape(-1, 1)
    yb = _moe_blocks(block_e, n_used, xb, gate_rows, wts["w_gate"], wts["w_up"], wts["w_down"])
    ye = jnp.take(yb, pos, axis=0).reshape(t, TOP_K * yb.shape[1])
    outs, t0 = [], 0
    for x, f in zip(xs, fronts):
        n = f[0].shape[0]
        out = _final(f[0], ye[t0:t0 + n], p["norm_final"], min(512, n))
        outs.append(out.reshape(x.shape))
        t0 += n
    return outs


def kernel(x_prompt, x_sample, norm_mix, w_in, g_q_lora, g_kv_lora, w_uq, w_ukv, g_attn_out, conv_w, a_log,
           dt_bias, g_gdn_out, w_out, norm_ffn, w_router_group, b_router_group, w_router_expert,
           b_router_expert, w_gate, w_up, w_down, norm_final):
    p = dict(norm_mix=norm_mix[0], w_in=w_in[0], g_q_lora=g_q_lora[0], g_kv_lora=g_kv_lora[0], w_uq=w_uq[0],
             w_ukv=w_ukv[0], g_attn_out=g_attn_out[0], conv_w=conv_w[0], a_log=a_log[0], dt_bias=dt_bias[0],
             g_gdn_out=g_gdn_out[0], w_out=w_out[0], norm_ffn=norm_ffn[0], w_router_group=w_router_group[0],
             b_router_group=b_router_group[0], w_router_expert=w_router_expert[0],
             b_router_expert=b_router_expert[0], w_gate=w_gate[0], w_up=w_up[0], w_down=w_down[0],
             norm_final=norm_final)
    y_prompt, y_sample = _encode([x_prompt, x_sample], p)
    return (y_prompt, y_sample)
```

```python
import functools
import math

import jax
import jax.numpy as jnp
import numpy as np
from jax import lax
from jax.experimental import pallas as pl
from jax.experimental.pallas import tpu as pltpu

F32 = jnp.float32
BF16 = jnp.bfloat16

D_MODEL = 2048
MLA_HEADS = 8
Q_LORA = 512
KV_LORA = 512
QK_NOPE = 128
QK_ROPE = 64
V_HEAD = 128
ROPE_THETA = 10000.0
GDN_HEADS = 8
GDN_DK = 128
GDN_DV = 128
GDN_QKV = GDN_HEADS * (2 * GDN_DK + GDN_DV)
CONV_K = 5
CHUNK = 64
D_ATTN = MLA_HEADS * V_HEAD
D_GDN = GDN_HEADS * GDN_DV
N_GROUPS = 8
EXPERTS_PER_GROUP = 8
N_EXPERTS = N_GROUPS * EXPERTS_PER_GROUP
TOP_K = 2
D_EXPERT = 512
EPS = 1e-6

LANE = 128
QK_PAD = 256
VMEM_LIMIT = 56 * 1024 * 1024


def _params(*sem):
    return pltpu.CompilerParams(dimension_semantics=sem, vmem_limit_bytes=VMEM_LIMIT)


def _resident(shape):
    return pl.BlockSpec(shape, lambda *_: (0,) * len(shape), pipeline_mode=pl.Buffered(1))


def _rms(x, g):
    return x * lax.rsqrt(jnp.mean(x * x, axis=-1, keepdims=True) + EPS) * g


def _dot(a, b):
    return jnp.dot(a, b, preferred_element_type=F32)


def _dot_nt(a, b):
    return lax.dot_general(a, b, (((1,), (1,)), ((), ())), preferred_element_type=F32)


def _inproj_kernel(x_ref, g_ref, w_lat_ref, w_sm_ref, w_smt_ref, w_qkv_ref, w_z_ref,
                   lat_ref, sm_ref, smt_ref, qkv_ref, z_ref):
    xn = _rms(x_ref[...], g_ref[...]).astype(BF16)
    lat_ref[...] = _dot(xn, w_lat_ref[...]).astype(BF16)
    sm_ref[...] = _dot(xn, w_sm_ref[...])
    smt_ref[...] = _dot_nt(w_smt_ref[...], xn)
    qkv_ref[...] = _dot(xn, w_qkv_ref[...]).astype(BF16)
    z_ref[...] = _dot(xn, w_z_ref[...]).astype(BF16)


def _inproj(x2d, g, w_lat, w_sm, w_smt, w_qkv, w_z, tm):
    t, d = x2d.shape
    n_lat, n_sm, n_qkv, n_z = w_lat.shape[1], w_sm.shape[1], w_qkv.shape[1], w_z.shape[1]
    n_smt = w_smt.shape[0]
    row = lambda n: pl.BlockSpec((tm, n), lambda i: (i, 0))
    return pl.pallas_call(
        _inproj_kernel,
        grid=(t // tm,),
        in_specs=[row(d), _resident((1, d)), _resident(w_lat.shape), _resident(w_sm.shape),
                  _resident(w_smt.shape), _resident(w_qkv.shape), _resident(w_z.shape)],
        out_specs=[row(n_lat), row(n_sm), pl.BlockSpec((n_smt, tm), lambda i: (0, i)), row(n_qkv), row(n_z)],
        out_shape=[jax.ShapeDtypeStruct((t, n_lat), BF16), jax.ShapeDtypeStruct((t, n_sm), F32),
                   jax.ShapeDtypeStruct((n_smt, t), F32), jax.ShapeDtypeStruct((t, n_qkv), BF16),
                   jax.ShapeDtypeStruct((t, n_z), BF16)],
        compiler_params=_params("arbitrary"),
        name="inproj",
    )(x2d, g, w_lat, w_sm, w_smt, w_qkv, w_z)


def _mla_up_kernel(lat_ref, sm_ref, gq_ref, gkv_ref, wqt_ref, wk_ref, wvt_ref,
                   cos_t_ref, sin_t_ref, ck_ref, sk_ref, qt_ref, k_ref, vt_ref):
    lat = lat_ref[...].astype(F32)
    cqn = _rms(lat[:, :Q_LORA], gq_ref[...]).astype(BF16)
    ckvn = _rms(lat[:, Q_LORA:], gkv_ref[...]).astype(BF16)
    half = QK_ROPE // 2
    dqk = QK_NOPE + QK_ROPE
    qt = _dot_nt(wqt_ref[...], cqn)
    cos_t, sin_t = cos_t_ref[...], sin_t_ref[...]
    zero = jnp.zeros((QK_PAD - dqk, qt.shape[1]), BF16)
    for h in range(MLA_HEADS):
        lo = h * dqk
        x1 = qt[lo + QK_NOPE:lo + QK_NOPE + half, :]
        x2 = qt[lo + QK_NOPE + half:lo + dqk, :]
        qt_ref[0, h, 0, 0:QK_NOPE, :] = qt[lo:lo + QK_NOPE, :].astype(BF16)
        qt_ref[0, h, 0, QK_NOPE:QK_NOPE + half, :] = (x1 * cos_t - x2 * sin_t).astype(BF16)
        qt_ref[0, h, 0, QK_NOPE + half:dqk, :] = (x2 * cos_t + x1 * sin_t).astype(BF16)
        qt_ref[0, h, 0, dqk:, :] = zero
    kn = _dot(ckvn, wk_ref[...])
    sm = sm_ref[...]
    pe = (sm[:, :LANE] * ck_ref[...] + sm[:, LANE:] * sk_ref[...]).astype(BF16)
    for h in range(MLA_HEADS):
        k_ref[:, h * QK_PAD:h * QK_PAD + QK_NOPE] = kn[:, h * QK_NOPE:(h + 1) * QK_NOPE].astype(BF16)
        k_ref[:, h * QK_PAD + QK_NOPE:(h + 1) * QK_PAD] = pe
    vt = _dot_nt(wvt_ref[...], ckvn)
    for h in range(MLA_HEADS):
        vt_ref[0, h, 0, 0:V_HEAD, :] = vt[h * V_HEAD:(h + 1) * V_HEAD, :].astype(BF16)
        vt_ref[0, h, 0, V_HEAD:, :] = jnp.ones((V_ROWS - V_HEAD, vt.shape[1]), BF16)


def _mla_up(lat, sm, gq, gkv, wqt, wk, wvt, cos_t, sin_t, ck, sk, b, s, ts):
    t = b * s
    nj = s // ts
    hk = MLA_HEADS * QK_PAD
    return pl.pallas_call(
        _mla_up_kernel,
        grid=(b, nj),
        in_specs=[pl.BlockSpec((ts, lat.shape[1]), lambda bi, j: (bi * nj + j, 0)),
                  pl.BlockSpec((ts, sm.shape[1]), lambda bi, j: (bi * nj + j, 0)),
                  _resident(gq.shape), _resident(gkv.shape), _resident(wqt.shape), _resident(wk.shape),
                  _resident(wvt.shape),
                  pl.BlockSpec((QK_ROPE // 2, ts), lambda bi, j: (0, j)),
                  pl.BlockSpec((QK_ROPE // 2, ts), lambda bi, j: (0, j)),
                  pl.BlockSpec((ts, LANE), lambda bi, j: (j, 0)),
                  pl.BlockSpec((ts, LANE), lambda bi, j: (j, 0))],
        out_specs=[pl.BlockSpec((1, MLA_HEADS, 1, QK_PAD, ts), lambda bi, j: (bi, 0, j, 0, 0)),
                   pl.BlockSpec((ts, hk), lambda bi, j: (bi * nj + j, 0)),
                   pl.BlockSpec((1, MLA_HEADS, 1, V_ROWS, ts), lambda bi, j: (bi, 0, j, 0, 0))],
        out_shape=[jax.ShapeDtypeStruct((b, MLA_HEADS, nj, QK_PAD, ts), BF16),
                   jax.ShapeDtypeStruct((t, hk), BF16),
                   jax.ShapeDtypeStruct((b, MLA_HEADS, nj, V_ROWS, ts), BF16)],
        compiler_params=_params("arbitrary", "arbitrary"),
        name="mla_up",
    )(lat, sm, gq, gkv, wqt, wk, wvt, cos_t, sin_t, ck, sk)


V_ROWS = V_HEAD + 16


def _attn_kernel(qt_ref, k_ref, vt_ref, o_ref, m_ref, acc_ref, s_buf, p_buf, a_buf, *, nk, tk, unroll):
    qt = qt_ref[0, 0, 0]
    m_ref[...] = jnp.full(m_ref.shape, -jnp.inf, F32)
    acc_ref[...] = jnp.zeros(acc_ref.shape, F32)
    p_buf[1] = jnp.zeros(p_buf.shape[1:], BF16)
    a_buf[1] = jnp.ones(a_buf.shape[1:], F32)
    s_buf[0] = _dot(k_ref[0:tk, :], qt)

    def step(j, cur, nxt):
        jp = jnp.maximum(j - 1, 0)
        acc_ref[...] = a_buf[nxt] * acc_ref[...] + _dot(vt_ref[0, 0, jp], p_buf[nxt])
        jn = jnp.minimum(j + 1, nk - 1)
        s_buf[nxt] = _dot(k_ref[pl.ds(pl.multiple_of(jn * tk, tk), tk), :], qt)
        s = s_buf[cur]
        m_old = m_ref[...]
        m_new = jnp.maximum(m_old, jnp.max(s, axis=0, keepdims=True))
        a_buf[cur] = jnp.exp2(m_old - m_new)
        p_buf[cur] = jnp.exp2(s - m_new).astype(BF16)
        m_ref[...] = m_new

    def body(jj, carry):
        for u in range(unroll):
            step(unroll * jj + u, u % 2, 1 - u % 2)
        return carry

    lax.fori_loop(0, nk // unroll, body, 0)
    acc = a_buf[1] * acc_ref[...] + _dot(vt_ref[0, 0, nk - 1], p_buf[1])
    o_ref[...] = (acc[:V_HEAD] / acc[V_HEAD:V_HEAD + 1]).T.astype(o_ref.dtype)


def _attention(qt, k, vt, b, s):
    nq, tq = qt.shape[2], qt.shape[4]
    nk, tk = vt.shape[2], vt.shape[4]
    assert nk % 2 == 0, "the two-stage buffers alternate statically"
    unroll = 4 if nk % 4 == 0 else 2
    return pl.pallas_call(
        functools.partial(_attn_kernel, nk=nk, tk=tk, unroll=unroll),
        grid=(b, MLA_HEADS, nq),
        in_specs=[pl.BlockSpec((1, 1, 1, QK_PAD, tq), lambda bi, h, i: (bi, h, i, 0, 0)),
                  pl.BlockSpec((s, QK_PAD), lambda bi, h, i: (bi, h)),
                  pl.BlockSpec((1, 1, nk, V_ROWS, tk), lambda bi, h, i: (bi, h, 0, 0, 0))],
        out_specs=pl.BlockSpec((tq, V_HEAD), lambda bi, h, i: (bi * nq + i, h)),
        out_shape=jax.ShapeDtypeStruct((b * s, D_ATTN), BF16),
        scratch_shapes=[pltpu.VMEM((1, tq), F32), pltpu.VMEM((V_ROWS, tq), F32),
                        pltpu.VMEM((2, tk, tq), F32), pltpu.VMEM((2, tk, tq), BF16), pltpu.VMEM((2, 1, tq), F32)],
        compiler_params=_params("arbitrary", "arbitrary", "arbitrary"),
        name="attention",
    )(qt, k, vt)


def _rope_tables(s):
    inv_freq = ROPE_THETA ** (-jnp.arange(0, QK_ROPE, 2, dtype=F32) / QK_ROPE)
    ang = jnp.arange(s, dtype=F32)[:, None] * inv_freq[None, :]
    return jnp.cos(ang), jnp.sin(ang)


def _prep_mla_weights(w_uq, w_ukv):
    half = QK_ROPE // 2
    scale = (QK_NOPE + QK_ROPE) ** -0.5 * math.log2(math.e)
    wqt = (w_uq * scale).T
    wkv = w_ukv.reshape(KV_LORA, MLA_HEADS, QK_NOPE + V_HEAD)
    wk = wkv[..., :QK_NOPE].reshape(KV_LORA, MLA_HEADS * QK_NOPE)
    wvt = wkv[..., QK_NOPE:].reshape(KV_LORA, MLA_HEADS * V_HEAD).T
    return wqt.astype(BF16), wk.astype(BF16), wvt.astype(BF16)


def _mla(lat, sm, g_q, g_kv, w_uq, w_ukv, b, s):
    ts = min(512, s)
    wqt, wk, wvt = _prep_mla_weights(w_uq, w_ukv)
    cos, sin = _rope_tables(s)
    zeros = jnp.zeros((s, LANE - QK_ROPE), F32)
    ck = jnp.concatenate([cos, cos, zeros], axis=-1)
    sk = jnp.concatenate([-sin, sin, zeros], axis=-1)
    qt, k, vt = _mla_up(lat, sm, g_q.reshape(1, -1), g_kv.reshape(1, -1), wqt, wk, wvt,
                        cos.T, sin.T, ck, sk, b, s, ts)
    return _attention(qt, k, vt, b, s)


HALO = 16
GATE_A = QK_ROPE
GATE_BT = QK_ROPE + 2 * GDN_HEADS
NG = 2 * GDN_HEADS


def _chunk_masks(n, rev):
    ri = lax.broadcasted_iota(jnp.int32, (n, n), 0)
    ci = lax.broadcasted_iota(jnp.int32, (n, n), 1)
    same = (ri // CHUNK) == (ci // CHUNK)
    if rev:
        return same, same & (ri <= ci), same & (ri < ci), ri == ci
    return same, same & (ri >= ci), same & (ri > ci), ri == ci


def _dot_exact(a, b):
    return jnp.dot(a, b, preferred_element_type=F32, precision=lax.Precision.HIGHEST)


def _dot_nt_exact(a, b):
    return lax.dot_general(a, b, (((1,), (1,)), ((), ())), preferred_element_type=F32,
                           precision=lax.Precision.HIGHEST)


def _softplus(x):
    return jnp.maximum(x, 0.0) + jnp.log(1.0 + jnp.exp(-jnp.abs(x)))


def _gdn_prep_kernel(qkv_ref, prev_ref, next_ref, sm_ref, smt_ref, cw_ref, alog_ref, dtb_ref,
                     alogt_ref, dtbt_ref, q_ref, k_ref, v_ref, kt_ref, gc_ref, beta_ref, gct_ref, glt_ref,
                     xs_ref, *, ts):
    j = pl.program_id(1)
    nj = pl.num_programs(1)
    xs_ref[0:HALO, :] = jnp.where(j > 0, prev_ref[...].astype(F32), 0.0)
    xs_ref[HALO:HALO + ts, :] = qkv_ref[...].astype(F32)
    xs_ref[HALO + ts:, :] = jnp.where(j < nj - 1, next_ref[...].astype(F32), 0.0)
    acc = None
    for tap in range(CONV_K):
        lo = HALO - CONV_K // 2 + tap
        term = xs_ref[lo:lo + ts, :] * cw_ref[tap:tap + 1, :]
        acc = term if acc is None else acc + term
    act = acc * jax.nn.sigmoid(acc)
    eye = (lax.broadcasted_iota(jnp.int32, (GDN_DK, GDN_DK), 0)
           == lax.broadcasted_iota(jnp.int32, (GDN_DK, GDN_DK), 1)).astype(BF16)
    hk = GDN_HEADS * GDN_DK
    for h in range(GDN_HEADS):
        qh = act[:, h * GDN_DK:(h + 1) * GDN_DK]
        kh = act[:, hk + h * GDN_DK:hk + (h + 1) * GDN_DK]
        qh = qh * (lax.rsqrt(jnp.sum(qh * qh, axis=-1, keepdims=True) + EPS) * GDN_DK ** -0.5)
        kh = (kh * lax.rsqrt(jnp.sum(kh * kh, axis=-1, keepdims=True) + EPS)).astype(BF16)
        q_ref[:, h * GDN_DK:(h + 1) * GDN_DK] = qh.astype(BF16)
        k_ref[:, h * GDN_DK:(h + 1) * GDN_DK] = kh
        kt_ref[h] = _dot_nt(eye, kh).astype(BF16)
    v_ref[...] = act[:, 2 * hk:].astype(BF16)

    sm = sm_ref[...]
    g = -jnp.exp(alog_ref[...]) * _softplus(sm[:, GATE_A:GATE_A + NG] + dtb_ref[...])
    beta_ref[...] = jax.nn.sigmoid(sm[:, GATE_BT:GATE_BT + NG])
    _, incl_f, _, _ = _chunk_masks(ts, False)
    _, incl_b, _, _ = _chunk_masks(ts, True)
    tri_f = incl_f.astype(F32)
    tri_b = incl_b.astype(F32)
    is_fwd = lax.broadcasted_iota(jnp.int32, (ts, NG), 1) < GDN_HEADS
    gc_ref[...] = jnp.where(is_fwd, _dot_exact(tri_f, g), _dot_exact(tri_b, g))
    smt = smt_ref[...]
    gt = -jnp.exp(alogt_ref[...]) * _softplus(smt[GATE_A:GATE_A + NG, :] + dtbt_ref[...])
    is_fwd_t = lax.broadcasted_iota(jnp.int32, (NG, ts), 0) < GDN_HEADS
    gct_ref[...] = jnp.where(is_fwd_t, _dot_nt_exact(gt, tri_f), _dot_nt_exact(gt, tri_b))
    same, _, _, _ = _chunk_masks(ts, False)
    glt_ref[...] = _dot_exact(gt, same.astype(F32))


def _gdn_prep(qkv, sm, smt, conv_w, a_log, dt_bias, b, s, ts):
    t = b * s
    nj = s // ts
    c = qkv.shape[1]
    hb = ts // HALO
    d = GDN_HEADS * GDN_DK
    tile = lambda n: pl.BlockSpec((ts, n), lambda bi, j: (bi * nj + j, 0))
    tile_t = lambda n: pl.BlockSpec((n, ts), lambda bi, j: (0, bi * nj + j))
    return pl.pallas_call(
        functools.partial(_gdn_prep_kernel, ts=ts),
        grid=(b, nj),
        in_specs=[tile(c),
                  pl.BlockSpec((HALO, c), lambda bi, j: (jnp.maximum((bi * nj + j) * hb - 1, 0), 0)),
                  pl.BlockSpec((HALO, c), lambda bi, j: (jnp.minimum((bi * nj + j + 1) * hb, t // HALO - 1), 0)),
                  tile(sm.shape[1]), tile_t(smt.shape[0]),
                  _resident(conv_w.shape), _resident((1, NG)), _resident((1, NG)),
                  _resident((NG, 1)), _resident((NG, 1))],
        out_specs=[tile(d), tile(d), tile(d),
                   pl.BlockSpec((GDN_HEADS, GDN_DK, ts), lambda bi, j: (0, 0, bi * nj + j)),
                   tile(NG), tile(NG), tile_t(NG), tile_t(NG)],
        out_shape=[jax.ShapeDtypeStruct((t, d), BF16)] * 3
        + [jax.ShapeDtypeStruct((GDN_HEADS, GDN_DK, t), BF16),
           jax.ShapeDtypeStruct((t, NG), F32), jax.ShapeDtypeStruct((t, NG), F32),
           jax.ShapeDtypeStruct((NG, t), F32), jax.ShapeDtypeStruct((NG, t), F32)],
        scratch_shapes=[pltpu.VMEM((ts + 2 * HALO, c), F32)],
        compiler_params=_params("arbitrary", "arbitrary"),
        name="gdn_prep",
    )(qkv, qkv, qkv, sm, smt, conv_w, a_log.reshape(1, NG), dt_bias.reshape(1, NG),
      a_log.reshape(NG, 1), dt_bias.reshape(NG, 1))


STEP = 2 * CHUNK


def _gdn_scan_kernel(q_ref, k_ref, v_ref, kt_ref, gc_ref, beta_ref, gct_ref, glt_ref, o_ref, s_ref, *, rev):
    @pl.when(pl.program_id(1) == 0)
    def _():
        s_ref[...] = jnp.zeros(s_ref.shape, F32)

    _, incl, strict, diag = _chunk_masks(STEP, rev)
    eye = diag.astype(F32)
    lane = lax.broadcasted_iota(jnp.int32, (1, STEP), 1)
    zeros = jnp.zeros((CHUNK, GDN_DV), F32)
    col0 = GDN_HEADS if rev else 0
    heads = range(GDN_HEADS)
    sls = [slice(h * GDN_DK, (h + 1) * GDN_DK) for h in heads]
    q = [q_ref[:, sl] for sl in sls]
    kt = [kt_ref[h] for h in heads]
    gcol = [gc_ref[:, col0 + h:col0 + h + 1] for h in heads]
    bcol = [beta_ref[:, col0 + h:col0 + h + 1] for h in heads]
    grow = [gct_ref[col0 + h:col0 + h + 1, :] for h in heads]
    glrow = [glt_ref[col0 + h:col0 + h + 1, :] for h in heads]
    decay = [jnp.exp(jnp.where(incl, gcol[h] - grow[h], -jnp.inf)) for h in heads]
    egc = [jnp.exp(gcol[h]) for h in heads]
    kb = [k_ref[:, sls[h]].astype(F32) * bcol[h] for h in heads]
    aq = [_dot(jnp.concatenate([kb[h].astype(BF16), q[h]], axis=0), kt[h]) for h in heads]
    power = [-jnp.where(strict, aq[h][:STEP] * decay[h], 0.0) for h in heads]
    intra = [(aq[h][STEP:] * decay[h]).astype(BF16) for h in heads]
    inv = [eye + power[h] for h in heads]
    for _ in range(int(math.log2(CHUNK)) - 1):
        pb = [power[h].astype(BF16) for h in heads]
        power = [_dot(pb[h], pb[h]) for h in heads]
        inv = [inv[h] + _dot(inv[h].astype(BF16), power[h].astype(BF16)) for h in heads]
    rhs = [jnp.concatenate([(v_ref[:, sls[h]].astype(F32) * bcol[h]).astype(BF16),
                            (kb[h] * egc[h]).astype(BF16)], axis=1) for h in heads]
    uw = [_dot(inv[h].astype(BF16), rhs[h]) for h in heads]
    qd = [(q[h].astype(F32) * egc[h]).astype(BF16) for h in heads]
    kdt = [(kt[h].astype(F32) * jnp.exp(glrow[h] - grow[h])).astype(BF16) for h in heads]
    state = [s_ref[h] for h in heads]
    for ch in ((1, 0) if rev else (0, 1)):
        r = slice(ch * CHUNK, (ch + 1) * CHUNK)
        ws = [_dot(jnp.concatenate([uw[h][r, GDN_DV:].astype(BF16), qd[h][r]], axis=0), state[h].astype(BF16))
              for h in heads]
        v_new = [uw[h][r, :GDN_DV] - ws[h][:CHUNK] for h in heads]
        vpad = [jnp.concatenate([v_new[h], zeros] if ch == 0 else [zeros, v_new[h]], axis=0).astype(BF16)
                for h in heads]
        res = [_dot(jnp.concatenate([intra[h][r], kdt[h]], axis=0), vpad[h]) for h in heads]
        for h in heads:
            o_ref[r, sls[h]] = (ws[h][CHUNK:] + res[h][:CHUNK]).astype(o_ref.dtype)
        gl = [jnp.sum(jnp.where(lane == ch * CHUNK, glrow[h], 0.0), axis=-1, keepdims=True) for h in heads]
        state = [state[h] * jnp.exp(gl[h]) + res[h][CHUNK:] for h in heads]
    for h in heads:
        s_ref[h] = state[h]


def _gdn_scan(q, k, v, kt, gc, beta, gct, glt, b, s, rev):
    t = b * s
    n = s // STEP
    d = GDN_HEADS * GDN_DK
    if rev:
        blk = lambda bi, i: bi * n + (n - 1 - i)
    else:
        blk = lambda bi, i: bi * n + i
    tile = lambda w: pl.BlockSpec((STEP, w), lambda bi, i: (blk(bi, i), 0))
    tile_t = lambda w: pl.BlockSpec((w, STEP), lambda bi, i: (0, blk(bi, i)))
    return pl.pallas_call(
        functools.partial(_gdn_scan_kernel, rev=rev),
        grid=(b, n),
        in_specs=[tile(d), tile(d), tile(d),
                  pl.BlockSpec((GDN_HEADS, GDN_DK, STEP), lambda bi, i: (0, 0, blk(bi, i))),
                  tile(NG), tile(NG), tile_t(NG), tile_t(NG)],
        out_specs=tile(d),
        out_shape=jax.ShapeDtypeStruct((t, d), BF16),
        scratch_shapes=[pltpu.VMEM((GDN_HEADS, GDN_DK, GDN_DV), F32)],
        compiler_params=_params("arbitrary", "arbitrary"),
        name="gdn_scan_bwd" if rev else "gdn_scan_fwd",
    )(q, k, v, kt, gc, beta, gct, glt)


def _gdn(qkv, sm, smt, conv_w, a_log, dt_bias, b, s):
    ts = min(256, s)
    q, k, v, kt, gc, beta, gct, glt = _gdn_prep(qkv, sm, smt, conv_w, a_log, dt_bias, b, s, ts)
    o_f = _gdn_scan(q, k, v, kt, gc, beta, gct, glt, b, s, False)
    o_b = _gdn_scan(q, k, v, kt, gc, beta, gct, glt, b, s, True)
    return o_f, o_b


def _outproj_kernel(x_ref, att_ref, of_ref, ob_ref, z_ref, ga_ref, gg_ref, wo_ref, nf_ref, wr_ref, br_ref,
                    y_ref, yn_ref, lg_ref):
    att = _rms(att_ref[...].astype(F32), ga_ref[...]).astype(BF16)
    o = of_ref[...].astype(F32) + ob_ref[...].astype(F32)
    z = z_ref[...].astype(F32)
    gg = gg_ref[...]
    lin = []
    for h in range(GDN_HEADS):
        sl = slice(h * GDN_DV, (h + 1) * GDN_DV)
        zh = z[:, sl]
        lin.append((_rms(o[:, sl], gg) * (zh * jax.nn.sigmoid(zh))).astype(BF16))
    lin = jnp.concatenate(lin, axis=-1)
    y = x_ref[...] + _dot(att, wo_ref[:D_ATTN, :]) + _dot(lin, wo_ref[D_ATTN:, :])
    y_ref[...] = y
    yn = _rms(y, nf_ref[...]).astype(BF16)
    yn_ref[...] = yn
    lg_ref[...] = _dot(yn, wr_ref[...]) + br_ref[...]


def _outproj(x2d, att, o_f, o_b, z, g_attn, g_gdn, w_out, norm_ffn, w_r, b_r, tm):
    t, d = x2d.shape
    row = lambda n: pl.BlockSpec((tm, n), lambda i: (i, 0))
    return pl.pallas_call(
        _outproj_kernel,
        grid=(t // tm,),
        in_specs=[row(d), row(D_ATTN), row(D_GDN), row(D_GDN), row(D_GDN), _resident((1, D_ATTN)),
                  _resident((1, GDN_DV)), _resident(w_out.shape), _resident((1, d)), _resident(w_r.shape),
                  _resident((1, LANE))],
        out_specs=[row(d), row(d), row(LANE)],
        out_shape=[jax.ShapeDtypeStruct((t, d), F32), jax.ShapeDtypeStruct((t, d), BF16),
                   jax.ShapeDtypeStruct((t, LANE), F32)],
        compiler_params=_params("arbitrary"),
        name="outproj",
    )(x2d, att, o_f, o_b, z, g_attn.reshape(1, -1), g_gdn.reshape(1, -1), w_out, norm_ffn.reshape(1, -1),
      w_r, b_r)


def _router_kernel(lg_ref, eid_ref, gate_ref):
    lg = lg_ref[...]
    lane = lax.broadcasted_iota(jnp.int32, lg.shape, 1)
    neg = -jnp.inf

    def first_max(x):
        mx = jnp.max(x, axis=-1, keepdims=True)
        return mx, jnp.min(jnp.where(x == mx, lane, LANE), axis=-1, keepdims=True)

    gl = jnp.where(lane < N_GROUPS, lg, neg)
    gmax, grp = first_max(gl)
    grp_p = 1.0 / jnp.sum(jnp.exp(gl - gmax), axis=-1, keepdims=True)
    lo = N_GROUPS + grp * EXPERTS_PER_GROUP
    el = jnp.where((lane >= lo) & (lane < lo + EXPERTS_PER_GROUP), lg, neg)
    l1, i1 = first_max(el)
    l2, i2 = first_max(jnp.where(lane == i1, neg, el))
    e = jnp.exp(l2 - l1)
    g1 = grp_p / (1.0 + e)
    eid_ref[...] = jnp.where(lane == 0, i1 - N_GROUPS, jnp.where(lane == 1, i2 - N_GROUPS, 0))
    gate_ref[...] = jnp.where(lane == 0, g1, jnp.where(lane == 1, g1 * e, 0.0))


def _router(logits, tm):
    t = logits.shape[0]
    row = pl.BlockSpec((tm, LANE), lambda i: (i, 0))
    return pl.pallas_call(
        _router_kernel,
        grid=(t // tm,),
        in_specs=[row],
        out_specs=[row, row],
        out_shape=[jax.ShapeDtypeStruct((t, LANE), jnp.int32), jax.ShapeDtypeStruct((t, LANE), F32)],
        compiler_params=_params("arbitrary"),
        name="router",
    )(logits)


MOE_BLOCK = 256


def _moe_kernel(be_ref, nu_ref, xb_ref, gate_ref, wg_ref, wu_ref, wd_ref, yb_ref):
    @pl.when(pl.program_id(0) < nu_ref[0])
    def _():
        x = xb_ref[...]
        a = _dot(x, wg_ref[0])
        hdn = (a * jax.nn.sigmoid(a) * _dot(x, wu_ref[0])).astype(BF16)
        yb_ref[...] = (_dot(hdn, wd_ref[0]) * gate_ref[...]).astype(yb_ref.dtype)


def _moe_blocks(block_e, n_used, xb, gate_rows, w_gate, w_up, w_down):
    p, d = xb.shape
    nb = p // MOE_BLOCK
    clamp = lambda i, nu: jnp.minimum(i, nu[0] - 1)
    grid_spec = pltpu.PrefetchScalarGridSpec(
        num_scalar_prefetch=2,
        grid=(nb,),
        in_specs=[pl.BlockSpec((MOE_BLOCK, d), lambda i, be, nu: (clamp(i, nu), 0)),
                  pl.BlockSpec((MOE_BLOCK, 1), lambda i, be, nu: (clamp(i, nu), 0)),
                  pl.BlockSpec((1, d, D_EXPERT), lambda i, be, nu: (be[clamp(i, nu)], 0, 0)),
                  pl.BlockSpec((1, d, D_EXPERT), lambda i, be, nu: (be[clamp(i, nu)], 0, 0)),
                  pl.BlockSpec((1, D_EXPERT, d), lambda i, be, nu: (be[clamp(i, nu)], 0, 0))],
        out_specs=pl.BlockSpec((MOE_BLOCK, d), lambda i, be, nu: (clamp(i, nu), 0)),
    )
    return pl.pallas_call(
        _moe_kernel,
        grid_spec=grid_spec,
        out_shape=jax.ShapeDtypeStruct((p, d), BF16),
        compiler_params=_params("arbitrary"),
        name="moe_experts",
    )(block_e, n_used, xb, gate_rows, w_gate, w_up, w_down)


def _final_kernel(y_ref, e_ref, g_ref, o_ref):
    d = y_ref.shape[1]
    e = e_ref[...].astype(F32)
    o_ref[...] = _rms(y_ref[...] + e[:, :d] + e[:, d:], g_ref[...])


def _final(y, ye, g, tm):
    t, d = y.shape
    return pl.pallas_call(
        _final_kernel,
        grid=(t // tm,),
        in_specs=[pl.BlockSpec((tm, d), lambda i: (i, 0)), pl.BlockSpec((tm, TOP_K * d), lambda i: (i, 0)),
                  _resident((1, d))],
        out_specs=pl.BlockSpec((tm, d), lambda i: (i, 0)),
        out_shape=jax.ShapeDtypeStruct((t, d), F32),
        compiler_params=_params("arbitrary"),
        name="final_norm",
    )(y, ye, g.reshape(1, -1))


def _route_plan(eid):
    m = eid.shape[0]
    order = jnp.argsort(eid).astype(jnp.int32)
    counts = jnp.zeros((N_EXPERTS,), jnp.int32).at[eid].add(1)
    starts = jnp.cumsum(counts) - counts
    padded = (counts + MOE_BLOCK - 1) // MOE_BLOCK * MOE_BLOCK
    pend = jnp.cumsum(padded)
    pstart = pend - padded
    sorted_e = eid[order]
    dest = pstart[sorted_e] + (jnp.arange(m, dtype=jnp.int32) - starts[sorted_e])
    n_blocks = -(-m // MOE_BLOCK) + N_EXPERTS
    slot_at = jnp.full((n_blocks * MOE_BLOCK,), m, jnp.int32).at[dest].set(order)
    block_e = jnp.minimum(jnp.searchsorted(pend, jnp.arange(n_blocks, dtype=jnp.int32) * MOE_BLOCK, side='right'),
                          N_EXPERTS - 1).astype(jnp.int32)
    pos = jnp.zeros((m,), jnp.int32).at[order].set(dest)
    n_used = (pend[-1] // MOE_BLOCK).astype(jnp.int32).reshape(1)
    return slot_at, block_e, pos, n_used


def _encoder_front(x, p, wts):
    b, s, d = x.shape
    x2d = x.reshape(b * s, d)
    tm = min(512, b * s)
    lat, sm, smt, qkv, z = _inproj(x2d, p["norm_mix"].reshape(1, -1), wts["w_lat"], wts["w_sm"], wts["w_smt"],
                                   wts["w_qkv"], wts["w_z"], tm)
    att = _mla(lat, sm, p["g_q_lora"], p["g_kv_lora"], p["w_uq"], p["w_ukv"], b, s)
    o_f, o_b = _gdn(qkv, sm, smt, p["conv_w"], p["a_log"], p["dt_bias"], b, s)
    return _outproj(x2d, att, o_f, o_b, z, p["g_attn_out"], p["g_gdn_out"], wts["w_out"], p["norm_ffn"],
                    wts["w_r"], wts["b_r"], tm)


def _prep_weights(p):
    w_in = p["w_in"]
    o = np.cumsum([0, Q_LORA, KV_LORA, QK_ROPE, GDN_QKV, D_GDN, NG, NG])
    d = w_in.shape[0]
    half = QK_ROPE // 2
    w_sm = jnp.concatenate([w_in[:, o[2]:o[3]], w_in[:, o[5]:o[7]], jnp.zeros((d, LANE - QK_ROPE - 2 * NG), F32),
                            w_in[:, o[2] + half:o[3]], w_in[:, o[2]:o[2] + half],
                            jnp.zeros((d, LANE - QK_ROPE), F32)], axis=1).astype(BF16)
    w_r = jnp.concatenate([p["w_router_group"], p["w_router_expert"],
                           jnp.zeros((d, LANE - N_GROUPS - N_EXPERTS), F32)], axis=1).astype(BF16)
    b_r = jnp.concatenate([p["b_router_group"], p["b_router_expert"],
                           jnp.zeros((LANE - N_GROUPS - N_EXPERTS,), F32)]).reshape(1, LANE)
    return dict(w_lat=w_in[:, :o[2]].astype(BF16), w_sm=w_sm, w_smt=w_sm[:, :LANE].T, w_qkv=w_in[:, o[3]:o[4]].astype(BF16),
                w_z=w_in[:, o[4]:o[5]].astype(BF16), w_out=p["w_out"].astype(BF16), w_r=w_r, b_r=b_r,
                w_gate=p["w_gate"].astype(BF16), w_up=p["w_up"].astype(BF16), w_down=p["w_down"].astype(BF16))


def _encode(xs, p):
    wts = _prep_weights(p)
    fronts = [_encoder_front(x, p, wts) for x in xs]
    yn = jnp.concatenate([f[1] for f in fronts], axis=0)
    logits = jnp.concatenate([f[2] for f in fronts], axis=0)
    t = yn.shape[0]
    eid_l, gate_l = _router(logits, min(512, t))
    eid = eid_l[:, :TOP_K].reshape(-1)
    gate = gate_l[:, :TOP_K].reshape(-1)
    slot_at, block_e, pos, n_used = _route_plan(eid)
    m = eid.shape[0]
    tok = jnp.minimum(slot_at, m - 1) // TOP_K
    xb = jnp.take(yn, tok, axis=0)
    gate_rows = jnp.where(slot_at < m, jnp.take(gate, jnp.minimum(slot_at, m - 1)), 0.0).reshape(-1, 1)
    yb = _moe_blocks(block_e, n_used, xb, gate_rows, wts["w_gate"], wts["w_up"], wts["w_down"])
    ye = jnp.take(yb, pos, axis=0).reshape(t, TOP_K * yb.shape[1])
    outs, t0 = [], 0
    for x, f in zip(xs, fronts):
        n = f[0].shape[0]
        out = _final(f[0], ye[t0:t0 + n], p["norm_final"], min(512, n))
        outs.append(out.reshape(x.shape))
        t0 += n
    return outs


def kernel(x_prompt, x_sample, norm_mix, w_in, g_q_lora, g_kv_lora, w_uq, w_ukv, g_attn_out, conv_w, a_log,
           dt_bias, g_gdn_out, w_out, norm_ffn, w_router_group, b_router_group, w_router_expert,
           b_router_expert, w_gate, w_up, w_down, norm_final):
    p = dict(norm_mix=norm_mix[0], w_in=w_in[0], g_q_lora=g_q_lora[0], g_kv_lora=g_kv_lora[0], w_uq=w_uq[0],
             w_ukv=w_ukv[0], g_attn_out=g_attn_out[0], conv_w=conv_w[0], a_log=a_log[0], dt_bias=dt_bias[0],
             g_gdn_out=g_gdn_out[0], w_out=w_out[0], norm_ffn=norm_ffn[0], w_router_group=w_router_group[0],
             b_router_group=b_router_group[0], w_router_expert=w_router_expert[0],
             b_router_expert=b_router_expert[0], w_gate=w_gate[0], w_up=w_up[0], w_down=w_down[0],
             norm_final=norm_final)
    y_prompt, y_sample = _encode([x_prompt, x_sample], p)
    return (y_prompt, y_sample)
```

```python
import functools
import math

import jax
import jax.numpy as jnp
import numpy as np
from jax import lax
from jax.experimental import pallas as pl
from jax.experimental.pallas import tpu as pltpu

F32 = jnp.float32
BF16 = jnp.bfloat16

D_MODEL = 2048
MLA_HEADS = 8
Q_LORA = 512
KV_LORA = 512
QK_NOPE = 128
QK_ROPE = 64
V_HEAD = 128
ROPE_THETA = 10000.0
GDN_HEADS = 8
GDN_DK = 128
GDN_DV = 128
GDN_QKV = GDN_HEADS * (2 * GDN_DK + GDN_DV)
CONV_K = 5
CHUNK = 64
D_ATTN = MLA_HEADS * V_HEAD
D_GDN = GDN_HEADS * GDN_DV
N_GROUPS = 8
EXPERTS_PER_GROUP = 8
N_EXPERTS = N_GROUPS * EXPERTS_PER_GROUP
TOP_K = 2
D_EXPERT = 512
EPS = 1e-6

LANE = 128
QK_PAD = 256
VMEM_LIMIT = 56 * 1024 * 1024


def _params(*sem):
    return pltpu.CompilerParams(dimension_semantics=sem, vmem_limit_bytes=VMEM_LIMIT)


def _resident(shape):
    return pl.BlockSpec(shape, lambda *_: (0,) * len(shape), pipeline_mode=pl.Buffered(1))


def _rms(x, g):
    return x * lax.rsqrt(jnp.mean(x * x, axis=-1, keepdims=True) + EPS) * g


def _dot(a, b):
    return jnp.dot(a, b, preferred_element_type=F32)


def _dot_nt(a, b):
    return lax.dot_general(a, b, (((1,), (1,)), ((), ())), preferred_element_type=F32)


def _pack_halves(x):
    w = x.shape[1] // 2
    lo = pltpu.bitcast(x[:, :w].astype(F32), jnp.uint32)
    hi = pltpu.bitcast(x[:, w:].astype(F32), jnp.uint32)
    return (hi & jnp.uint32(0xFFFF0000)) | (lo >> 16)


def _unpack_halves(u):
    lo = pltpu.bitcast(u << 16, F32)
    hi = pltpu.bitcast(u & jnp.uint32(0xFFFF0000), F32)
    return lo, hi


def _inproj_kernel(x_ref, g_ref, w_lat_ref, w_sm_ref, w_smt_ref, w_qkv_ref, w_z_ref,
                   lat_ref, sm_ref, smt_ref, qkv_ref, z_ref):
    xn = _rms(x_ref[...], g_ref[...]).astype(BF16)
    lat_ref[...] = _dot(xn, w_lat_ref[...]).astype(BF16)
    sm_ref[...] = _dot(xn, w_sm_ref[...])
    smt_ref[...] = _dot_nt(w_smt_ref[...], xn)
    qkv_ref[...] = _dot(xn, w_qkv_ref[...]).astype(BF16)
    z_ref[...] = _dot(xn, w_z_ref[...]).astype(BF16)


def _inproj(x2d, g, w_lat, w_sm, w_smt, w_qkv, w_z, tm):
    t, d = x2d.shape
    n_lat, n_sm, n_qkv, n_z = w_lat.shape[1], w_sm.shape[1], w_qkv.shape[1], w_z.shape[1]
    n_smt = w_smt.shape[0]
    row = lambda n: pl.BlockSpec((tm, n), lambda i: (i, 0))
    return pl.pallas_call(
        _inproj_kernel,
        grid=(t // tm,),
        in_specs=[row(d), _resident((1, d)), _resident(w_lat.shape), _resident(w_sm.shape),
                  _resident(w_smt.shape), _resident(w_qkv.shape), _resident(w_z.shape)],
        out_specs=[row(n_lat), row(n_sm), pl.BlockSpec((n_smt, tm), lambda i: (0, i)), row(n_qkv), row(n_z)],
        out_shape=[jax.ShapeDtypeStruct((t, n_lat), BF16), jax.ShapeDtypeStruct((t, n_sm), F32),
                   jax.ShapeDtypeStruct((n_smt, t), F32), jax.ShapeDtypeStruct((t, n_qkv), BF16),
                   jax.ShapeDtypeStruct((t, n_z), BF16)],
        compiler_params=_params("arbitrary"),
        name="inproj",
    )(x2d, g, w_lat, w_sm, w_smt, w_qkv, w_z)


def _mla_up_kernel(lat_ref, sm_ref, gq_ref, gkv_ref, wqt_ref, wk_ref, wvt_ref,
                   cos_t_ref, sin_t_ref, ck_ref, sk_ref, qt_ref, k_ref, vt_ref):
    lat = lat_ref[...].astype(F32)
    cqn = _rms(lat[:, :Q_LORA], gq_ref[...]).astype(BF16)
    ckvn = _rms(lat[:, Q_LORA:], gkv_ref[...]).astype(BF16)
    half = QK_ROPE // 2
    dqk = QK_NOPE + QK_ROPE
    qt = _dot_nt(wqt_ref[...], cqn)
    cos_t, sin_t = cos_t_ref[...], sin_t_ref[...]
    zero = jnp.zeros((QK_PAD - dqk, qt.shape[1]), BF16)
    for h in range(MLA_HEADS):
        lo = h * dqk
        x1 = qt[lo + QK_NOPE:lo + QK_NOPE + half, :]
        x2 = qt[lo + QK_NOPE + half:lo + dqk, :]
        qt_ref[0, h, 0, 0:QK_NOPE, :] = qt[lo:lo + QK_NOPE, :].astype(BF16)
        qt_ref[0, h, 0, QK_NOPE:QK_NOPE + half, :] = (x1 * cos_t - x2 * sin_t).astype(BF16)
        qt_ref[0, h, 0, QK_NOPE + half:dqk, :] = (x2 * cos_t + x1 * sin_t).astype(BF16)
        qt_ref[0, h, 0, dqk:, :] = zero
    kn = _dot(ckvn, wk_ref[...])
    sm = sm_ref[...]
    pe = (sm[:, :LANE] * ck_ref[...] + sm[:, LANE:] * sk_ref[...]).astype(BF16)
    for h in range(MLA_HEADS):
        k_ref[:, h * QK_PAD:h * QK_PAD + QK_NOPE] = kn[:, h * QK_NOPE:(h + 1) * QK_NOPE].astype(BF16)
        k_ref[:, h * QK_PAD + QK_NOPE:(h + 1) * QK_PAD] = pe
    vt = _dot_nt(wvt_ref[...], ckvn)
    for h in range(MLA_HEADS):
        vt_ref[0, h, 0, 0:V_HEAD, :] = vt[h * V_HEAD:(h + 1) * V_HEAD, :].astype(BF16)
        vt_ref[0, h, 0, V_HEAD:, :] = jnp.ones((V_ROWS - V_HEAD, vt.shape[1]), BF16)


def _mla_up(lat, sm, gq, gkv, wqt, wk, wvt, cos_t, sin_t, ck, sk, b, s, ts):
    t = b * s
    nj = s // ts
    hk = MLA_HEADS * QK_PAD
    return pl.pallas_call(
        _mla_up_kernel,
        grid=(b, nj),
        in_specs=[pl.BlockSpec((ts, lat.shape[1]), lambda bi, j: (bi * nj + j, 0)),
                  pl.BlockSpec((ts, sm.shape[1]), lambda bi, j: (bi * nj + j, 0)),
                  _resident(gq.shape), _resident(gkv.shape), _resident(wqt.shape), _resident(wk.shape),
                  _resident(wvt.shape),
                  pl.BlockSpec((QK_ROPE // 2, ts), lambda bi, j: (0, j)),
                  pl.BlockSpec((QK_ROPE // 2, ts), lambda bi, j: (0, j)),
                  pl.BlockSpec((ts, LANE), lambda bi, j: (j, 0)),
                  pl.BlockSpec((ts, LANE), lambda bi, j: (j, 0))],
        out_specs=[pl.BlockSpec((1, MLA_HEADS, 1, QK_PAD, ts), lambda bi, j: (bi, 0, j, 0, 0)),
                   pl.BlockSpec((ts, hk), lambda bi, j: (bi * nj + j, 0)),
                   pl.BlockSpec((1, MLA_HEADS, 1, V_ROWS, ts), lambda bi, j: (bi, 0, j, 0, 0))],
        out_shape=[jax.ShapeDtypeStruct((b, MLA_HEADS, nj, QK_PAD, ts), BF16),
                   jax.ShapeDtypeStruct((t, hk), BF16),
                   jax.ShapeDtypeStruct((b, MLA_HEADS, nj, V_ROWS, ts), BF16)],
        compiler_params=_params("arbitrary", "arbitrary"),
        name="mla_up",
    )(lat, sm, gq, gkv, wqt, wk, wvt, cos_t, sin_t, ck, sk)


V_ROWS = V_HEAD + 16


def _attn_kernel(qt_ref, k_ref, vt_ref, o_ref, m_ref, acc_ref, s_buf, p_buf, a_buf, *, nk, tk, unroll):
    qt = qt_ref[0, 0, 0]
    m_ref[...] = jnp.full(m_ref.shape, -jnp.inf, F32)
    acc_ref[...] = jnp.zeros(acc_ref.shape, F32)
    p_buf[1] = jnp.zeros(p_buf.shape[1:], BF16)
    a_buf[1] = jnp.ones(a_buf.shape[1:], F32)
    s_buf[0] = _dot(k_ref[0:tk, :], qt)

    def step(j, cur, nxt):
        jp = jnp.maximum(j - 1, 0)
        acc_ref[...] = a_buf[nxt] * acc_ref[...] + _dot(vt_ref[0, 0, jp], p_buf[nxt])
        jn = jnp.minimum(j + 1, nk - 1)
        s_buf[nxt] = _dot(k_ref[pl.ds(pl.multiple_of(jn * tk, tk), tk), :], qt)
        s = s_buf[cur]
        m_old = m_ref[...]
        m_new = jnp.maximum(m_old, jnp.max(s, axis=0, keepdims=True))
        a_buf[cur] = jnp.exp2(m_old - m_new)
        p_buf[cur] = jnp.exp2(s - m_new).astype(BF16)
        m_ref[...] = m_new

    def body(jj, carry):
        for u in range(unroll):
            step(unroll * jj + u, u % 2, 1 - u % 2)
        return carry

    lax.fori_loop(0, nk // unroll, body, 0)
    acc = a_buf[1] * acc_ref[...] + _dot(vt_ref[0, 0, nk - 1], p_buf[1])
    o_ref[...] = (acc[:V_HEAD] / acc[V_HEAD:V_HEAD + 1]).T.astype(o_ref.dtype)


def _attention(qt, k, vt, b, s):
    nq, tq = qt.shape[2], qt.shape[4]
    nk, tk = vt.shape[2], vt.shape[4]
    assert nk % 2 == 0, "the two-stage buffers alternate statically"
    unroll = 4 if nk % 4 == 0 else 2
    return pl.pallas_call(
        functools.partial(_attn_kernel, nk=nk, tk=tk, unroll=unroll),
        grid=(b, MLA_HEADS, nq),
        in_specs=[pl.BlockSpec((1, 1, 1, QK_PAD, tq), lambda bi, h, i: (bi, h, i, 0, 0)),
                  pl.BlockSpec((s, QK_PAD), lambda bi, h, i: (bi, h)),
                  pl.BlockSpec((1, 1, nk, V_ROWS, tk), lambda bi, h, i: (bi, h, 0, 0, 0))],
        out_specs=pl.BlockSpec((tq, V_HEAD), lambda bi, h, i: (bi * nq + i, h)),
        out_shape=jax.ShapeDtypeStruct((b * s, D_ATTN), BF16),
        scratch_shapes=[pltpu.VMEM((1, tq), F32), pltpu.VMEM((V_ROWS, tq), F32),
                        pltpu.VMEM((2, tk, tq), F32), pltpu.VMEM((2, tk, tq), BF16), pltpu.VMEM((2, 1, tq), F32)],
        compiler_params=_params("arbitrary", "arbitrary", "arbitrary"),
        name="attention",
    )(qt, k, vt)


def _rope_tables(s):
    inv_freq = ROPE_THETA ** (-jnp.arange(0, QK_ROPE, 2, dtype=F32) / QK_ROPE)
    ang = jnp.arange(s, dtype=F32)[:, None] * inv_freq[None, :]
    return jnp.cos(ang), jnp.sin(ang)


def _prep_mla_weights(w_uq, w_ukv):
    half = QK_ROPE // 2
    scale = (QK_NOPE + QK_ROPE) ** -0.5 * math.log2(math.e)
    wqt = (w_uq * scale).T
    wkv = w_ukv.reshape(KV_LORA, MLA_HEADS, QK_NOPE + V_HEAD)
    wk = wkv[..., :QK_NOPE].reshape(KV_LORA, MLA_HEADS * QK_NOPE)
    wvt = wkv[..., QK_NOPE:].reshape(KV_LORA, MLA_HEADS * V_HEAD).T
    return wqt.astype(BF16), wk.astype(BF16), wvt.astype(BF16)


def _mla(lat, sm, g_q, g_kv, w_uq, w_ukv, b, s):
    ts = min(512, s)
    wqt, wk, wvt = _prep_mla_weights(w_uq, w_ukv)
    cos, sin = _rope_tables(s)
    zeros = jnp.zeros((s, LANE - QK_ROPE), F32)
    ck = jnp.concatenate([cos, cos, zeros], axis=-1)
    sk = jnp.concatenate([-sin, sin, zeros], axis=-1)
    qt, k, vt = _mla_up(lat, sm, g_q.reshape(1, -1), g_kv.reshape(1, -1), wqt, wk, wvt,
                        cos.T, sin.T, ck, sk, b, s, ts)
    return _attention(qt, k, vt, b, s)


HALO = 16
GATE_A = QK_ROPE
GATE_BT = QK_ROPE + 2 * GDN_HEADS
NG = 2 * GDN_HEADS


def _chunk_masks(n, rev):
    ri = lax.broadcasted_iota(jnp.int32, (n, n), 0)
    ci = lax.broadcasted_iota(jnp.int32, (n, n), 1)
    same = (ri // CHUNK) == (ci // CHUNK)
    if rev:
        return same, same & (ri <= ci), same & (ri < ci), ri == ci
    return same, same & (ri >= ci), same & (ri > ci), ri == ci


def _dot_exact(a, b):
    return jnp.dot(a, b, preferred_element_type=F32, precision=lax.Precision.HIGHEST)


def _dot_nt_exact(a, b):
    return lax.dot_general(a, b, (((1,), (1,)), ((), ())), preferred_element_type=F32,
                           precision=lax.Precision.HIGHEST)


def _softplus(x):
    return jnp.maximum(x, 0.0) + jnp.log(1.0 + jnp.exp(-jnp.abs(x)))


def _gdn_prep_kernel(qkv_ref, prev_ref, next_ref, sm_ref, smt_ref, cw_ref, alog_ref, dtb_ref,
                     alogt_ref, dtbt_ref, q_ref, k_ref, v_ref, kt_ref, gc_ref, beta_ref, gct_ref, glt_ref,
                     xs_ref, *, ts):
    j = pl.program_id(1)
    nj = pl.num_programs(1)
    xs_ref[0:HALO, :] = jnp.where(j > 0, prev_ref[...].astype(F32), 0.0)
    xs_ref[HALO:HALO + ts, :] = qkv_ref[...].astype(F32)
    xs_ref[HALO + ts:, :] = jnp.where(j < nj - 1, next_ref[...].astype(F32), 0.0)
    acc = None
    for tap in range(CONV_K):
        lo = HALO - CONV_K // 2 + tap
        term = xs_ref[lo:lo + ts, :] * cw_ref[tap:tap + 1, :]
        acc = term if acc is None else acc + term
    act = acc * jax.nn.sigmoid(acc)
    eye = (lax.broadcasted_iota(jnp.int32, (GDN_DK, GDN_DK), 0)
           == lax.broadcasted_iota(jnp.int32, (GDN_DK, GDN_DK), 1)).astype(BF16)
    hk = GDN_HEADS * GDN_DK
    for h in range(GDN_HEADS):
        qh = act[:, h * GDN_DK:(h + 1) * GDN_DK]
        kh = act[:, hk + h * GDN_DK:hk + (h + 1) * GDN_DK]
        qh = qh * (lax.rsqrt(jnp.sum(qh * qh, axis=-1, keepdims=True) + EPS) * GDN_DK ** -0.5)
        kh = (kh * lax.rsqrt(jnp.sum(kh * kh, axis=-1, keepdims=True) + EPS)).astype(BF16)
        q_ref[:, h * GDN_DK:(h + 1) * GDN_DK] = qh.astype(BF16)
        k_ref[:, h * GDN_DK:(h + 1) * GDN_DK] = kh
        kt_ref[h] = _dot_nt(eye, kh).astype(BF16)
    v_ref[...] = act[:, 2 * hk:].astype(BF16)

    sm = sm_ref[...]
    g = -jnp.exp(alog_ref[...]) * _softplus(sm[:, GATE_A:GATE_A + NG] + dtb_ref[...])
    beta_ref[...] = jax.nn.sigmoid(sm[:, GATE_BT:GATE_BT + NG])
    _, incl_f, _, _ = _chunk_masks(ts, False)
    _, incl_b, _, _ = _chunk_masks(ts, True)
    tri_f = incl_f.astype(F32)
    tri_b = incl_b.astype(F32)
    is_fwd = lax.broadcasted_iota(jnp.int32, (ts, NG), 1) < GDN_HEADS
    gc_ref[...] = jnp.where(is_fwd, _dot_exact(tri_f, g), _dot_exact(tri_b, g))
    smt = smt_ref[...]
    gt = -jnp.exp(alogt_ref[...]) * _softplus(smt[GATE_A:GATE_A + NG, :] + dtbt_ref[...])
    is_fwd_t = lax.broadcasted_iota(jnp.int32, (NG, ts), 0) < GDN_HEADS
    gct_ref[...] = jnp.where(is_fwd_t, _dot_nt_exact(gt, tri_f), _dot_nt_exact(gt, tri_b))
    same, _, _, _ = _chunk_masks(ts, False)
    glt_ref[...] = _dot_exact(gt, same.astype(F32))


def _gdn_prep(qkv, sm, smt, conv_w, a_log, dt_bias, b, s, ts):
    t = b * s
    nj = s // ts
    c = qkv.shape[1]
    hb = ts // HALO
    d = GDN_HEADS * GDN_DK
    tile = lambda n: pl.BlockSpec((ts, n), lambda bi, j: (bi * nj + j, 0))
    tile_t = lambda n: pl.BlockSpec((n, ts), lambda bi, j: (0, bi * nj + j))
    return pl.pallas_call(
        functools.partial(_gdn_prep_kernel, ts=ts),
        grid=(b, nj),
        in_specs=[tile(c),
                  pl.BlockSpec((HALO, c), lambda bi, j: (jnp.maximum((bi * nj + j) * hb - 1, 0), 0)),
                  pl.BlockSpec((HALO, c), lambda bi, j: (jnp.minimum((bi * nj + j + 1) * hb, t // HALO - 1), 0)),
                  tile(sm.shape[1]), tile_t(smt.shape[0]),
                  _resident(conv_w.shape), _resident((1, NG)), _resident((1, NG)),
                  _resident((NG, 1)), _resident((NG, 1))],
        out_specs=[tile(d), tile(d), tile(d),
                   pl.BlockSpec((GDN_HEADS, GDN_DK, ts), lambda bi, j: (0, 0, bi * nj + j)),
                   tile(NG), tile(NG), tile_t(NG), tile_t(NG)],
        out_shape=[jax.ShapeDtypeStruct((t, d), BF16)] * 3
        + [jax.ShapeDtypeStruct((GDN_HEADS, GDN_DK, t), BF16),
           jax.ShapeDtypeStruct((t, NG), F32), jax.ShapeDtypeStruct((t, NG), F32),
           jax.ShapeDtypeStruct((NG, t), F32), jax.ShapeDtypeStruct((NG, t), F32)],
        scratch_shapes=[pltpu.VMEM((ts + 2 * HALO, c), F32)],
        compiler_params=_params("arbitrary", "arbitrary"),
        name="gdn_prep",
    )(qkv, qkv, qkv, sm, smt, conv_w, a_log.reshape(1, NG), dt_bias.reshape(1, NG),
      a_log.reshape(NG, 1), dt_bias.reshape(NG, 1))


STEP = 2 * CHUNK


def _gdn_scan_kernel(q_ref, k_ref, v_ref, kt_ref, gc_ref, beta_ref, gct_ref, glt_ref, o_ref, s_ref, *, rev):
    @pl.when(pl.program_id(1) == 0)
    def _():
        s_ref[...] = jnp.zeros(s_ref.shape, F32)

    _, incl, strict, diag = _chunk_masks(STEP, rev)
    eye = diag.astype(F32)
    lane = lax.broadcasted_iota(jnp.int32, (1, STEP), 1)
    zeros = jnp.zeros((CHUNK, GDN_DV), F32)
    col0 = GDN_HEADS if rev else 0
    heads = range(GDN_HEADS)
    sls = [slice(h * GDN_DK, (h + 1) * GDN_DK) for h in heads]
    q = [q_ref[:, sl] for sl in sls]
    kt = [kt_ref[h] for h in heads]
    gcol = [gc_ref[:, col0 + h:col0 + h + 1] for h in heads]
    bcol = [beta_ref[:, col0 + h:col0 + h + 1] for h in heads]
    grow = [gct_ref[col0 + h:col0 + h + 1, :] for h in heads]
    glrow = [glt_ref[col0 + h:col0 + h + 1, :] for h in heads]
    decay = [jnp.exp(jnp.where(incl, gcol[h] - grow[h], -jnp.inf)) for h in heads]
    egc = [jnp.exp(gcol[h]) for h in heads]
    kb = [k_ref[:, sls[h]].astype(F32) * bcol[h] for h in heads]
    aq = [_dot(jnp.concatenate([kb[h].astype(BF16), q[h]], axis=0), kt[h]) for h in heads]
    power = [-jnp.where(strict, aq[h][:STEP] * decay[h], 0.0) for h in heads]
    intra = [(aq[h][STEP:] * decay[h]).astype(BF16) for h in heads]
    inv = [eye + power[h] for h in heads]
    for _ in range(int(math.log2(CHUNK)) - 1):
        pb = [power[h].astype(BF16) for h in heads]
        power = [_dot(pb[h], pb[h]) for h in heads]
        inv = [inv[h] + _dot(inv[h].astype(BF16), power[h].astype(BF16)) for h in heads]
    rhs = [jnp.concatenate([(v_ref[:, sls[h]].astype(F32) * bcol[h]).astype(BF16),
                            (kb[h] * egc[h]).astype(BF16)], axis=1) for h in heads]
    uw = [_dot(inv[h].astype(BF16), rhs[h]) for h in heads]
    qd = [(q[h].astype(F32) * egc[h]).astype(BF16) for h in heads]
    kdt = [(kt[h].astype(F32) * jnp.exp(glrow[h] - grow[h])).astype(BF16) for h in heads]
    state = [s_ref[h] for h in heads]
    for ch in ((1, 0) if rev else (0, 1)):
        r = slice(ch * CHUNK, (ch + 1) * CHUNK)
        ws = [_dot(jnp.concatenate([uw[h][r, GDN_DV:].astype(BF16), qd[h][r]], axis=0), state[h].astype(BF16))
              for h in heads]
        v_new = [uw[h][r, :GDN_DV] - ws[h][:CHUNK] for h in heads]
        vpad = [jnp.concatenate([v_new[h], zeros] if ch == 0 else [zeros, v_new[h]], axis=0).astype(BF16)
                for h in heads]
        res = [_dot(jnp.concatenate([intra[h][r], kdt[h]], axis=0), vpad[h]) for h in heads]
        for h in heads:
            o_ref[r, sls[h]] = (ws[h][CHUNK:] + res[h][:CHUNK]).astype(o_ref.dtype)
        gl = [jnp.sum(jnp.where(lane == ch * CHUNK, glrow[h], 0.0), axis=-1, keepdims=True) for h in heads]
        state = [state[h] * jnp.exp(gl[h]) + res[h][CHUNK:] for h in heads]
    for h in heads:
        s_ref[h] = state[h]


def _gdn_scan(q, k, v, kt, gc, beta, gct, glt, b, s, rev):
    t = b * s
    n = s // STEP
    d = GDN_HEADS * GDN_DK
    if rev:
        blk = lambda bi, i: bi * n + (n - 1 - i)
    else:
        blk = lambda bi, i: bi * n + i
    tile = lambda w: pl.BlockSpec((STEP, w), lambda bi, i: (blk(bi, i), 0))
    tile_t = lambda w: pl.BlockSpec((w, STEP), lambda bi, i: (0, blk(bi, i)))
    return pl.pallas_call(
        functools.partial(_gdn_scan_kernel, rev=rev),
        grid=(b, n),
        in_specs=[tile(d), tile(d), tile(d),
                  pl.BlockSpec((GDN_HEADS, GDN_DK, STEP), lambda bi, i: (0, 0, blk(bi, i))),
                  tile(NG), tile(NG), tile_t(NG), tile_t(NG)],
        out_specs=tile(d),
        out_shape=jax.ShapeDtypeStruct((t, d), BF16),
        scratch_shapes=[pltpu.VMEM((GDN_HEADS, GDN_DK, GDN_DV), F32)],
        compiler_params=_params("arbitrary", "arbitrary"),
        name="gdn_scan_bwd" if rev else "gdn_scan_fwd",
    )(q, k, v, kt, gc, beta, gct, glt)


def _gdn(qkv, sm, smt, conv_w, a_log, dt_bias, b, s):
    ts = min(256, s)
    q, k, v, kt, gc, beta, gct, glt = _gdn_prep(qkv, sm, smt, conv_w, a_log, dt_bias, b, s, ts)
    o_f = _gdn_scan(q, k, v, kt, gc, beta, gct, glt, b, s, False)
    o_b = _gdn_scan(q, k, v, kt, gc, beta, gct, glt, b, s, True)
    return o_f, o_b


def _outproj_kernel(x_ref, att_ref, of_ref, ob_ref, z_ref, ga_ref, gg_ref, wo_ref, nf_ref, wr_ref, br_ref,
                    y_ref, yn_ref, lg_ref):
    att = _rms(att_ref[...].astype(F32), ga_ref[...]).astype(BF16)
    o = of_ref[...].astype(F32) + ob_ref[...].astype(F32)
    z = z_ref[...].astype(F32)
    gg = gg_ref[...]
    lin = []
    for h in range(GDN_HEADS):
        sl = slice(h * GDN_DV, (h + 1) * GDN_DV)
        zh = z[:, sl]
        lin.append((_rms(o[:, sl], gg) * (zh * jax.nn.sigmoid(zh))).astype(BF16))
    lin = jnp.concatenate(lin, axis=-1)
    y = x_ref[...] + _dot(att, wo_ref[:D_ATTN, :]) + _dot(lin, wo_ref[D_ATTN:, :])
    y_ref[...] = y
    yn = _rms(y, nf_ref[...]).astype(BF16)
    yn_ref[...] = _pack_halves(yn)
    lg_ref[...] = _dot(yn, wr_ref[...]) + br_ref[...]


def _outproj(x2d, att, o_f, o_b, z, g_attn, g_gdn, w_out, norm_ffn, w_r, b_r, tm):
    t, d = x2d.shape
    row = lambda n: pl.BlockSpec((tm, n), lambda i: (i, 0))
    return pl.pallas_call(
        _outproj_kernel,
        grid=(t // tm,),
        in_specs=[row(d), row(D_ATTN), row(D_GDN), row(D_GDN), row(D_GDN), _resident((1, D_ATTN)),
                  _resident((1, GDN_DV)), _resident(w_out.shape), _resident((1, d)), _resident(w_r.shape),
                  _resident((1, LANE))],
        out_specs=[row(d), row(d // 2), row(LANE)],
        out_shape=[jax.ShapeDtypeStruct((t, d), F32), jax.ShapeDtypeStruct((t, d // 2), jnp.uint32),
                   jax.ShapeDtypeStruct((t, LANE), F32)],
        compiler_params=_params("arbitrary"),
        name="outproj",
    )(x2d, att, o_f, o_b, z, g_attn.reshape(1, -1), g_gdn.reshape(1, -1), w_out, norm_ffn.reshape(1, -1),
      w_r, b_r)


def _router_kernel(lg_ref, eid_ref, gate_ref, rank_ref, cnt_ref, carry_ref):
    @pl.when(pl.program_id(0) == 0)
    def _():
        carry_ref[...] = jnp.zeros(carry_ref.shape, F32)

    lg = lg_ref[...]
    lane = lax.broadcasted_iota(jnp.int32, lg.shape, 1)
    neg = -jnp.inf

    def first_max(x):
        mx = jnp.max(x, axis=-1, keepdims=True)
        return mx, jnp.min(jnp.where(x == mx, lane, LANE), axis=-1, keepdims=True)

    gl = jnp.where(lane < N_GROUPS, lg, neg)
    gmax, grp = first_max(gl)
    grp_p = 1.0 / jnp.sum(jnp.exp(gl - gmax), axis=-1, keepdims=True)
    lo = N_GROUPS + grp * EXPERTS_PER_GROUP
    el = jnp.where((lane >= lo) & (lane < lo + EXPERTS_PER_GROUP), lg, neg)
    l1, i1 = first_max(el)
    l2, i2 = first_max(jnp.where(lane == i1, neg, el))
    e = jnp.exp(l2 - l1)
    g1 = grp_p / (1.0 + e)
    e1, e2 = i1 - N_GROUPS, i2 - N_GROUPS
    eid_ref[...] = jnp.where(lane == 0, e1, jnp.where(lane == 1, e2, 0))
    gate_ref[...] = jnp.where(lane == 0, g1, jnp.where(lane == 1, g1 * e, 0.0))
    tm = lg.shape[0]
    oh1, oh2 = lane == e1, lane == e2
    before = (lax.broadcasted_iota(jnp.int32, (tm, tm), 1)
              < lax.broadcasted_iota(jnp.int32, (tm, tm), 0)).astype(BF16)
    r1 = _dot(before, oh1.astype(BF16))
    r2 = _dot(before, oh2.astype(BF16))
    c1 = jnp.sum(oh1.astype(F32), axis=0, keepdims=True)
    c2 = jnp.sum(oh2.astype(F32), axis=0, keepdims=True)
    carry = carry_ref[...]
    rank1 = jnp.sum(jnp.where(oh1, r1 + carry, 0.0), axis=-1, keepdims=True)
    rank2 = jnp.sum(jnp.where(oh2, r2 + (carry + c1), 0.0), axis=-1, keepdims=True)
    rank_ref[...] = jnp.where(lane == 0, rank1, jnp.where(lane == 1, rank2, 0.0)).astype(jnp.int32)
    carry_ref[...] = carry + c1 + c2
    cnt_ref[...] = (carry + c1 + c2).astype(jnp.int32)


def _router(logits, tm):
    t = logits.shape[0]
    row = pl.BlockSpec((tm, LANE), lambda i: (i, 0))
    return pl.pallas_call(
        _router_kernel,
        grid=(t // tm,),
        in_specs=[row],
        out_specs=[row, row, row, pl.BlockSpec((1, LANE), lambda i: (0, 0))],
        out_shape=[jax.ShapeDtypeStruct((t, LANE), jnp.int32), jax.ShapeDtypeStruct((t, LANE), F32),
                   jax.ShapeDtypeStruct((t, LANE), jnp.int32), jax.ShapeDtypeStruct((1, LANE), jnp.int32)],
        scratch_shapes=[pltpu.VMEM((1, LANE), F32)],
        compiler_params=_params("arbitrary"),
        name="router",
    )(logits)


ROW_UNROLL = 8


def _row_copies(tm, copy):
    def start(t, c):
        for k in range(TOP_K):
            copy(t, k).start()
        return c

    def wait(t, c):
        for k in range(TOP_K):
            copy(t, k).wait()
        return c

    lax.fori_loop(0, tm, start, 0, unroll=ROW_UNROLL)
    lax.fori_loop(0, tm, wait, 0, unroll=ROW_UNROLL)


def _dispatch_kernel(dest_ref, yn_ref, xb_in_ref, xb_ref, sem):
    del xb_in_ref
    _row_copies(yn_ref.shape[0], lambda t, k: pltpu.make_async_copy(
        yn_ref.at[pl.ds(t, 1)], xb_ref.at[pl.ds(dest_ref[t * TOP_K + k], 1)], sem))


def _dispatch(dest, yn, xb, tm):
    t, w = yn.shape
    return pl.pallas_call(
        _dispatch_kernel,
        grid=(t // tm,),
        in_specs=[pl.BlockSpec((tm * TOP_K,), lambda i: (i,), memory_space=pltpu.SMEM),
                  pl.BlockSpec((tm, w), lambda i: (i, 0)),
                  pl.BlockSpec(memory_space=pl.ANY)],
        out_specs=pl.BlockSpec(memory_space=pl.ANY),
        out_shape=jax.ShapeDtypeStruct(xb.shape, xb.dtype),
        scratch_shapes=[pltpu.SemaphoreType.DMA],
        input_output_aliases={2: 0},
        compiler_params=pltpu.CompilerParams(dimension_semantics=("arbitrary",), vmem_limit_bytes=VMEM_LIMIT,
                                             has_side_effects=True),
        name="moe_dispatch",
    )(dest, yn, xb)


MOE_BLOCK = 256


def _moe_kernel(be_ref, nu_ref, xb_ref, wg_ref, wu_ref, wd_ref, yb_ref, wg_bf, wu_bf, wd_bf):
    i = pl.program_id(0)

    @pl.when(i < nu_ref[0])
    def _():
        @pl.when((i == 0) | (be_ref[i] != be_ref[jnp.maximum(i - 1, 0)]))
        def _():
            wg_bf[...] = wg_ref[0].astype(BF16)
            wu_bf[...] = wu_ref[0].astype(BF16)
            wd_bf[...] = wd_ref[0].astype(BF16)

        lo, hi = _unpack_halves(xb_ref[...])
        lo, hi = lo.astype(BF16), hi.astype(BF16)
        w = lo.shape[1]
        a = _dot(lo, wg_bf[:w, :]) + _dot(hi, wg_bf[w:, :])
        u = _dot(lo, wu_bf[:w, :]) + _dot(hi, wu_bf[w:, :])
        hdn = (a * jax.nn.sigmoid(a) * u).astype(BF16)
        yb_ref[...] = _pack_halves(_dot(hdn, wd_bf[...]).astype(BF16))


def _moe_blocks(block_e, n_used, xb, w_gate, w_up, w_down):
    p, w = xb.shape
    d = 2 * w
    nb = p // MOE_BLOCK
    clamp = lambda i, nu: jnp.minimum(i, nu[0] - 1)
    grid_spec = pltpu.PrefetchScalarGridSpec(
        num_scalar_prefetch=2,
        grid=(nb,),
        in_specs=[pl.BlockSpec((MOE_BLOCK, w), lambda i, be, nu: (clamp(i, nu), 0)),
                  pl.BlockSpec((1, d, D_EXPERT), lambda i, be, nu: (be[clamp(i, nu)], 0, 0)),
                  pl.BlockSpec((1, d, D_EXPERT), lambda i, be, nu: (be[clamp(i, nu)], 0, 0)),
                  pl.BlockSpec((1, D_EXPERT, d), lambda i, be, nu: (be[clamp(i, nu)], 0, 0))],
        out_specs=pl.BlockSpec((MOE_BLOCK, w), lambda i, be, nu: (clamp(i, nu), 0)),
        scratch_shapes=[pltpu.VMEM((d, D_EXPERT), BF16), pltpu.VMEM((d, D_EXPERT), BF16),
                        pltpu.VMEM((D_EXPERT, d), BF16)],
    )
    return pl.pallas_call(
        _moe_kernel,
        grid_spec=grid_spec,
        out_shape=jax.ShapeDtypeStruct((p, w), jnp.uint32),
        compiler_params=_params("arbitrary"),
        name="moe_experts",
    )(block_e, n_used, xb, w_gate, w_up, w_down)


def _combine_kernel(dest_ref, y_ref, gate_ref, g_ref, yb_ref, o_ref, buf, sem):
    _row_copies(y_ref.shape[0], lambda t, k: pltpu.make_async_copy(
        yb_ref.at[pl.ds(dest_ref[t * TOP_K + k], 1)], buf.at[k, pl.ds(t, 1)], sem))
    gate = gate_ref[...]
    acc_lo = acc_hi = None
    for k in range(TOP_K):
        lo, hi = _unpack_halves(buf[k])
        gk = gate[:, k:k + 1]
        acc_lo = gk * lo if acc_lo is None else acc_lo + gk * lo
        acc_hi = gk * hi if acc_hi is None else acc_hi + gk * hi
    y = y_ref[...] + jnp.concatenate([acc_lo, acc_hi], axis=-1)
    o_ref[...] = _rms(y, g_ref[...])


def _combine(dest, y, gate, g, yb, tm):
    t, d = y.shape
    return pl.pallas_call(
        _combine_kernel,
        grid=(t // tm,),
        in_specs=[pl.BlockSpec((tm * TOP_K,), lambda i: (i,), memory_space=pltpu.SMEM),
                  pl.BlockSpec((tm, d), lambda i: (i, 0)), pl.BlockSpec((tm, LANE), lambda i: (i, 0)),
                  _resident((1, d)), pl.BlockSpec(memory_space=pl.ANY)],
        out_specs=pl.BlockSpec((tm, d), lambda i: (i, 0)),
        out_shape=jax.ShapeDtypeStruct((t, d), F32),
        scratch_shapes=[pltpu.VMEM((TOP_K, tm, d // 2), jnp.uint32), pltpu.SemaphoreType.DMA],
        compiler_params=_params("arbitrary"),
        name="moe_combine",
    )(dest, y, gate, g.reshape(1, -1), yb)


def _route_plan(counts, n_slots):
    padded = (counts + MOE_BLOCK - 1) // MOE_BLOCK * MOE_BLOCK
    pend = jnp.cumsum(padded)
    pstart = pend - padded
    n_blocks = -(-n_slots // MOE_BLOCK) + N_EXPERTS
    first_row = jnp.arange(n_blocks, dtype=jnp.int32) * MOE_BLOCK
    block_e = jnp.minimum(jnp.sum(pend[None, :] <= first_row[:, None], axis=1), N_EXPERTS - 1).astype(jnp.int32)
    n_used = (pend[-1] // MOE_BLOCK).astype(jnp.int32).reshape(1)
    return pstart.astype(jnp.int32), block_e, n_used, n_blocks


def _encoder_front(x, p, wts):
    b, s, d = x.shape
    x2d = x.reshape(b * s, d)
    tm = min(512, b * s)
    lat, sm, smt, qkv, z = _inproj(x2d, p["norm_mix"].reshape(1, -1), wts["w_lat"], wts["w_sm"], wts["w_smt"],
                                   wts["w_qkv"], wts["w_z"], tm)
    att = _mla(lat, sm, p["g_q_lora"], p["g_kv_lora"], p["w_uq"], p["w_ukv"], b, s)
    o_f, o_b = _gdn(qkv, sm, smt, p["conv_w"], p["a_log"], p["dt_bias"], b, s)
    return _outproj(x2d, att, o_f, o_b, z, p["g_attn_out"], p["g_gdn_out"], wts["w_out"], p["norm_ffn"],
                    wts["w_r"], wts["b_r"], tm)


def _prep_weights(p):
    w_in = p["w_in"]
    o = np.cumsum([0, Q_LORA, KV_LORA, QK_ROPE, GDN_QKV, D_GDN, NG, NG])
    d = w_in.shape[0]
    half = QK_ROPE // 2
    w_sm = jnp.concatenate([w_in[:, o[2]:o[3]], w_in[:, o[5]:o[7]], jnp.zeros((d, LANE - QK_ROPE - 2 * NG), F32),
                            w_in[:, o[2] + half:o[3]], w_in[:, o[2]:o[2] + half],
                            jnp.zeros((d, LANE - QK_ROPE), F32)], axis=1).astype(BF16)
    w_r = jnp.concatenate([p["w_router_group"], p["w_router_expert"],
                           jnp.zeros((d, LANE - N_GROUPS - N_EXPERTS), F32)], axis=1).astype(BF16)
    b_r = jnp.concatenate([p["b_router_group"], p["b_router_expert"],
                           jnp.zeros((LANE - N_GROUPS - N_EXPERTS,), F32)]).reshape(1, LANE)
    return dict(w_lat=w_in[:, :o[2]].astype(BF16), w_sm=w_sm, w_smt=w_sm[:, :LANE].T, w_qkv=w_in[:, o[3]:o[4]].astype(BF16),
                w_z=w_in[:, o[4]:o[5]].astype(BF16), w_out=p["w_out"].astype(BF16), w_r=w_r, b_r=b_r)


ROW_TILE = 512


def _encode(xs, p):
    wts = _prep_weights(p)
    fronts = [_encoder_front(x, p, wts) for x in xs]
    logits = jnp.concatenate([f[2] for f in fronts], axis=0)
    t = logits.shape[0]
    eid, gate, rank, counts = _router(logits, min(ROW_TILE, t))
    pstart, block_e, n_used, n_blocks = _route_plan(counts[0, :N_EXPERTS], t * TOP_K)
    dest = (jnp.take(pstart, eid[:, :TOP_K]) + rank[:, :TOP_K]).reshape(-1)
    xb = jnp.zeros((n_blocks * MOE_BLOCK, fronts[0][1].shape[1]), jnp.uint32)
    bounds = np.cumsum([0] + [f[0].shape[0] for f in fronts])
    for f, t0, t1 in zip(fronts, bounds[:-1], bounds[1:]):
        xb = _dispatch(dest[t0 * TOP_K:t1 * TOP_K], f[1], xb, min(ROW_TILE, t1 - t0))
    yb = _moe_blocks(block_e, n_used, xb, p["w_gate"], p["w_up"], p["w_down"])
    outs = []
    for x, f, t0, t1 in zip(xs, fronts, bounds[:-1], bounds[1:]):
        out = _combine(dest[t0 * TOP_K:t1 * TOP_K], f[0], gate[t0:t1], p["norm_final"], yb, min(ROW_TILE, t1 - t0))
        outs.append(out.reshape(x.shape))
    return outs


def kernel(x_prompt, x_sample, norm_mix, w_in, g_q_lora, g_kv_lora, w_uq, w_ukv, g_attn_out, conv_w, a_log,
           dt_bias, g_gdn_out, w_out, norm_ffn, w_router_group, b_router_group, w_router_expert,
           b_router_expert, w_gate, w_up, w_down, norm_final):
    p = dict(norm_mix=norm_mix[0], w_in=w_in[0], g_q_lora=g_q_lora[0], g_kv_lora=g_kv_lora[0], w_uq=w_uq[0],
             w_ukv=w_ukv[0], g_attn_out=g_attn_out[0], conv_w=conv_w[0], a_log=a_log[0], dt_bias=dt_bias[0],
             g_gdn_out=g_gdn_out[0], w_out=w_out[0], norm_ffn=norm_ffn[0], w_router_group=w_router_group[0],
             b_router_group=b_router_group[0], w_router_expert=w_router_expert[0],
             b_router_expert=b_router_expert[0], w_gate=w_gate[0], w_up=w_up[0], w_down=w_down[0],
             norm_final=norm_final)
    y_prompt, y_sample = _encode([x_prompt, x_sample], p)
    return (y_prompt, y_sample)
```

```python
import functools
import math

import jax
import jax.numpy as jnp
import numpy as np
from jax import lax
from jax.experimental import pallas as pl
from jax.experimental.pallas import tpu as pltpu

F32 = jnp.float32
BF16 = jnp.bfloat16

D_MODEL = 2048
MLA_HEADS = 8
Q_LORA = 512
KV_LORA = 512
QK_NOPE = 128
QK_ROPE = 64
V_HEAD = 128
ROPE_THETA = 10000.0
GDN_HEADS = 8
GDN_DK = 128
GDN_DV = 128
GDN_QKV = GDN_HEADS * (2 * GDN_DK + GDN_DV)
CONV_K = 5
CHUNK = 64
D_ATTN = MLA_HEADS * V_HEAD
D_GDN = GDN_HEADS * GDN_DV
N_GROUPS = 8
EXPERTS_PER_GROUP = 8
N_EXPERTS = N_GROUPS * EXPERTS_PER_GROUP
TOP_K = 2
D_EXPERT = 512
EPS = 1e-6

LANE = 128
QK_PAD = 256
VMEM_LIMIT = 56 * 1024 * 1024


def _params(*sem, **kw):
    return pltpu.CompilerParams(dimension_semantics=sem, vmem_limit_bytes=VMEM_LIMIT, **kw)


def _resident(shape):
    return pl.BlockSpec(shape, lambda *_: (0,) * len(shape), pipeline_mode=pl.Buffered(1))


def _rms(x, g):
    return x * lax.rsqrt(jnp.mean(x * x, axis=-1, keepdims=True) + EPS) * g


def _dot(a, b):
    return jnp.dot(a, b, preferred_element_type=F32)


def _dot_nt(a, b):
    return lax.dot_general(a, b, (((1,), (1,)), ((), ())), preferred_element_type=F32)


def _pack_halves(x):
    w = x.shape[1] // 2
    lo = pltpu.bitcast(x[:, :w].astype(F32), jnp.uint32)
    hi = pltpu.bitcast(x[:, w:].astype(F32), jnp.uint32)
    return (hi & jnp.uint32(0xFFFF0000)) | (lo >> 16)


def _unpack_halves(u):
    lo = pltpu.bitcast(u << 16, F32)
    hi = pltpu.bitcast(u & jnp.uint32(0xFFFF0000), F32)
    return lo, hi


def _inproj_kernel(x_ref, g_ref, w_lat_ref, w_sm_ref, w_smt_ref, w_qkv_ref, w_z_ref,
                   lat_ref, sm_ref, smt_ref, qkv_ref, z_ref):
    xn = _rms(x_ref[...], g_ref[...]).astype(BF16)
    lat_ref[...] = _dot(xn, w_lat_ref[...]).astype(BF16)
    sm_ref[...] = _dot(xn, w_sm_ref[...])
    smt_ref[...] = _dot_nt(w_smt_ref[...], xn)
    qkv_ref[...] = _dot(xn, w_qkv_ref[...]).astype(BF16)
    z_ref[...] = _dot(xn, w_z_ref[...]).astype(BF16)


def _inproj(x2d, g, w_lat, w_sm, w_smt, w_qkv, w_z, tm):
    t, d = x2d.shape
    n_lat, n_sm, n_qkv, n_z = w_lat.shape[1], w_sm.shape[1], w_qkv.shape[1], w_z.shape[1]
    n_smt = w_smt.shape[0]
    row = lambda n: pl.BlockSpec((tm, n), lambda i: (i, 0))
    return pl.pallas_call(
        _inproj_kernel,
        grid=(t // tm,),
        in_specs=[row(d), _resident((1, d)), _resident(w_lat.shape), _resident(w_sm.shape),
                  _resident(w_smt.shape), _resident(w_qkv.shape), _resident(w_z.shape)],
        out_specs=[row(n_lat), row(n_sm), pl.BlockSpec((n_smt, tm), lambda i: (0, i)), row(n_qkv), row(n_z)],
        out_shape=[jax.ShapeDtypeStruct((t, n_lat), BF16), jax.ShapeDtypeStruct((t, n_sm), F32),
                   jax.ShapeDtypeStruct((n_smt, t), F32), jax.ShapeDtypeStruct((t, n_qkv), BF16),
                   jax.ShapeDtypeStruct((t, n_z), BF16)],
        compiler_params=_params("arbitrary"),
        name="inproj",
    )(x2d, g, w_lat, w_sm, w_smt, w_qkv, w_z)


def _mla_up_kernel(lat_ref, sm_ref, gq_ref, gkv_ref, wqt_ref, wk_ref, wvt_ref,
                   cos_t_ref, sin_t_ref, ck_ref, sk_ref, qt_ref, k_ref, vt_ref):
    lat = lat_ref[...].astype(F32)
    cqn = _rms(lat[:, :Q_LORA], gq_ref[...]).astype(BF16)
    ckvn = _rms(lat[:, Q_LORA:], gkv_ref[...]).astype(BF16)
    half = QK_ROPE // 2
    dqk = QK_NOPE + QK_ROPE
    qt = _dot_nt(wqt_ref[...], cqn)
    cos_t, sin_t = cos_t_ref[...], sin_t_ref[...]
    zero = jnp.zeros((QK_PAD - dqk, qt.shape[1]), BF16)
    for h in range(MLA_HEADS):
        lo = h * dqk
        x1 = qt[lo + QK_NOPE:lo + QK_NOPE + half, :]
        x2 = qt[lo + QK_NOPE + half:lo + dqk, :]
        qt_ref[0, h, 0, 0:QK_NOPE, :] = qt[lo:lo + QK_NOPE, :].astype(BF16)
        qt_ref[0, h, 0, QK_NOPE:QK_NOPE + half, :] = (x1 * cos_t - x2 * sin_t).astype(BF16)
        qt_ref[0, h, 0, QK_NOPE + half:dqk, :] = (x2 * cos_t + x1 * sin_t).astype(BF16)
        qt_ref[0, h, 0, dqk:, :] = zero
    kn = _dot(ckvn, wk_ref[...])
    sm = sm_ref[...]
    pe = (sm[:, :LANE] * ck_ref[...] + sm[:, LANE:] * sk_ref[...]).astype(BF16)
    for h in range(MLA_HEADS):
        k_ref[:, h * QK_PAD:h * QK_PAD + QK_NOPE] = kn[:, h * QK_NOPE:(h + 1) * QK_NOPE].astype(BF16)
        k_ref[:, h * QK_PAD + QK_NOPE:(h + 1) * QK_PAD] = pe
    vt = _dot_nt(wvt_ref[...], ckvn)
    for h in range(MLA_HEADS):
        vt_ref[0, h, 0, 0:V_HEAD, :] = vt[h * V_HEAD:(h + 1) * V_HEAD, :].astype(BF16)
        vt_ref[0, h, 0, V_HEAD:, :] = jnp.ones((V_ROWS - V_HEAD, vt.shape[1]), BF16)


def _mla_up(lat, sm, gq, gkv, wqt, wk, wvt, cos_t, sin_t, ck, sk, b, s, ts):
    t = b * s
    nj = s // ts
    hk = MLA_HEADS * QK_PAD
    return pl.pallas_call(
        _mla_up_kernel,
        grid=(b, nj),
        in_specs=[pl.BlockSpec((ts, lat.shape[1]), lambda bi, j: (bi * nj + j, 0)),
                  pl.BlockSpec((ts, sm.shape[1]), lambda bi, j: (bi * nj + j, 0)),
                  _resident(gq.shape), _resident(gkv.shape), _resident(wqt.shape), _resident(wk.shape),
                  _resident(wvt.shape),
                  pl.BlockSpec((QK_ROPE // 2, ts), lambda bi, j: (0, j)),
                  pl.BlockSpec((QK_ROPE // 2, ts), lambda bi, j: (0, j)),
                  pl.BlockSpec((ts, LANE), lambda bi, j: (j, 0)),
                  pl.BlockSpec((ts, LANE), lambda bi, j: (j, 0))],
        out_specs=[pl.BlockSpec((1, MLA_HEADS, 1, QK_PAD, ts), lambda bi, j: (bi, 0, j, 0, 0)),
                   pl.BlockSpec((ts, hk), lambda bi, j: (bi * nj + j, 0)),
                   pl.BlockSpec((1, MLA_HEADS, 1, V_ROWS, ts), lambda bi, j: (bi, 0, j, 0, 0))],
        out_shape=[jax.ShapeDtypeStruct((b, MLA_HEADS, nj, QK_PAD, ts), BF16),
                   jax.ShapeDtypeStruct((t, hk), BF16),
                   jax.ShapeDtypeStruct((b, MLA_HEADS, nj, V_ROWS, ts), BF16)],
        compiler_params=_params("arbitrary", "arbitrary"),
        name="mla_up",
    )(lat, sm, gq, gkv, wqt, wk, wvt, cos_t, sin_t, ck, sk)


ATTN_UNROLL = 16
V_ROWS = V_HEAD + 16


def _attn_kernel(qt_ref, k_ref, vt_ref, o_ref, m_ref, acc_ref, s_buf, c_buf, *, nk, tk, unroll):
    qt = qt_ref[0, 0, 0]
    m_ref[...] = jnp.full(m_ref.shape, -jnp.inf, F32)
    acc_ref[...] = jnp.zeros(acc_ref.shape, F32)

    def scores(j, buf):
        s = _dot(k_ref[pl.ds(pl.multiple_of(j * tk, tk), tk), :], qt)
        s_buf[buf] = s
        c_buf[buf] = jnp.max(s, axis=0, keepdims=True)

    scores(0, 0)

    def step(j, cur, nxt):
        scores(jnp.minimum(j + 1, nk - 1), nxt)
        m_old = m_ref[...]
        m_new = jnp.maximum(m_old, c_buf[cur])
        alpha = jnp.exp2(m_old - m_new)
        p = jnp.exp2(s_buf[cur] - m_new).astype(BF16)
        acc_ref[...] = alpha * acc_ref[...] + _dot(vt_ref[0, 0, j], p)
        m_ref[...] = m_new

    def body(jj, carry):
        for u in range(unroll):
            step(unroll * jj + u, u % 2, 1 - u % 2)
        return carry

    lax.fori_loop(0, nk // unroll, body, 0)
    acc = acc_ref[...]
    o_ref[...] = (acc[:V_HEAD] / acc[V_HEAD:V_HEAD + 1]).T.astype(o_ref.dtype)


def _attention(qt, k, vt, b, s):
    nq, tq = qt.shape[2], qt.shape[4]
    nk, tk = vt.shape[2], vt.shape[4]
    assert nk % 2 == 0, "the score buffers alternate statically"
    unroll = math.gcd(nk, ATTN_UNROLL)
    return pl.pallas_call(
        functools.partial(_attn_kernel, nk=nk, tk=tk, unroll=unroll),
        grid=(b, MLA_HEADS, nq),
        in_specs=[pl.BlockSpec((1, 1, 1, QK_PAD, tq), lambda bi, h, i: (bi, h, i, 0, 0)),
                  pl.BlockSpec((s, QK_PAD), lambda bi, h, i: (bi, h)),
                  pl.BlockSpec((1, 1, nk, V_ROWS, tk), lambda bi, h, i: (bi, h, 0, 0, 0))],
        out_specs=pl.BlockSpec((tq, V_HEAD), lambda bi, h, i: (bi * nq + i, h)),
        out_shape=jax.ShapeDtypeStruct((b * s, D_ATTN), BF16),
        scratch_shapes=[pltpu.VMEM((1, tq), F32), pltpu.VMEM((V_ROWS, tq), F32),
                        pltpu.VMEM((2, tk, tq), F32), pltpu.VMEM((2, 1, tq), F32)],
        compiler_params=_params("arbitrary", "arbitrary", "arbitrary"),
        name="attention",
    )(qt, k, vt)


def _rope_tables(s):
    inv_freq = ROPE_THETA ** (-jnp.arange(0, QK_ROPE, 2, dtype=F32) / QK_ROPE)
    ang = jnp.arange(s, dtype=F32)[:, None] * inv_freq[None, :]
    return jnp.cos(ang), jnp.sin(ang)


def _prep_mla_weights(w_uq, w_ukv):
    half = QK_ROPE // 2
    scale = (QK_NOPE + QK_ROPE) ** -0.5 * math.log2(math.e)
    wqt = (w_uq * scale).T
    wkv = w_ukv.reshape(KV_LORA, MLA_HEADS, QK_NOPE + V_HEAD)
    wk = wkv[..., :QK_NOPE].reshape(KV_LORA, MLA_HEADS * QK_NOPE)
    wvt = wkv[..., QK_NOPE:].reshape(KV_LORA, MLA_HEADS * V_HEAD).T
    return wqt.astype(BF16), wk.astype(BF16), wvt.astype(BF16)


def _mla(lat, sm, g_q, g_kv, w_uq, w_ukv, b, s):
    ts = min(512, s)
    wqt, wk, wvt = _prep_mla_weights(w_uq, w_ukv)
    cos, sin = _rope_tables(s)
    zeros = jnp.zeros((s, LANE - QK_ROPE), F32)
    ck = jnp.concatenate([cos, cos, zeros], axis=-1)
    sk = jnp.concatenate([-sin, sin, zeros], axis=-1)
    qt, k, vt = _mla_up(lat, sm, g_q.reshape(1, -1), g_kv.reshape(1, -1), wqt, wk, wvt,
                        cos.T, sin.T, ck, sk, b, s, ts)
    return _attention(qt, k, vt, b, s)


HALO = 16
GATE_A = QK_ROPE
GATE_BT = QK_ROPE + 2 * GDN_HEADS
NG = 2 * GDN_HEADS


def _chunk_masks(n, rev):
    ri = lax.broadcasted_iota(jnp.int32, (n, n), 0)
    ci = lax.broadcasted_iota(jnp.int32, (n, n), 1)
    same = (ri // CHUNK) == (ci // CHUNK)
    if rev:
        return same, same & (ri <= ci), same & (ri < ci), ri == ci
    return same, same & (ri >= ci), same & (ri > ci), ri == ci


def _dot_exact(a, b):
    return jnp.dot(a, b, preferred_element_type=F32, precision=lax.Precision.HIGHEST)


def _dot_nt_exact(a, b):
    return lax.dot_general(a, b, (((1,), (1,)), ((), ())), preferred_element_type=F32,
                           precision=lax.Precision.HIGHEST)


def _softplus(x):
    return jnp.maximum(x, 0.0) + jnp.log(1.0 + jnp.exp(-jnp.abs(x)))


def _gdn_prep_kernel(qkv_ref, prev_ref, next_ref, sm_ref, smt_ref, cw_ref, alog_ref, dtb_ref,
                     alogt_ref, dtbt_ref, q_ref, k_ref, v_ref, kt_ref, gc_ref, beta_ref, gct_ref, glt_ref,
                     xs_ref, *, ts):
    j = pl.program_id(1)
    nj = pl.num_programs(1)
    xs_ref[0:HALO, :] = jnp.where(j > 0, prev_ref[...].astype(F32), 0.0)
    xs_ref[HALO:HALO + ts, :] = qkv_ref[...].astype(F32)
    xs_ref[HALO + ts:, :] = jnp.where(j < nj - 1, next_ref[...].astype(F32), 0.0)
    acc = None
    for tap in range(CONV_K):
        lo = HALO - CONV_K // 2 + tap
        term = xs_ref[lo:lo + ts, :] * cw_ref[tap:tap + 1, :]
        acc = term if acc is None else acc + term
    act = acc * jax.nn.sigmoid(acc)
    eye = (lax.broadcasted_iota(jnp.int32, (GDN_DK, GDN_DK), 0)
           == lax.broadcasted_iota(jnp.int32, (GDN_DK, GDN_DK), 1)).astype(BF16)
    hk = GDN_HEADS * GDN_DK
    for h in range(GDN_HEADS):
        qh = act[:, h * GDN_DK:(h + 1) * GDN_DK]
        kh = act[:, hk + h * GDN_DK:hk + (h + 1) * GDN_DK]
        qh = qh * (lax.rsqrt(jnp.sum(qh * qh, axis=-1, keepdims=True) + EPS) * GDN_DK ** -0.5)
        kh = (kh * lax.rsqrt(jnp.sum(kh * kh, axis=-1, keepdims=True) + EPS)).astype(BF16)
        q_ref[:, h * GDN_DK:(h + 1) * GDN_DK] = qh.astype(BF16)
        k_ref[:, h * GDN_DK:(h + 1) * GDN_DK] = kh
        kt_ref[h] = _dot_nt(eye, kh).astype(BF16)
    v_ref[...] = act[:, 2 * hk:].astype(BF16)

    sm = sm_ref[...]
    g = -jnp.exp(alog_ref[...]) * _softplus(sm[:, GATE_A:GATE_A + NG] + dtb_ref[...])
    beta_ref[...] = jax.nn.sigmoid(sm[:, GATE_BT:GATE_BT + NG])
    _, incl_f, _, _ = _chunk_masks(ts, False)
    _, incl_b, _, _ = _chunk_masks(ts, True)
    tri_f = incl_f.astype(F32)
    tri_b = incl_b.astype(F32)
    is_fwd = lax.broadcasted_iota(jnp.int32, (ts, NG), 1) < GDN_HEADS
    gc_ref[...] = jnp.where(is_fwd, _dot_exact(tri_f, g), _dot_exact(tri_b, g))
    smt = smt_ref[...]
    gt = -jnp.exp(alogt_ref[...]) * _softplus(smt[GATE_A:GATE_A + NG, :] + dtbt_ref[...])
    is_fwd_t = lax.broadcasted_iota(jnp.int32, (NG, ts), 0) < GDN_HEADS
    gct_ref[...] = jnp.where(is_fwd_t, _dot_nt_exact(gt, tri_f), _dot_nt_exact(gt, tri_b))
    same, _, _, _ = _chunk_masks(ts, False)
    glt_ref[...] = _dot_exact(gt, same.astype(F32))


def _gdn_prep(qkv, sm, smt, conv_w, a_log, dt_bias, b, s, ts):
    t = b * s
    nj = s // ts
    c = qkv.shape[1]
    hb = ts // HALO
    d = GDN_HEADS * GDN_DK
    tile = lambda n: pl.BlockSpec((ts, n), lambda bi, j: (bi * nj + j, 0))
    tile_t = lambda n: pl.BlockSpec((n, ts), lambda bi, j: (0, bi * nj + j))
    return pl.pallas_call(
        functools.partial(_gdn_prep_kernel, ts=ts),
        grid=(b, nj),
        in_specs=[tile(c),
                  pl.BlockSpec((HALO, c), lambda bi, j: (jnp.maximum((bi * nj + j) * hb - 1, 0), 0)),
                  pl.BlockSpec((HALO, c), lambda bi, j: (jnp.minimum((bi * nj + j + 1) * hb, t // HALO - 1), 0)),
                  tile(sm.shape[1]), tile_t(smt.shape[0]),
                  _resident(conv_w.shape), _resident((1, NG)), _resident((1, NG)),
                  _resident((NG, 1)), _resident((NG, 1))],
        out_specs=[tile(d), tile(d), tile(d),
                   pl.BlockSpec((GDN_HEADS, GDN_DK, ts), lambda bi, j: (0, 0, bi * nj + j)),
                   tile(NG), tile(NG), tile_t(NG), tile_t(NG)],
        out_shape=[jax.ShapeDtypeStruct((t, d), BF16)] * 3
        + [jax.ShapeDtypeStruct((GDN_HEADS, GDN_DK, t), BF16),
           jax.ShapeDtypeStruct((t, NG), F32), jax.ShapeDtypeStruct((t, NG), F32),
           jax.ShapeDtypeStruct((NG, t), F32), jax.ShapeDtypeStruct((NG, t), F32)],
        scratch_shapes=[pltpu.VMEM((ts + 2 * HALO, c), F32)],
        compiler_params=_params("arbitrary", "arbitrary"),
        name="gdn_prep",
    )(qkv, qkv, qkv, sm, smt, conv_w, a_log.reshape(1, NG), dt_bias.reshape(1, NG),
      a_log.reshape(NG, 1), dt_bias.reshape(NG, 1))


STEP = 2 * CHUNK


def _gdn_scan_kernel(q_ref, k_ref, v_ref, kt_ref, gc_ref, beta_ref, gct_ref, glt_ref, o_ref, s_ref, *, rev):
    @pl.when(pl.program_id(1) == 0)
    def _():
        s_ref[...] = jnp.zeros(s_ref.shape, F32)

    _, incl, strict, diag = _chunk_masks(STEP, rev)
    eye = diag.astype(F32)
    lane = lax.broadcasted_iota(jnp.int32, (1, STEP), 1)
    zeros = jnp.zeros((CHUNK, GDN_DV), F32)
    col0 = GDN_HEADS if rev else 0
    heads = range(GDN_HEADS)
    sls = [slice(h * GDN_DK, (h + 1) * GDN_DK) for h in heads]
    q = [q_ref[:, sl] for sl in sls]
    kt = [kt_ref[h] for h in heads]
    gcol = [gc_ref[:, col0 + h:col0 + h + 1] for h in heads]
    bcol = [beta_ref[:, col0 + h:col0 + h + 1] for h in heads]
    grow = [gct_ref[col0 + h:col0 + h + 1, :] for h in heads]
    glrow = [glt_ref[col0 + h:col0 + h + 1, :] for h in heads]
    decay = [jnp.exp(jnp.where(incl, gcol[h] - grow[h], -jnp.inf)) for h in heads]
    egc = [jnp.exp(gcol[h]) for h in heads]
    kb = [k_ref[:, sls[h]].astype(F32) * bcol[h] for h in heads]
    aq = [_dot(jnp.concatenate([kb[h].astype(BF16), q[h]], axis=0), kt[h]) for h in heads]
    power = [-jnp.where(strict, aq[h][:STEP] * decay[h], 0.0) for h in heads]
    intra = [(aq[h][STEP:] * decay[h]).astype(BF16) for h in heads]
    inv = [eye + power[h] for h in heads]
    for _ in range(int(math.log2(CHUNK)) - 1):
        pb = [power[h].astype(BF16) for h in heads]
        power = [_dot(pb[h], pb[h]) for h in heads]
        inv = [inv[h] + _dot(inv[h].astype(BF16), power[h].astype(BF16)) for h in heads]
    rhs = [jnp.concatenate([(v_ref[:, sls[h]].astype(F32) * bcol[h]).astype(BF16),
                            (kb[h] * egc[h]).astype(BF16)], axis=1) for h in heads]
    uw = [_dot(inv[h].astype(BF16), rhs[h]) for h in heads]
    qd = [(q[h].astype(F32) * egc[h]).astype(BF16) for h in heads]
    kdt = [(kt[h].astype(F32) * jnp.exp(glrow[h] - grow[h])).astype(BF16) for h in heads]
    state = [s_ref[h] for h in heads]
    for ch in ((1, 0) if rev else (0, 1)):
        r = slice(ch * CHUNK, (ch + 1) * CHUNK)
        ws = [_dot(jnp.concatenate([uw[h][r, GDN_DV:].astype(BF16), qd[h][r]], axis=0), state[h].astype(BF16))
              for h in heads]
        v_new = [uw[h][r, :GDN_DV] - ws[h][:CHUNK] for h in heads]
        vpad = [jnp.concatenate([v_new[h], zeros] if ch == 0 else [zeros, v_new[h]], axis=0).astype(BF16)
                for h in heads]
        res = [_dot(jnp.concatenate([intra[h][r], kdt[h]], axis=0), vpad[h]) for h in heads]
        for h in heads:
            o_ref[r, sls[h]] = (ws[h][CHUNK:] + res[h][:CHUNK]).astype(o_ref.dtype)
        gl = [jnp.sum(jnp.where(lane == ch * CHUNK, glrow[h], 0.0), axis=-1, keepdims=True) for h in heads]
        state = [state[h] * jnp.exp(gl[h]) + res[h][CHUNK:] for h in heads]
    for h in heads:
        s_ref[h] = state[h]


def _gdn_scan(q, k, v, kt, gc, beta, gct, glt, b, s, rev):
    t = b * s
    n = s // STEP
    d = GDN_HEADS * GDN_DK
    if rev:
        blk = lambda bi, i: bi * n + (n - 1 - i)
    else:
        blk = lambda bi, i: bi * n + i
    tile = lambda w: pl.BlockSpec((STEP, w), lambda bi, i: (blk(bi, i), 0))
    tile_t = lambda w: pl.BlockSpec((w, STEP), lambda bi, i: (0, blk(bi, i)))
    return pl.pallas_call(
        functools.partial(_gdn_scan_kernel, rev=rev),
        grid=(b, n),
        in_specs=[tile(d), tile(d), tile(d),
                  pl.BlockSpec((GDN_HEADS, GDN_DK, STEP), lambda bi, i: (0, 0, blk(bi, i))),
                  tile(NG), tile(NG), tile_t(NG), tile_t(NG)],
        out_specs=tile(d),
        out_shape=jax.ShapeDtypeStruct((t, d), BF16),
        scratch_shapes=[pltpu.VMEM((GDN_HEADS, GDN_DK, GDN_DV), F32)],
        compiler_params=_params("arbitrary", "arbitrary"),
        name="gdn_scan_bwd" if rev else "gdn_scan_fwd",
    )(q, k, v, kt, gc, beta, gct, glt)


def _gdn(qkv, sm, smt, conv_w, a_log, dt_bias, b, s):
    ts = min(256, s)
    q, k, v, kt, gc, beta, gct, glt = _gdn_prep(qkv, sm, smt, conv_w, a_log, dt_bias, b, s, ts)
    o_f = _gdn_scan(q, k, v, kt, gc, beta, gct, glt, b, s, False)
    o_b = _gdn_scan(q, k, v, kt, gc, beta, gct, glt, b, s, True)
    return o_f, o_b


def _outproj_kernel(x_ref, att_ref, of_ref, ob_ref, z_ref, ga_ref, gg_ref, wo_ref, nf_ref, wr_ref, br_ref,
                    y_ref, yn_ref, lg_ref):
    att = _rms(att_ref[...].astype(F32), ga_ref[...]).astype(BF16)
    o = of_ref[...].astype(F32) + ob_ref[...].astype(F32)
    z = z_ref[...].astype(F32)
    gg = gg_ref[...]
    lin = []
    for h in range(GDN_HEADS):
        sl = slice(h * GDN_DV, (h + 1) * GDN_DV)
        zh = z[:, sl]
        lin.append((_rms(o[:, sl], gg) * (zh * jax.nn.sigmoid(zh))).astype(BF16))
    lin = jnp.concatenate(lin, axis=-1)
    y = x_ref[...] + _dot(att, wo_ref[:D_ATTN, :]) + _dot(lin, wo_ref[D_ATTN:, :])
    y_ref[...] = y
    yn = _rms(y, nf_ref[...]).astype(BF16)
    yn_ref[...] = _pack_halves(yn)
    lg_ref[...] = _dot(yn, wr_ref[...]) + br_ref[...]


def _outproj(x2d, att, o_f, o_b, z, g_attn, g_gdn, w_out, norm_ffn, w_r, b_r, tm):
    t, d = x2d.shape
    row = lambda n: pl.BlockSpec((tm, n), lambda i: (i, 0))
    return pl.pallas_call(
        _outproj_kernel,
        grid=(t // tm,),
        in_specs=[row(d), row(D_ATTN), row(D_GDN), row(D_GDN), row(D_GDN), _resident((1, D_ATTN)),
                  _resident((1, GDN_DV)), _resident(w_out.shape), _resident((1, d)), _resident(w_r.shape),
                  _resident((1, LANE))],
        out_specs=[row(d), row(d // 2), row(LANE)],
        out_shape=[jax.ShapeDtypeStruct((t, d), F32), jax.ShapeDtypeStruct((t, d // 2), jnp.uint32),
                   jax.ShapeDtypeStruct((t, LANE), F32)],
        compiler_params=_params("arbitrary"),
        name="outproj",
    )(x2d, att, o_f, o_b, z, g_attn.reshape(1, -1), g_gdn.reshape(1, -1), w_out, norm_ffn.reshape(1, -1),
      w_r, b_r)


def _router_kernel(lg_ref, eid_ref, gate_ref, rank_ref, cnt_ref, carry_ref):
    @pl.when(pl.program_id(0) == 0)
    def _():
        carry_ref[...] = jnp.zeros(carry_ref.shape, F32)

    lg = lg_ref[...]
    lane = lax.broadcasted_iota(jnp.int32, lg.shape, 1)
    neg = -jnp.inf

    def first_max(x):
        mx = jnp.max(x, axis=-1, keepdims=True)
        return mx, jnp.min(jnp.where(x == mx, lane, LANE), axis=-1, keepdims=True)

    gl = jnp.where(lane < N_GROUPS, lg, neg)
    gmax, grp = first_max(gl)
    grp_p = 1.0 / jnp.sum(jnp.exp(gl - gmax), axis=-1, keepdims=True)
    lo = N_GROUPS + grp * EXPERTS_PER_GROUP
    el = jnp.where((lane >= lo) & (lane < lo + EXPERTS_PER_GROUP), lg, neg)
    l1, i1 = first_max(el)
    l2, i2 = first_max(jnp.where(lane == i1, neg, el))
    e = jnp.exp(l2 - l1)
    g1 = grp_p / (1.0 + e)
    e1, e2 = i1 - N_GROUPS, i2 - N_GROUPS
    eid_ref[...] = jnp.where(lane == 0, e1, jnp.where(lane == 1, e2, 0))
    gate_ref[...] = jnp.where(lane == 0, g1, jnp.where(lane == 1, g1 * e, 0.0))
    tm = lg.shape[0]
    oh1, oh2 = lane == e1, lane == e2
    before = (lax.broadcasted_iota(jnp.int32, (tm, tm), 1)
              < lax.broadcasted_iota(jnp.int32, (tm, tm), 0)).astype(BF16)
    r1 = _dot(before, oh1.astype(BF16))
    r2 = _dot(before, oh2.astype(BF16))
    c1 = jnp.sum(oh1.astype(F32), axis=0, keepdims=True)
    c2 = jnp.sum(oh2.astype(F32), axis=0, keepdims=True)
    carry = carry_ref[...]
    rank1 = jnp.sum(jnp.where(oh1, r1 + carry, 0.0), axis=-1, keepdims=True)
    rank2 = jnp.sum(jnp.where(oh2, r2 + (carry + c1), 0.0), axis=-1, keepdims=True)
    rank_ref[...] = jnp.where(lane == 0, rank1, jnp.where(lane == 1, rank2, 0.0)).astype(jnp.int32)
    carry_ref[...] = carry + c1 + c2
    cnt_ref[...] = (carry + c1 + c2).astype(jnp.int32)


def _router(logits, tm):
    t = logits.shape[0]
    row = pl.BlockSpec((tm, LANE), lambda i: (i, 0))
    return pl.pallas_call(
        _router_kernel,
        grid=(t // tm,),
        in_specs=[row],
        out_specs=[row, row, row, pl.BlockSpec((1, LANE), lambda i: (0, 0))],
        out_shape=[jax.ShapeDtypeStruct((t, LANE), jnp.int32), jax.ShapeDtypeStruct((t, LANE), F32),
                   jax.ShapeDtypeStruct((t, LANE), jnp.int32), jax.ShapeDtypeStruct((1, LANE), jnp.int32)],
        scratch_shapes=[pltpu.VMEM((1, LANE), F32)],
        compiler_params=_params("arbitrary"),
        name="router",
    )(logits)


ROW_UNROLL = 8


def _row_copies(tm, copy):
    def start(t, c):
        for k in range(TOP_K):
            copy(t, k).start()
        return c

    def wait(t, c):
        for k in range(TOP_K):
            copy(t, k).wait()
        return c

    lax.fori_loop(0, tm, start, 0, unroll=ROW_UNROLL)
    lax.fori_loop(0, tm, wait, 0, unroll=ROW_UNROLL)


def _dispatch_kernel(dest_ref, yn_ref, xb_in_ref, xb_ref, sem):
    del xb_in_ref
    _row_copies(yn_ref.shape[0], lambda t, k: pltpu.make_async_copy(
        yn_ref.at[pl.ds(t, 1)], xb_ref.at[pl.ds(dest_ref[t * TOP_K + k], 1)], sem))


def _dispatch(dest, yn, xb, tm):
    t, w = yn.shape
    return pl.pallas_call(
        _dispatch_kernel,
        grid=(t // tm,),
        in_specs=[pl.BlockSpec((tm * TOP_K,), lambda i: (i,), memory_space=pltpu.SMEM),
                  pl.BlockSpec((tm, w), lambda i: (i, 0)),
                  pl.BlockSpec(memory_space=pl.ANY)],
        out_specs=pl.BlockSpec(memory_space=pl.ANY),
        out_shape=jax.ShapeDtypeStruct(xb.shape, xb.dtype),
        scratch_shapes=[pltpu.SemaphoreType.DMA],
        input_output_aliases={2: 0},
        compiler_params=pltpu.CompilerParams(dimension_semantics=("arbitrary",), vmem_limit_bytes=VMEM_LIMIT,
                                             has_side_effects=True),
        name="moe_dispatch",
    )(dest, yn, xb)


MOE_BLOCK = 256


def _moe_kernel(be_ref, nu_ref, xb_ref, wg_ref, wu_ref, wd_ref, yb_ref, wg_bf, wu_bf, wd_bf):
    i = pl.program_id(0)

    @pl.when(i < nu_ref[0])
    def _():
        @pl.when((i == 0) | (be_ref[i] != be_ref[jnp.maximum(i - 1, 0)]))
        def _():
            wg_bf[...] = wg_ref[0].astype(BF16)
            wu_bf[...] = wu_ref[0].astype(BF16)
            wd_bf[...] = wd_ref[0].astype(BF16)

        lo, hi = _unpack_halves(xb_ref[...])
        lo, hi = lo.astype(BF16), hi.astype(BF16)
        w = lo.shape[1]
        a = _dot(lo, wg_bf[:w, :]) + _dot(hi, wg_bf[w:, :])
        u = _dot(lo, wu_bf[:w, :]) + _dot(hi, wu_bf[w:, :])
        hdn = (a * jax.nn.sigmoid(a) * u).astype(BF16)
        yb_ref[...] = _pack_halves(_dot(hdn, wd_bf[...]).astype(BF16))


def _moe_blocks(block_e, n_used, xb, w_gate, w_up, w_down):
    p, w = xb.shape
    d = 2 * w
    nb = p // MOE_BLOCK
    clamp = lambda i, nu: jnp.minimum(i, nu[0] - 1)
    grid_spec = pltpu.PrefetchScalarGridSpec(
        num_scalar_prefetch=2,
        grid=(nb,),
        in_specs=[pl.BlockSpec((MOE_BLOCK, w), lambda i, be, nu: (clamp(i, nu), 0)),
                  pl.BlockSpec((1, d, D_EXPERT), lambda i, be, nu: (be[clamp(i, nu)], 0, 0)),
                  pl.BlockSpec((1, d, D_EXPERT), lambda i, be, nu: (be[clamp(i, nu)], 0, 0)),
                  pl.BlockSpec((1, D_EXPERT, d), lambda i, be, nu: (be[clamp(i, nu)], 0, 0))],
        out_specs=pl.BlockSpec((MOE_BLOCK, w), lambda i, be, nu: (clamp(i, nu), 0)),
        scratch_shapes=[pltpu.VMEM((d, D_EXPERT), BF16), pltpu.VMEM((d, D_EXPERT), BF16),
                        pltpu.VMEM((D_EXPERT, d), BF16)],
    )
    return pl.pallas_call(
        _moe_kernel,
        grid_spec=grid_spec,
        out_shape=jax.ShapeDtypeStruct((p, w), jnp.uint32),
        compiler_params=_params("arbitrary"),
        name="moe_experts",
    )(block_e, n_used, xb, w_gate, w_up, w_down)


def _combine_kernel(dest_ref, y_ref, gate_ref, g_ref, yb_ref, o_ref, buf, sem):
    _row_copies(y_ref.shape[0], lambda t, k: pltpu.make_async_copy(
        yb_ref.at[pl.ds(dest_ref[t * TOP_K + k], 1)], buf.at[k, pl.ds(t, 1)], sem))
    gate = gate_ref[...]
    acc_lo = acc_hi = None
    for k in range(TOP_K):
        lo, hi = _unpack_halves(buf[k])
        gk = gate[:, k:k + 1]
        acc_lo = gk * lo if acc_lo is None else acc_lo + gk * lo
        acc_hi = gk * hi if acc_hi is None else acc_hi + gk * hi
    y = y_ref[...] + jnp.concatenate([acc_lo, acc_hi], axis=-1)
    o_ref[...] = _rms(y, g_ref[...])


def _combine(dest, y, gate, g, yb, tm):
    t, d = y.shape
    return pl.pallas_call(
        _combine_kernel,
        grid=(t // tm,),
        in_specs=[pl.BlockSpec((tm * TOP_K,), lambda i: (i,), memory_space=pltpu.SMEM),
                  pl.BlockSpec((tm, d), lambda i: (i, 0)), pl.BlockSpec((tm, LANE), lambda i: (i, 0)),
                  _resident((1, d)), pl.BlockSpec(memory_space=pl.ANY)],
        out_specs=pl.BlockSpec((tm, d), lambda i: (i, 0)),
        out_shape=jax.ShapeDtypeStruct((t, d), F32),
        scratch_shapes=[pltpu.VMEM((TOP_K, tm, d // 2), jnp.uint32), pltpu.SemaphoreType.DMA],
        compiler_params=_params("arbitrary"),
        name="moe_combine",
    )(dest, y, gate, g.reshape(1, -1), yb)


def _route_plan(counts, n_slots):
    padded = (counts + MOE_BLOCK - 1) // MOE_BLOCK * MOE_BLOCK
    pend = jnp.cumsum(padded)
    pstart = pend - padded
    n_blocks = -(-n_slots // MOE_BLOCK) + N_EXPERTS
    first_row = jnp.arange(n_blocks, dtype=jnp.int32) * MOE_BLOCK
    block_e = jnp.minimum(jnp.sum(pend[None, :] <= first_row[:, None], axis=1), N_EXPERTS - 1).astype(jnp.int32)
    n_used = (pend[-1] // MOE_BLOCK).astype(jnp.int32).reshape(1)
    return pstart.astype(jnp.int32), block_e, n_used, n_blocks


def _encoder_front(x, p, wts):
    b, s, d = x.shape
    x2d = x.reshape(b * s, d)
    tm = min(512, b * s)
    lat, sm, smt, qkv, z = _inproj(x2d, p["norm_mix"].reshape(1, -1), wts["w_lat"], wts["w_sm"], wts["w_smt"],
                                   wts["w_qkv"], wts["w_z"], tm)
    att = _mla(lat, sm, p["g_q_lora"], p["g_kv_lora"], p["w_uq"], p["w_ukv"], b, s)
    o_f, o_b = _gdn(qkv, sm, smt, p["conv_w"], p["a_log"], p["dt_bias"], b, s)
    return _outproj(x2d, att, o_f, o_b, z, p["g_attn_out"], p["g_gdn_out"], wts["w_out"], p["norm_ffn"],
                    wts["w_r"], wts["b_r"], tm)


def _prep_weights(p):
    w_in = p["w_in"]
    o = np.cumsum([0, Q_LORA, KV_LORA, QK_ROPE, GDN_QKV, D_GDN, NG, NG])
    d = w_in.shape[0]
    half = QK_ROPE // 2
    w_sm = jnp.concatenate([w_in[:, o[2]:o[3]], w_in[:, o[5]:o[7]], jnp.zeros((d, LANE - QK_ROPE - 2 * NG), F32),
                            w_in[:, o[2] + half:o[3]], w_in[:, o[2]:o[2] + half],
                            jnp.zeros((d, LANE - QK_ROPE), F32)], axis=1).astype(BF16)
    w_r = jnp.concatenate([p["w_router_group"], p["w_router_expert"],
                           jnp.zeros((d, LANE - N_GROUPS - N_EXPERTS), F32)], axis=1).astype(BF16)
    b_r = jnp.concatenate([p["b_router_group"], p["b_router_expert"],
                           jnp.zeros((LANE - N_GROUPS - N_EXPERTS,), F32)]).reshape(1, LANE)
    return dict(w_lat=w_in[:, :o[2]].astype(BF16), w_sm=w_sm, w_smt=w_sm[:, :LANE].T, w_qkv=w_in[:, o[3]:o[4]].astype(BF16),
                w_z=w_in[:, o[4]:o[5]].astype(BF16), w_out=p["w_out"].astype(BF16), w_r=w_r, b_r=b_r)


ROW_TILE = 512


def _encode(xs, p):
    wts = _prep_weights(p)
    fronts = [_encoder_front(x, p, wts) for x in xs]
    logits = jnp.concatenate([f[2] for f in fronts], axis=0)
    t = logits.shape[0]
    eid, gate, rank, counts = _router(logits, min(ROW_TILE, t))
    pstart, block_e, n_used, n_blocks = _route_plan(counts[0, :N_EXPERTS], t * TOP_K)
    dest = (jnp.take(pstart, eid[:, :TOP_K]) + rank[:, :TOP_K]).reshape(-1)
    xb = jnp.zeros((n_blocks * MOE_BLOCK, fronts[0][1].shape[1]), jnp.uint32)
    bounds = np.cumsum([0] + [f[0].shape[0] for f in fronts])
    for f, t0, t1 in zip(fronts, bounds[:-1], bounds[1:]):
        xb = _dispatch(dest[t0 * TOP_K:t1 * TOP_K], f[1], xb, min(ROW_TILE, t1 - t0))
    yb = _moe_blocks(block_e, n_used, xb, p["w_gate"], p["w_up"], p["w_down"])
    outs = []
    for x, f, t0, t1 in zip(xs, fronts, bounds[:-1], bounds[1:]):
        out = _combine(dest[t0 * TOP_K:t1 * TOP_K], f[0], gate[t0:t1], p["norm_final"], yb, min(ROW_TILE, t1 - t0))
        outs.append(out.reshape(x.shape))
    return outs


def kernel(x_prompt, x_sample, norm_mix, w_in, g_q_lora, g_kv_lora, w_uq, w_ukv, g_attn_out, conv_w, a_log,
           dt_bias, g_gdn_out, w_out, norm_ffn, w_router_group, b_router_group, w_router_expert,
           b_router_expert, w_gate, w_up, w_down, norm_final):
    p = dict(norm_mix=norm_mix[0], w_in=w_in[0], g_q_lora=g_q_lora[0], g_kv_lora=g_kv_lora[0], w_uq=w_uq[0],
             w_ukv=w_ukv[0], g_attn_out=g_attn_out[0], conv_w=conv_w[0], a_log=a_log[0], dt_bias=dt_bias[0],
             g_gdn_out=g_gdn_out[0], w_out=w_out[0], norm_ffn=norm_ffn[0], w_router_group=w_router_group[0],
             b_router_group=b_router_group[0], w_router_expert=w_router_expert[0],
             b_router_expert=b_router_expert[0], w_gate=w_gate[0], w_up=w_up[0], w_down=w_down[0],
             norm_final=norm_final)
    y_prompt, y_sample = _encode([x_prompt, x_sample], p)
    return (y_prompt, y_sample)
```

```python
import functools
import math

import jax
import jax.numpy as jnp
import numpy as np
from jax import lax
from jax.experimental import pallas as pl
from jax.experimental.pallas import tpu as pltpu

F32 = jnp.float32
BF16 = jnp.bfloat16

D_MODEL = 2048
MLA_HEADS = 8
Q_LORA = 512
KV_LORA = 512
QK_NOPE = 128
QK_ROPE = 64
V_HEAD = 128
ROPE_THETA = 10000.0
GDN_HEADS = 8
GDN_DK = 128
GDN_DV = 128
GDN_QKV = GDN_HEADS * (2 * GDN_DK + GDN_DV)
CONV_K = 5
CHUNK = 64
D_ATTN = MLA_HEADS * V_HEAD
D_GDN = GDN_HEADS * GDN_DV
N_GROUPS = 8
EXPERTS_PER_GROUP = 8
N_EXPERTS = N_GROUPS * EXPERTS_PER_GROUP
TOP_K = 2
D_EXPERT = 512
EPS = 1e-6

LANE = 128
QK_PAD = 256
VMEM_LIMIT = 56 * 1024 * 1024


def _params(*sem, **kw):
    return pltpu.CompilerParams(dimension_semantics=sem, vmem_limit_bytes=VMEM_LIMIT, **kw)


def _resident(shape):
    return pl.BlockSpec(shape, lambda *_: (0,) * len(shape), pipeline_mode=pl.Buffered(1))


def _rms(x, g):
    return x * lax.rsqrt(jnp.mean(x * x, axis=-1, keepdims=True) + EPS) * g


def _dot(a, b):
    return jnp.dot(a, b, preferred_element_type=F32)


def _dot_nt(a, b):
    return lax.dot_general(a, b, (((1,), (1,)), ((), ())), preferred_element_type=F32)


def _pack_halves(x):
    w = x.shape[1] // 2
    lo = pltpu.bitcast(x[:, :w].astype(F32), jnp.uint32)
    hi = pltpu.bitcast(x[:, w:].astype(F32), jnp.uint32)
    return (hi & jnp.uint32(0xFFFF0000)) | (lo >> 16)


def _unpack_halves(u):
    lo = pltpu.bitcast(u << 16, F32)
    hi = pltpu.bitcast(u & jnp.uint32(0xFFFF0000), F32)
    return lo, hi


def _inproj_kernel(x_ref, g_ref, w_lat_ref, w_sm_ref, w_smt_ref, w_qkv_ref, w_z_ref,
                   lat_ref, sm_ref, smt_ref, qkv_ref, z_ref):
    xn = _rms(x_ref[...], g_ref[...]).astype(BF16)
    lat_ref[...] = _dot(xn, w_lat_ref[...]).astype(BF16)
    sm_ref[...] = _dot(xn, w_sm_ref[...])
    smt_ref[...] = _dot_nt(w_smt_ref[...], xn)
    qkv_ref[...] = _dot(xn, w_qkv_ref[...]).astype(BF16)
    z_ref[...] = _dot(xn, w_z_ref[...]).astype(BF16)


def _inproj(x2d, g, w_lat, w_sm, w_smt, w_qkv, w_z, tm):
    t, d = x2d.shape
    n_lat, n_sm, n_qkv, n_z = w_lat.shape[1], w_sm.shape[1], w_qkv.shape[1], w_z.shape[1]
    n_smt = w_smt.shape[0]
    row = lambda n: pl.BlockSpec((tm, n), lambda i: (i, 0))
    return pl.pallas_call(
        _inproj_kernel,
        grid=(t // tm,),
        in_specs=[row(d), _resident((1, d)), _resident(w_lat.shape), _resident(w_sm.shape),
                  _resident(w_smt.shape), _resident(w_qkv.shape), _resident(w_z.shape)],
        out_specs=[row(n_lat), row(n_sm), pl.BlockSpec((n_smt, tm), lambda i: (0, i)), row(n_qkv), row(n_z)],
        out_shape=[jax.ShapeDtypeStruct((t, n_lat), BF16), jax.ShapeDtypeStruct((t, n_sm), F32),
                   jax.ShapeDtypeStruct((n_smt, t), F32), jax.ShapeDtypeStruct((t, n_qkv), BF16),
                   jax.ShapeDtypeStruct((t, n_z), BF16)],
        compiler_params=_params("arbitrary"),
        name="inproj",
    )(x2d, g, w_lat, w_sm, w_smt, w_qkv, w_z)


def _mla_up_kernel(lat_ref, sm_ref, gq_ref, gkv_ref, wqt_ref, wk_ref, wvt_ref,
                   cos_t_ref, sin_t_ref, ck_ref, sk_ref, qt_ref, k_ref, vt_ref):
    lat = lat_ref[...].astype(F32)
    cqn = _rms(lat[:, :Q_LORA], gq_ref[...]).astype(BF16)
    ckvn = _rms(lat[:, Q_LORA:], gkv_ref[...]).astype(BF16)
    half = QK_ROPE // 2
    dqk = QK_NOPE + QK_ROPE
    qt = _dot_nt(wqt_ref[...], cqn)
    cos_t, sin_t = cos_t_ref[...], sin_t_ref[...]
    zero = jnp.zeros((QK_PAD - dqk, qt.shape[1]), BF16)
    for h in range(MLA_HEADS):
        lo = h * dqk
        x1 = qt[lo + QK_NOPE:lo + QK_NOPE + half, :]
        x2 = qt[lo + QK_NOPE + half:lo + dqk, :]
        qt_ref[0, h, 0, 0:QK_NOPE, :] = qt[lo:lo + QK_NOPE, :].astype(BF16)
        qt_ref[0, h, 0, QK_NOPE:QK_NOPE + half, :] = (x1 * cos_t - x2 * sin_t).astype(BF16)
        qt_ref[0, h, 0, QK_NOPE + half:dqk, :] = (x2 * cos_t + x1 * sin_t).astype(BF16)
        qt_ref[0, h, 0, dqk:, :] = zero
    kn = _dot(ckvn, wk_ref[...])
    sm = sm_ref[...]
    pe = (sm[:, :LANE] * ck_ref[...] + sm[:, LANE:] * sk_ref[...]).astype(BF16)
    for h in range(MLA_HEADS):
        k_ref[:, h * QK_PAD:h * QK_PAD + QK_NOPE] = kn[:, h * QK_NOPE:(h + 1) * QK_NOPE].astype(BF16)
        k_ref[:, h * QK_PAD + QK_NOPE:(h + 1) * QK_PAD] = pe
    vt = _dot_nt(wvt_ref[...], ckvn)
    for h in range(MLA_HEADS):
        vt_ref[0, h, 0, 0:V_HEAD, :] = vt[h * V_HEAD:(h + 1) * V_HEAD, :].astype(BF16)
        vt_ref[0, h, 0, V_HEAD:, :] = jnp.ones((V_ROWS - V_HEAD, vt.shape[1]), BF16)


def _mla_up(lat, sm, gq, gkv, wqt, wk, wvt, cos_t, sin_t, ck, sk, b, s, ts):
    t = b * s
    nj = s // ts
    hk = MLA_HEADS * QK_PAD
    return pl.pallas_call(
        _mla_up_kernel,
        grid=(b, nj),
        in_specs=[pl.BlockSpec((ts, lat.shape[1]), lambda bi, j: (bi * nj + j, 0)),
                  pl.BlockSpec((ts, sm.shape[1]), lambda bi, j: (bi * nj + j, 0)),
                  _resident(gq.shape), _resident(gkv.shape), _resident(wqt.shape), _resident(wk.shape),
                  _resident(wvt.shape),
                  pl.BlockSpec((QK_ROPE // 2, ts), lambda bi, j: (0, j)),
                  pl.BlockSpec((QK_ROPE // 2, ts), lambda bi, j: (0, j)),
                  pl.BlockSpec((ts, LANE), lambda bi, j: (j, 0)),
                  pl.BlockSpec((ts, LANE), lambda bi, j: (j, 0))],
        out_specs=[pl.BlockSpec((1, MLA_HEADS, 1, QK_PAD, ts), lambda bi, j: (bi, 0, j, 0, 0)),
                   pl.BlockSpec((ts, hk), lambda bi, j: (bi * nj + j, 0)),
                   pl.BlockSpec((1, MLA_HEADS, 1, V_ROWS, ts), lambda bi, j: (bi, 0, j, 0, 0))],
        out_shape=[jax.ShapeDtypeStruct((b, MLA_HEADS, nj, QK_PAD, ts), BF16),
                   jax.ShapeDtypeStruct((t, hk), BF16),
                   jax.ShapeDtypeStruct((b, MLA_HEADS, nj, V_ROWS, ts), BF16)],
        compiler_params=_params("arbitrary", "arbitrary"),
        name="mla_up",
    )(lat, sm, gq, gkv, wqt, wk, wvt, cos_t, sin_t, ck, sk)


ATTN_UNROLL = 16
V_ROWS = V_HEAD + 16


def _attn_kernel(qt_ref, k_ref, vt_ref, o_ref, m_ref, acc_ref, s_buf, c_buf, *, nk, tk, unroll):
    qt = qt_ref[0, 0, 0]
    m_ref[...] = jnp.full(m_ref.shape, -jnp.inf, F32)
    acc_ref[...] = jnp.zeros(acc_ref.shape, F32)

    def scores(j, buf):
        s = _dot(k_ref[pl.ds(pl.multiple_of(j * tk, tk), tk), :], qt)
        s_buf[buf] = s
        c_buf[buf] = jnp.max(s, axis=0, keepdims=True)

    scores(0, 0)

    def step(j, cur, nxt):
        scores(jnp.minimum(j + 1, nk - 1), nxt)
        m_old = m_ref[...]
        m_new = jnp.maximum(m_old, c_buf[cur])
        alpha = jnp.exp2(m_old - m_new)
        p = jnp.exp2(s_buf[cur] - m_new).astype(BF16)
        acc_ref[...] = alpha * acc_ref[...] + _dot(vt_ref[0, 0, j], p)
        m_ref[...] = m_new

    def body(jj, carry):
        for u in range(unroll):
            step(unroll * jj + u, u % 2, 1 - u % 2)
        return carry

    lax.fori_loop(0, nk // unroll, body, 0)
    acc = acc_ref[...]
    o_ref[...] = (acc[:V_HEAD] / acc[V_HEAD:V_HEAD + 1]).T.astype(o_ref.dtype)


def _attention(qt, k, vt, b, s):
    nq, tq = qt.shape[2], qt.shape[4]
    nk, tk = vt.shape[2], vt.shape[4]
    assert nk % 2 == 0, "the score buffers alternate statically"
    unroll = math.gcd(nk, ATTN_UNROLL)
    return pl.pallas_call(
        functools.partial(_attn_kernel, nk=nk, tk=tk, unroll=unroll),
        grid=(b, MLA_HEADS, nq),
        in_specs=[pl.BlockSpec((1, 1, 1, QK_PAD, tq), lambda bi, h, i: (bi, h, i, 0, 0)),
                  pl.BlockSpec((s, QK_PAD), lambda bi, h, i: (bi, h)),
                  pl.BlockSpec((1, 1, nk, V_ROWS, tk), lambda bi, h, i: (bi, h, 0, 0, 0))],
        out_specs=pl.BlockSpec((tq, V_HEAD), lambda bi, h, i: (bi * nq + i, h)),
        out_shape=jax.ShapeDtypeStruct((b * s, D_ATTN), BF16),
        scratch_shapes=[pltpu.VMEM((1, tq), F32), pltpu.VMEM((V_ROWS, tq), F32),
                        pltpu.VMEM((2, tk, tq), F32), pltpu.VMEM((2, 1, tq), F32)],
        compiler_params=_params("arbitrary", "arbitrary", "arbitrary"),
        name="attention",
    )(qt, k, vt)


def _rope_tables(s):
    inv_freq = np.float32(ROPE_THETA) ** (-np.arange(0, QK_ROPE, 2, dtype=np.float32) / np.float32(QK_ROPE))
    ang = np.arange(s, dtype=np.float32)[:, None] * inv_freq[None, :].astype(np.float32)
    return np.cos(ang.astype(np.float64)).astype(np.float32), np.sin(ang.astype(np.float64)).astype(np.float32)


def _prep_mla_weights(w_uq, w_ukv):
    half = QK_ROPE // 2
    scale = (QK_NOPE + QK_ROPE) ** -0.5 * math.log2(math.e)
    wqt = (w_uq * scale).T
    wkv = w_ukv.reshape(KV_LORA, MLA_HEADS, QK_NOPE + V_HEAD)
    wk = wkv[..., :QK_NOPE].reshape(KV_LORA, MLA_HEADS * QK_NOPE)
    wvt = wkv[..., QK_NOPE:].reshape(KV_LORA, MLA_HEADS * V_HEAD).T
    return wqt.astype(BF16), wk.astype(BF16), wvt.astype(BF16)


def _mla(lat, sm, g_q, g_kv, w_uq, w_ukv, b, s):
    ts = min(512, s)
    wqt, wk, wvt = _prep_mla_weights(w_uq, w_ukv)
    cos, sin = _rope_tables(s)
    zeros = np.zeros((s, LANE - QK_ROPE), np.float32)
    ck = np.concatenate([cos, cos, zeros], axis=-1)
    sk = np.concatenate([-sin, sin, zeros], axis=-1)
    qt, k, vt = _mla_up(lat, sm, g_q.reshape(1, -1), g_kv.reshape(1, -1), wqt, wk, wvt,
                        jnp.asarray(cos.T), jnp.asarray(sin.T), jnp.asarray(ck), jnp.asarray(sk), b, s, ts)
    return _attention(qt, k, vt, b, s)


HALO = 16
GATE_A = QK_ROPE
GATE_BT = QK_ROPE + 2 * GDN_HEADS
NG = 2 * GDN_HEADS


def _chunk_masks(n, rev):
    ri = lax.broadcasted_iota(jnp.int32, (n, n), 0)
    ci = lax.broadcasted_iota(jnp.int32, (n, n), 1)
    same = (ri // CHUNK) == (ci // CHUNK)
    if rev:
        return same, same & (ri <= ci), same & (ri < ci), ri == ci
    return same, same & (ri >= ci), same & (ri > ci), ri == ci


def _dot_exact(a, b):
    return jnp.dot(a, b, preferred_element_type=F32, precision=lax.Precision.HIGHEST)


def _dot_nt_exact(a, b):
    return lax.dot_general(a, b, (((1,), (1,)), ((), ())), preferred_element_type=F32,
                           precision=lax.Precision.HIGHEST)


def _softplus(x):
    return jnp.maximum(x, 0.0) + jnp.log(1.0 + jnp.exp(-jnp.abs(x)))


def _gdn_prep_kernel(qkv_ref, prev_ref, next_ref, sm_ref, smt_ref, cw_ref, alog_ref, dtb_ref,
                     alogt_ref, dtbt_ref, q_ref, k_ref, v_ref, kt_ref, gc_ref, beta_ref, gct_ref, glt_ref,
                     xs_ref, *, ts):
    j = pl.program_id(1)
    nj = pl.num_programs(1)
    xs_ref[0:HALO, :] = jnp.where(j > 0, prev_ref[...].astype(F32), 0.0)
    xs_ref[HALO:HALO + ts, :] = qkv_ref[...].astype(F32)
    xs_ref[HALO + ts:, :] = jnp.where(j < nj - 1, next_ref[...].astype(F32), 0.0)
    acc = None
    for tap in range(CONV_K):
        lo = HALO - CONV_K // 2 + tap
        term = xs_ref[lo:lo + ts, :] * cw_ref[tap:tap + 1, :]
        acc = term if acc is None else acc + term
    act = acc * jax.nn.sigmoid(acc)
    eye = (lax.broadcasted_iota(jnp.int32, (GDN_DK, GDN_DK), 0)
           == lax.broadcasted_iota(jnp.int32, (GDN_DK, GDN_DK), 1)).astype(BF16)
    hk = GDN_HEADS * GDN_DK
    for h in range(GDN_HEADS):
        qh = act[:, h * GDN_DK:(h + 1) * GDN_DK]
        kh = act[:, hk + h * GDN_DK:hk + (h + 1) * GDN_DK]
        qh = qh * (lax.rsqrt(jnp.sum(qh * qh, axis=-1, keepdims=True) + EPS) * GDN_DK ** -0.5)
        kh = (kh * lax.rsqrt(jnp.sum(kh * kh, axis=-1, keepdims=True) + EPS)).astype(BF16)
        q_ref[:, h * GDN_DK:(h + 1) * GDN_DK] = qh.astype(BF16)
        k_ref[:, h * GDN_DK:(h + 1) * GDN_DK] = kh
        kt_ref[h] = _dot_nt(eye, kh).astype(BF16)
    v_ref[...] = act[:, 2 * hk:].astype(BF16)

    sm = sm_ref[...]
    g = -jnp.exp(alog_ref[...]) * _softplus(sm[:, GATE_A:GATE_A + NG] + dtb_ref[...])
    beta_ref[...] = jax.nn.sigmoid(sm[:, GATE_BT:GATE_BT + NG])
    _, incl_f, _, _ = _chunk_masks(ts, False)
    _, incl_b, _, _ = _chunk_masks(ts, True)
    tri_f = incl_f.astype(F32)
    tri_b = incl_b.astype(F32)
    is_fwd = lax.broadcasted_iota(jnp.int32, (ts, NG), 1) < GDN_HEADS
    gc_ref[...] = jnp.where(is_fwd, _dot_exact(tri_f, g), _dot_exact(tri_b, g))
    smt = smt_ref[...]
    gt = -jnp.exp(alogt_ref[...]) * _softplus(smt[GATE_A:GATE_A + NG, :] + dtbt_ref[...])
    is_fwd_t = lax.broadcasted_iota(jnp.int32, (NG, ts), 0) < GDN_HEADS
    gct_ref[...] = jnp.where(is_fwd_t, _dot_nt_exact(gt, tri_f), _dot_nt_exact(gt, tri_b))
    same, _, _, _ = _chunk_masks(ts, False)
    glt_ref[...] = _dot_exact(gt, same.astype(F32))


def _gdn_prep(qkv, sm, smt, conv_w, a_log, dt_bias, b, s, ts):
    t = b * s
    nj = s // ts
    c = qkv.shape[1]
    hb = ts // HALO
    d = GDN_HEADS * GDN_DK
    tile = lambda n: pl.BlockSpec((ts, n), lambda bi, j: (bi * nj + j, 0))
    tile_t = lambda n: pl.BlockSpec((n, ts), lambda bi, j: (0, bi * nj + j))
    return pl.pallas_call(
        functools.partial(_gdn_prep_kernel, ts=ts),
        grid=(b, nj),
        in_specs=[tile(c),
                  pl.BlockSpec((HALO, c), lambda bi, j: (jnp.maximum((bi * nj + j) * hb - 1, 0), 0)),
                  pl.BlockSpec((HALO, c), lambda bi, j: (jnp.minimum((bi * nj + j + 1) * hb, t // HALO - 1), 0)),
                  tile(sm.shape[1]), tile_t(smt.shape[0]),
                  _resident(conv_w.shape), _resident((1, NG)), _resident((1, NG)),
                  _resident((NG, 1)), _resident((NG, 1))],
        out_specs=[tile(d), tile(d), tile(d),
                   pl.BlockSpec((GDN_HEADS, GDN_DK, ts), lambda bi, j: (0, 0, bi * nj + j)),
                   tile(NG), tile(NG), tile_t(NG), tile_t(NG)],
        out_shape=[jax.ShapeDtypeStruct((t, d), BF16)] * 3
        + [jax.ShapeDtypeStruct((GDN_HEADS, GDN_DK, t), BF16),
           jax.ShapeDtypeStruct((t, NG), F32), jax.ShapeDtypeStruct((t, NG), F32),
           jax.ShapeDtypeStruct((NG, t), F32), jax.ShapeDtypeStruct((NG, t), F32)],
        scratch_shapes=[pltpu.VMEM((ts + 2 * HALO, c), F32)],
        compiler_params=_params("arbitrary", "arbitrary"),
        name="gdn_prep",
    )(qkv, qkv, qkv, sm, smt, conv_w, a_log.reshape(1, NG), dt_bias.reshape(1, NG),
      a_log.reshape(NG, 1), dt_bias.reshape(NG, 1))


STEP = 2 * CHUNK


def _gdn_scan_kernel(*refs, revs):
    nd = len(revs)
    ins = [refs[8 * d:8 * d + 8] for d in range(nd)]
    outs = refs[8 * nd:9 * nd]
    s_ref = refs[9 * nd]

    @pl.when(pl.program_id(1) == 0)
    def _():
        s_ref[...] = jnp.zeros(s_ref.shape, F32)

    masks = [_chunk_masks(STEP, rev) for rev in revs]
    lane = lax.broadcasted_iota(jnp.int32, (1, STEP), 1)
    zeros = jnp.zeros((CHUNK, GDN_DV), F32)
    chains = [(d, h) for d in range(nd) for h in range(GDN_HEADS)]
    ids = range(len(chains))
    sl = [slice(h * GDN_DK, (h + 1) * GDN_DK) for _, h in chains]
    col = [(GDN_HEADS if revs[d] else 0) + h for d, h in chains]
    incl = [masks[d][1] for d, _ in chains]
    strict = [masks[d][2] for d, _ in chains]
    eye = masks[0][3].astype(F32)
    q_ref, k_ref, v_ref, kt_ref, gc_ref, beta_ref, gct_ref, glt_ref = (
        [ins[d][i] for d, _ in chains] for i in range(8))
    q = [q_ref[c][:, sl[c]] for c in ids]
    kt = [kt_ref[c][chains[c][1]] for c in ids]
    gcol = [gc_ref[c][:, col[c]:col[c] + 1] for c in ids]
    bcol = [beta_ref[c][:, col[c]:col[c] + 1] for c in ids]
    grow = [gct_ref[c][col[c]:col[c] + 1, :] for c in ids]
    glrow = [glt_ref[c][col[c]:col[c] + 1, :] for c in ids]
    decay = [jnp.exp(jnp.where(incl[c], gcol[c] - grow[c], -jnp.inf)) for c in ids]
    egc = [jnp.exp(gcol[c]) for c in ids]
    kb = [k_ref[c][:, sl[c]].astype(F32) * bcol[c] for c in ids]
    aq = [_dot(jnp.concatenate([kb[c].astype(BF16), q[c]], axis=0), kt[c]) for c in ids]
    power = [-jnp.where(strict[c], aq[c][:STEP] * decay[c], 0.0) for c in ids]
    intra = [(aq[c][STEP:] * decay[c]).astype(BF16) for c in ids]
    inv = [eye + power[c] for c in ids]
    for _ in range(int(math.log2(CHUNK)) - 1):
        pb = [power[c].astype(BF16) for c in ids]
        power = [_dot(pb[c], pb[c]) for c in ids]
        inv = [inv[c] + _dot(inv[c].astype(BF16), power[c].astype(BF16)) for c in ids]
    rhs = [jnp.concatenate([(v_ref[c][:, sl[c]].astype(F32) * bcol[c]).astype(BF16),
                            (kb[c] * egc[c]).astype(BF16)], axis=1) for c in ids]
    uw = [_dot(inv[c].astype(BF16), rhs[c]) for c in ids]
    qd = [(q[c].astype(F32) * egc[c]).astype(BF16) for c in ids]
    kdt = [(kt[c].astype(F32) * jnp.exp(glrow[c] - grow[c])).astype(BF16) for c in ids]
    state = [s_ref[c] for c in ids]
    for half in range(2):
        ch = [1 - half if revs[d] else half for d, _ in chains]
        r = [slice(ch[c] * CHUNK, (ch[c] + 1) * CHUNK) for c in ids]
        ws = [_dot(jnp.concatenate([uw[c][r[c], GDN_DV:].astype(BF16), qd[c][r[c]]], axis=0),
                   state[c].astype(BF16)) for c in ids]
        v_new = [uw[c][r[c], :GDN_DV] - ws[c][:CHUNK] for c in ids]
        vpad = [jnp.concatenate([v_new[c], zeros] if ch[c] == 0 else [zeros, v_new[c]], axis=0).astype(BF16)
                for c in ids]
        res = [_dot(jnp.concatenate([intra[c][r[c]], kdt[c]], axis=0), vpad[c]) for c in ids]
        for c in ids:
            outs[chains[c][0]][r[c], sl[c]] = (ws[c][CHUNK:] + res[c][:CHUNK]).astype(outs[0].dtype)
        gl = [jnp.sum(jnp.where(lane == ch[c] * CHUNK, glrow[c], 0.0), axis=-1, keepdims=True) for c in ids]
        state = [state[c] * jnp.exp(gl[c]) + res[c][CHUNK:] for c in ids]
    for c in ids:
        s_ref[c] = state[c]


def _gdn_scan(q, k, v, kt, gc, beta, gct, glt, b, s, revs):
    t = b * s
    n = s // STEP
    d = GDN_HEADS * GDN_DK
    in_specs, operands, out_specs = [], [], []
    for rev in revs:
        blk = (lambda bi, i: bi * n + (n - 1 - i)) if rev else (lambda bi, i: bi * n + i)
        tile = lambda w, blk=blk: pl.BlockSpec((STEP, w), lambda bi, i: (blk(bi, i), 0))
        tile_t = lambda w, blk=blk: pl.BlockSpec((w, STEP), lambda bi, i: (0, blk(bi, i)))
        in_specs += [tile(d), tile(d), tile(d),
                     pl.BlockSpec((GDN_HEADS, GDN_DK, STEP), lambda bi, i, blk=blk: (0, 0, blk(bi, i))),
                     tile(NG), tile(NG), tile_t(NG), tile_t(NG)]
        operands += [q, k, v, kt, gc, beta, gct, glt]
        out_specs.append(tile(d))
    return pl.pallas_call(
        functools.partial(_gdn_scan_kernel, revs=tuple(revs)),
        grid=(b, n),
        in_specs=in_specs,
        out_specs=out_specs,
        out_shape=[jax.ShapeDtypeStruct((t, d), BF16)] * len(revs),
        scratch_shapes=[pltpu.VMEM((len(revs) * GDN_HEADS, GDN_DK, GDN_DV), F32)],
        compiler_params=_params("arbitrary", "arbitrary"),
        name="gdn_scan",
    )(*operands)


def _gdn(qkv, sm, smt, conv_w, a_log, dt_bias, b, s):
    ts = min(256, s)
    q, k, v, kt, gc, beta, gct, glt = _gdn_prep(qkv, sm, smt, conv_w, a_log, dt_bias, b, s, ts)
    o_f, o_b = _gdn_scan(q, k, v, kt, gc, beta, gct, glt, b, s, (False, True))
    return o_f, o_b


def _outproj_kernel(x_ref, att_ref, of_ref, ob_ref, z_ref, ga_ref, gg_ref, wo_ref, nf_ref, wr_ref, br_ref,
                    y_ref, yn_ref, lg_ref):
    att = _rms(att_ref[...].astype(F32), ga_ref[...]).astype(BF16)
    o = of_ref[...].astype(F32) + ob_ref[...].astype(F32)
    z = z_ref[...].astype(F32)
    gg = gg_ref[...]
    lin = []
    for h in range(GDN_HEADS):
        sl = slice(h * GDN_DV, (h + 1) * GDN_DV)
        zh = z[:, sl]
        lin.append((_rms(o[:, sl], gg) * (zh * jax.nn.sigmoid(zh))).astype(BF16))
    lin = jnp.concatenate(lin, axis=-1)
    y = x_ref[...] + _dot(att, wo_ref[:D_ATTN, :]) + _dot(lin, wo_ref[D_ATTN:, :])
    y_ref[...] = y
    yn = _rms(y, nf_ref[...]).astype(BF16)
    yn_ref[...] = _pack_halves(yn)
    lg_ref[...] = _dot(yn, wr_ref[...]) + br_ref[...]


def _outproj(x2d, att, o_f, o_b, z, g_attn, g_gdn, w_out, norm_ffn, w_r, b_r, tm):
    t, d = x2d.shape
    row = lambda n: pl.BlockSpec((tm, n), lambda i: (i, 0))
    return pl.pallas_call(
        _outproj_kernel,
        grid=(t // tm,),
        in_specs=[row(d), row(D_ATTN), row(D_GDN), row(D_GDN), row(D_GDN), _resident((1, D_ATTN)),
                  _resident((1, GDN_DV)), _resident(w_out.shape), _resident((1, d)), _resident(w_r.shape),
                  _resident((1, LANE))],
        out_specs=[row(d), row(d // 2), row(LANE)],
        out_shape=[jax.ShapeDtypeStruct((t, d), F32), jax.ShapeDtypeStruct((t, d // 2), jnp.uint32),
                   jax.ShapeDtypeStruct((t, LANE), F32)],
        compiler_params=_params("arbitrary"),
        name="outproj",
    )(x2d, att, o_f, o_b, z, g_attn.reshape(1, -1), g_gdn.reshape(1, -1), w_out, norm_ffn.reshape(1, -1),
      w_r, b_r)


def _router_kernel(lg_ref, eid_ref, gate_ref, rank_ref, cnt_ref, carry_ref):
    @pl.when(pl.program_id(0) == 0)
    def _():
        carry_ref[...] = jnp.zeros(carry_ref.shape, F32)

    lg = lg_ref[...]
    lane = lax.broadcasted_iota(jnp.int32, lg.shape, 1)
    neg = -jnp.inf

    def first_max(x):
        mx = jnp.max(x, axis=-1, keepdims=True)
        return mx, jnp.min(jnp.where(x == mx, lane, LANE), axis=-1, keepdims=True)

    gl = jnp.where(lane < N_GROUPS, lg, neg)
    gmax, grp = first_max(gl)
    grp_p = 1.0 / jnp.sum(jnp.exp(gl - gmax), axis=-1, keepdims=True)
    lo = N_GROUPS + grp * EXPERTS_PER_GROUP
    el = jnp.where((lane >= lo) & (lane < lo + EXPERTS_PER_GROUP), lg, neg)
    l1, i1 = first_max(el)
    l2, i2 = first_max(jnp.where(lane == i1, neg, el))
    e = jnp.exp(l2 - l1)
    g1 = grp_p / (1.0 + e)
    e1, e2 = i1 - N_GROUPS, i2 - N_GROUPS
    eid_ref[...] = jnp.where(lane == 0, e1, jnp.where(lane == 1, e2, 0))
    gate_ref[...] = jnp.where(lane == 0, g1, jnp.where(lane == 1, g1 * e, 0.0))
    tm = lg.shape[0]
    oh1, oh2 = lane == e1, lane == e2
    before = (lax.broadcasted_iota(jnp.int32, (tm, tm), 1)
              < lax.broadcasted_iota(jnp.int32, (tm, tm), 0)).astype(BF16)
    r1 = _dot(before, oh1.astype(BF16))
    r2 = _dot(before, oh2.astype(BF16))
    c1 = jnp.sum(oh1.astype(F32), axis=0, keepdims=True)
    c2 = jnp.sum(oh2.astype(F32), axis=0, keepdims=True)
    carry = carry_ref[...]
    rank1 = jnp.sum(jnp.where(oh1, r1 + carry, 0.0), axis=-1, keepdims=True)
    rank2 = jnp.sum(jnp.where(oh2, r2 + (carry + c1), 0.0), axis=-1, keepdims=True)
    rank_ref[...] = jnp.where(lane == 0, rank1, jnp.where(lane == 1, rank2, 0.0)).astype(jnp.int32)
    carry_ref[...] = carry + c1 + c2
    cnt_ref[...] = (carry + c1 + c2).astype(jnp.int32)


def _router(logits, tm):
    t = logits.shape[0]
    row = pl.BlockSpec((tm, LANE), lambda i: (i, 0))
    return pl.pallas_call(
        _router_kernel,
        grid=(t // tm,),
        in_specs=[row],
        out_specs=[row, row, row, pl.BlockSpec((1, LANE), lambda i: (0, 0))],
        out_shape=[jax.ShapeDtypeStruct((t, LANE), jnp.int32), jax.ShapeDtypeStruct((t, LANE), F32),
                   jax.ShapeDtypeStruct((t, LANE), jnp.int32), jax.ShapeDtypeStruct((1, LANE), jnp.int32)],
        scratch_shapes=[pltpu.VMEM((1, LANE), F32)],
        compiler_params=_params("arbitrary"),
        name="router",
    )(logits)


ROW_UNROLL = 8


def _row_copies(tm, copy):
    def start(t, c):
        for k in range(TOP_K):
            copy(t, k).start()
        return c

    def wait(t, c):
        for k in range(TOP_K):
            copy(t, k).wait()
        return c

    lax.fori_loop(0, tm, start, 0, unroll=ROW_UNROLL)
    lax.fori_loop(0, tm, wait, 0, unroll=ROW_UNROLL)


def _dispatch_kernel(dest_ref, yn_ref, xb_in_ref, xb_ref, sem):
    del xb_in_ref
    _row_copies(yn_ref.shape[0], lambda t, k: pltpu.make_async_copy(
        yn_ref.at[pl.ds(t, 1)], xb_ref.at[pl.ds(dest_ref[t * TOP_K + k], 1)], sem))


def _dispatch(dest, yn, xb, tm):
    t, w = yn.shape
    return pl.pallas_call(
        _dispatch_kernel,
        grid=(t // tm,),
        in_specs=[pl.BlockSpec((tm * TOP_K,), lambda i: (i,), memory_space=pltpu.SMEM),
                  pl.BlockSpec((tm, w), lambda i: (i, 0)),
                  pl.BlockSpec(memory_space=pl.ANY)],
        out_specs=pl.BlockSpec(memory_space=pl.ANY),
        out_shape=jax.ShapeDtypeStruct(xb.shape, xb.dtype),
        scratch_shapes=[pltpu.SemaphoreType.DMA],
        input_output_aliases={2: 0},
        compiler_params=pltpu.CompilerParams(dimension_semantics=("arbitrary",), vmem_limit_bytes=VMEM_LIMIT,
                                             has_side_effects=True),
        name="moe_dispatch",
    )(dest, yn, xb)


MOE_BLOCK = 256


def _moe_kernel(be_ref, nu_ref, xb_ref, wg_ref, wu_ref, wd_ref, yb_ref, wg_bf, wu_bf, wd_bf):
    i = pl.program_id(0)

    @pl.when(i < nu_ref[0])
    def _():
        @pl.when((i == 0) | (be_ref[i] != be_ref[jnp.maximum(i - 1, 0)]))
        def _():
            wg_bf[...] = wg_ref[0].astype(BF16)
            wu_bf[...] = wu_ref[0].astype(BF16)
            wd_bf[...] = wd_ref[0].astype(BF16)

        lo, hi = _unpack_halves(xb_ref[...])
        lo, hi = lo.astype(BF16), hi.astype(BF16)
        w = lo.shape[1]
        a = _dot(lo, wg_bf[:w, :]) + _dot(hi, wg_bf[w:, :])
        u = _dot(lo, wu_bf[:w, :]) + _dot(hi, wu_bf[w:, :])
        hdn = (a * jax.nn.sigmoid(a) * u).astype(BF16)
        yb_ref[...] = _pack_halves(_dot(hdn, wd_bf[...]).astype(BF16))


def _moe_blocks(block_e, n_used, xb, w_gate, w_up, w_down):
    p, w = xb.shape
    d = 2 * w
    nb = p // MOE_BLOCK
    clamp = lambda i, nu: jnp.minimum(i, nu[0] - 1)
    grid_spec = pltpu.PrefetchScalarGridSpec(
        num_scalar_prefetch=2,
        grid=(nb,),
        in_specs=[pl.BlockSpec((MOE_BLOCK, w), lambda i, be, nu: (clamp(i, nu), 0)),
                  pl.BlockSpec((1, d, D_EXPERT), lambda i, be, nu: (be[clamp(i, nu)], 0, 0)),
                  pl.BlockSpec((1, d, D_EXPERT), lambda i, be, nu: (be[clamp(i, nu)], 0, 0)),
                  pl.BlockSpec((1, D_EXPERT, d), lambda i, be, nu: (be[clamp(i, nu)], 0, 0))],
        out_specs=pl.BlockSpec((MOE_BLOCK, w), lambda i, be, nu: (clamp(i, nu), 0)),
        scratch_shapes=[pltpu.VMEM((d, D_EXPERT), BF16), pltpu.VMEM((d, D_EXPERT), BF16),
                        pltpu.VMEM((D_EXPERT, d), BF16)],
    )
    return pl.pallas_call(
        _moe_kernel,
        grid_spec=grid_spec,
        out_shape=jax.ShapeDtypeStruct((p, w), jnp.uint32),
        compiler_params=_params("arbitrary"),
        name="moe_experts",
    )(block_e, n_used, xb, w_gate, w_up, w_down)


def _combine_kernel(dest_ref, y_ref, gate_ref, g_ref, yb_ref, o_ref, buf, sem):
    _row_copies(y_ref.shape[0], lambda t, k: pltpu.make_async_copy(
        yb_ref.at[pl.ds(dest_ref[t * TOP_K + k], 1)], buf.at[k, pl.ds(t, 1)], sem))
    gate = gate_ref[...]
    acc_lo = acc_hi = None
    for k in range(TOP_K):
        lo, hi = _unpack_halves(buf[k])
        gk = gate[:, k:k + 1]
        acc_lo = gk * lo if acc_lo is None else acc_lo + gk * lo
        acc_hi = gk * hi if acc_hi is None else acc_hi + gk * hi
    y = y_ref[...] + jnp.concatenate([acc_lo, acc_hi], axis=-1)
    o_ref[...] = _rms(y, g_ref[...])


def _combine(dest, y, gate, g, yb, tm):
    t, d = y.shape
    return pl.pallas_call(
        _combine_kernel,
        grid=(t // tm,),
        in_specs=[pl.BlockSpec((tm * TOP_K,), lambda i: (i,), memory_space=pltpu.SMEM),
                  pl.BlockSpec((tm, d), lambda i: (i, 0)), pl.BlockSpec((tm, LANE), lambda i: (i, 0)),
                  _resident((1, d)), pl.BlockSpec(memory_space=pl.ANY)],
        out_specs=pl.BlockSpec((tm, d), lambda i: (i, 0)),
        out_shape=jax.ShapeDtypeStruct((t, d), F32),
        scratch_shapes=[pltpu.VMEM((TOP_K, tm, d // 2), jnp.uint32), pltpu.SemaphoreType.DMA],
        compiler_params=_params("arbitrary"),
        name="moe_combine",
    )(dest, y, gate, g.reshape(1, -1), yb)


def _route_plan(counts, n_slots):
    padded = (counts + MOE_BLOCK - 1) // MOE_BLOCK * MOE_BLOCK
    pend = jnp.cumsum(padded)
    pstart = pend - padded
    n_blocks = -(-n_slots // MOE_BLOCK) + N_EXPERTS
    first_row = jnp.arange(n_blocks, dtype=jnp.int32) * MOE_BLOCK
    block_e = jnp.minimum(jnp.sum(pend[None, :] <= first_row[:, None], axis=1), N_EXPERTS - 1).astype(jnp.int32)
    n_used = (pend[-1] // MOE_BLOCK).astype(jnp.int32).reshape(1)
    return pstart.astype(jnp.int32), block_e, n_used, n_blocks


def _encoder_front(x, p, wts):
    b, s, d = x.shape
    x2d = x.reshape(b * s, d)
    tm = min(512, b * s)
    lat, sm, smt, qkv, z = _inproj(x2d, p["norm_mix"].reshape(1, -1), wts["w_lat"], wts["w_sm"], wts["w_smt"],
                                   wts["w_qkv"], wts["w_z"], tm)
    att = _mla(lat, sm, p["g_q_lora"], p["g_kv_lora"], p["w_uq"], p["w_ukv"], b, s)
    o_f, o_b = _gdn(qkv, sm, smt, p["conv_w"], p["a_log"], p["dt_bias"], b, s)
    return _outproj(x2d, att, o_f, o_b, z, p["g_attn_out"], p["g_gdn_out"], wts["w_out"], p["norm_ffn"],
                    wts["w_r"], wts["b_r"], tm)


def _prep_weights(p):
    w_in = p["w_in"]
    o = np.cumsum([0, Q_LORA, KV_LORA, QK_ROPE, GDN_QKV, D_GDN, NG, NG])
    d = w_in.shape[0]
    half = QK_ROPE // 2
    w_sm = jnp.concatenate([w_in[:, o[2]:o[3]], w_in[:, o[5]:o[7]], jnp.zeros((d, LANE - QK_ROPE - 2 * NG), F32),
                            w_in[:, o[2] + half:o[3]], w_in[:, o[2]:o[2] + half],
                            jnp.zeros((d, LANE - QK_ROPE), F32)], axis=1).astype(BF16)
    w_r = jnp.concatenate([p["w_router_group"], p["w_router_expert"],
                           jnp.zeros((d, LANE - N_GROUPS - N_EXPERTS), F32)], axis=1).astype(BF16)
    b_r = jnp.concatenate([p["b_router_group"], p["b_router_expert"],
                           jnp.zeros((LANE - N_GROUPS - N_EXPERTS,), F32)]).reshape(1, LANE)
    return dict(w_lat=w_in[:, :o[2]].astype(BF16), w_sm=w_sm, w_smt=w_sm[:, :LANE].T, w_qkv=w_in[:, o[3]:o[4]].astype(BF16),
                w_z=w_in[:, o[4]:o[5]].astype(BF16), w_out=p["w_out"].astype(BF16), w_r=w_r, b_r=b_r)


ROW_TILE = 512


def _encode(xs, p):
    wts = _prep_weights(p)
    fronts = [_encoder_front(x, p, wts) for x in xs]
    logits = jnp.concatenate([f[2] for f in fronts], axis=0)
    t = logits.shape[0]
    eid, gate, rank, counts = _router(logits, min(ROW_TILE, t))
    pstart, block_e, n_used, n_blocks = _route_plan(counts[0, :N_EXPERTS], t * TOP_K)
    dest = (jnp.take(pstart, eid[:, :TOP_K]) + rank[:, :TOP_K]).reshape(-1)
    xb = jnp.zeros((n_blocks * MOE_BLOCK, fronts[0][1].shape[1]), jnp.uint32)
    bounds = np.cumsum([0] + [f[0].shape[0] for f in fronts])
    for f, t0, t1 in zip(fronts, bounds[:-1], bounds[1:]):
        xb = _dispatch(dest[t0 * TOP_K:t1 * TOP_K], f[1], xb, min(ROW_TILE, t1 - t0))
    yb = _moe_blocks(block_e, n_used, xb, p["w_gate"], p["w_up"], p["w_down"])
    outs = []
    for x, f, t0, t1 in zip(xs, fronts, bounds[:-1], bounds[1:]):
        out = _combine(dest[t0 * TOP_K:t1 * TOP_K], f[0], gate[t0:t1], p["norm_final"], yb, min(ROW_TILE, t1 - t0))
        outs.append(out.reshape(x.shape))
    return outs


def kernel(x_prompt, x_sample, norm_mix, w_in, g_q_lora, g_kv_lora, w_uq, w_ukv, g_attn_out, conv_w, a_log,
           dt_bias, g_gdn_out, w_out, norm_ffn, w_router_group, b_router_group, w_router_expert,
           b_router_expert, w_gate, w_up, w_down, norm_final):
    p = dict(norm_mix=norm_mix[0], w_in=w_in[0], g_q_lora=g_q_lora[0], g_kv_lora=g_kv_lora[0], w_uq=w_uq[0],
             w_ukv=w_ukv[0], g_attn_out=g_attn_out[0], conv_w=conv_w[0], a_log=a_log[0], dt_bias=dt_bias[0],
             g_gdn_out=g_gdn_out[0], w_out=w_out[0], norm_ffn=norm_ffn[0], w_router_group=w_router_group[0],
             b_router_group=b_router_group[0], w_router_expert=w_router_expert[0],
             b_router_expert=b_router_expert[0], w_gate=w_gate[0], w_up=w_up[0], w_down=w_down[0],
             norm_final=norm_final)
    y_prompt, y_sample = _encode([x_prompt, x_sample], p)
    return (y_prompt, y_sample)
```

```python
import functools
import math

import jax
import jax.numpy as jnp
import numpy as np
from jax import lax
from jax.experimental import pallas as pl
from jax.experimental.pallas import tpu as pltpu

F32 = jnp.float32
BF16 = jnp.bfloat16

D_MODEL = 2048
MLA_HEADS = 8
Q_LORA = 512
KV_LORA = 512
QK_NOPE = 128
QK_ROPE = 64
V_HEAD = 128
ROPE_THETA = 10000.0
GDN_HEADS = 8
GDN_DK = 128
GDN_DV = 128
GDN_QKV = GDN_HEADS * (2 * GDN_DK + GDN_DV)
CONV_K = 5
CHUNK = 64
D_ATTN = MLA_HEADS * V_HEAD
D_GDN = GDN_HEADS * GDN_DV
N_GROUPS = 8
EXPERTS_PER_GROUP = 8
N_EXPERTS = N_GROUPS * EXPERTS_PER_GROUP
TOP_K = 2
D_EXPERT = 512
EPS = 1e-6

LANE = 128
QK_PAD = 256
VMEM_LIMIT = 56 * 1024 * 1024


def _params(*sem, **kw):
    return pltpu.CompilerParams(dimension_semantics=sem, vmem_limit_bytes=VMEM_LIMIT, **kw)


def _resident(shape):
    return pl.BlockSpec(shape, lambda *_: (0,) * len(shape), pipeline_mode=pl.Buffered(1))


def _rms(x, g):
    return x * lax.rsqrt(jnp.mean(x * x, axis=-1, keepdims=True) + EPS) * g


def _dot(a, b):
    return jnp.dot(a, b, preferred_element_type=F32)


def _dot_nt(a, b):
    return lax.dot_general(a, b, (((1,), (1,)), ((), ())), preferred_element_type=F32)


def _pack_halves(x):
    w = x.shape[1] // 2
    lo = pltpu.bitcast(x[:, :w].astype(F32), jnp.uint32)
    hi = pltpu.bitcast(x[:, w:].astype(F32), jnp.uint32)
    return (hi & jnp.uint32(0xFFFF0000)) | (lo >> 16)


ROW_SUB = 8


def _row_tile(r):
    return pl.ds(pl.multiple_of(r * ROW_SUB, ROW_SUB), ROW_SUB)


def _store_rows(ref, u):
    n = u.shape[0]
    for s in range(ROW_SUB):
        ref[pl.ds(s, n, stride=ROW_SUB), :] = u[:, s * LANE:(s + 1) * LANE]


def _load_rows(ref):
    n = ref.shape[0] // ROW_SUB
    return jnp.concatenate([ref[pl.ds(s, n, stride=ROW_SUB), :] for s in range(ROW_SUB)], axis=1)


def _unpack_halves(u):
    lo = pltpu.bitcast(u << 16, F32)
    hi = pltpu.bitcast(u & jnp.uint32(0xFFFF0000), F32)
    return lo, hi


def _inproj_kernel(x_ref, g_ref, w_lat_ref, w_sm_ref, w_smt_ref, w_qkv_ref, w_z_ref,
                   lat_ref, sm_ref, smt_ref, qkv_ref, z_ref):
    xn = _rms(x_ref[...], g_ref[...]).astype(BF16)
    lat_ref[...] = _dot(xn, w_lat_ref[...]).astype(BF16)
    sm_ref[...] = _dot(xn, w_sm_ref[...])
    smt_ref[...] = _dot_nt(w_smt_ref[...], xn)
    qkv_ref[...] = _dot(xn, w_qkv_ref[...]).astype(BF16)
    z_ref[...] = _dot(xn, w_z_ref[...]).astype(BF16)


def _inproj(x2d, g, w_lat, w_sm, w_smt, w_qkv, w_z, tm):
    t, d = x2d.shape
    n_lat, n_sm, n_qkv, n_z = w_lat.shape[1], w_sm.shape[1], w_qkv.shape[1], w_z.shape[1]
    n_smt = w_smt.shape[0]
    row = lambda n: pl.BlockSpec((tm, n), lambda i: (i, 0))
    return pl.pallas_call(
        _inproj_kernel,
        grid=(t // tm,),
        in_specs=[row(d), _resident((1, d)), _resident(w_lat.shape), _resident(w_sm.shape),
                  _resident(w_smt.shape), _resident(w_qkv.shape), _resident(w_z.shape)],
        out_specs=[row(n_lat), row(n_sm), pl.BlockSpec((n_smt, tm), lambda i: (0, i)), row(n_qkv), row(n_z)],
        out_shape=[jax.ShapeDtypeStruct((t, n_lat), BF16), jax.ShapeDtypeStruct((t, n_sm), F32),
                   jax.ShapeDtypeStruct((n_smt, t), F32), jax.ShapeDtypeStruct((t, n_qkv), BF16),
                   jax.ShapeDtypeStruct((t, n_z), BF16)],
        compiler_params=_params("arbitrary"),
        name="inproj",
    )(x2d, g, w_lat, w_sm, w_smt, w_qkv, w_z)


def _mla_up_kernel(lat_ref, sm_ref, gq_ref, gkv_ref, wqt_ref, wk_ref, wvt_ref,
                   cos_t_ref, sin_t_ref, ck_ref, sk_ref, qt_ref, k_ref, vt_ref):
    lat = lat_ref[...].astype(F32)
    cqn = _rms(lat[:, :Q_LORA], gq_ref[...]).astype(BF16)
    ckvn = _rms(lat[:, Q_LORA:], gkv_ref[...]).astype(BF16)
    half = QK_ROPE // 2
    dqk = QK_NOPE + QK_ROPE
    qt = _dot_nt(wqt_ref[...], cqn)
    cos_t, sin_t = cos_t_ref[...], sin_t_ref[...]
    zero = jnp.zeros((QK_PAD - dqk, qt.shape[1]), BF16)
    for h in range(MLA_HEADS):
        lo = h * dqk
        x1 = qt[lo + QK_NOPE:lo + QK_NOPE + half, :]
        x2 = qt[lo + QK_NOPE + half:lo + dqk, :]
        qt_ref[0, h, 0, 0:QK_NOPE, :] = qt[lo:lo + QK_NOPE, :].astype(BF16)
        qt_ref[0, h, 0, QK_NOPE:QK_NOPE + half, :] = (x1 * cos_t - x2 * sin_t).astype(BF16)
        qt_ref[0, h, 0, QK_NOPE + half:dqk, :] = (x2 * cos_t + x1 * sin_t).astype(BF16)
        qt_ref[0, h, 0, dqk:, :] = zero
    kn = _dot(ckvn, wk_ref[...])
    sm = sm_ref[...]
    pe = (sm[:, :LANE] * ck_ref[...] + sm[:, LANE:] * sk_ref[...]).astype(BF16)
    for h in range(MLA_HEADS):
        k_ref[:, h * QK_PAD:h * QK_PAD + QK_NOPE] = kn[:, h * QK_NOPE:(h + 1) * QK_NOPE].astype(BF16)
        k_ref[:, h * QK_PAD + QK_NOPE:(h + 1) * QK_PAD] = pe
    vt = _dot_nt(wvt_ref[...], ckvn)
    for h in range(MLA_HEADS):
        vt_ref[0, h, 0, 0:V_HEAD, :] = vt[h * V_HEAD:(h + 1) * V_HEAD, :].astype(BF16)
        vt_ref[0, h, 0, V_HEAD:, :] = jnp.ones((V_ROWS - V_HEAD, vt.shape[1]), BF16)


def _mla_up(lat, sm, gq, gkv, wqt, wk, wvt, cos_t, sin_t, ck, sk, b, s, ts):
    t = b * s
    nj = s // ts
    hk = MLA_HEADS * QK_PAD
    return pl.pallas_call(
        _mla_up_kernel,
        grid=(b, nj),
        in_specs=[pl.BlockSpec((ts, lat.shape[1]), lambda bi, j: (bi * nj + j, 0)),
                  pl.BlockSpec((ts, sm.shape[1]), lambda bi, j: (bi * nj + j, 0)),
                  _resident(gq.shape), _resident(gkv.shape), _resident(wqt.shape), _resident(wk.shape),
                  _resident(wvt.shape),
                  pl.BlockSpec((QK_ROPE // 2, ts), lambda bi, j: (0, j)),
                  pl.BlockSpec((QK_ROPE // 2, ts), lambda bi, j: (0, j)),
                  pl.BlockSpec((ts, LANE), lambda bi, j: (j, 0)),
                  pl.BlockSpec((ts, LANE), lambda bi, j: (j, 0))],
        out_specs=[pl.BlockSpec((1, MLA_HEADS, 1, QK_PAD, ts), lambda bi, j: (bi, 0, j, 0, 0)),
                   pl.BlockSpec((ts, hk), lambda bi, j: (bi * nj + j, 0)),
                   pl.BlockSpec((1, MLA_HEADS, 1, V_ROWS, ts), lambda bi, j: (bi, 0, j, 0, 0))],
        out_shape=[jax.ShapeDtypeStruct((b, MLA_HEADS, nj, QK_PAD, ts), BF16),
                   jax.ShapeDtypeStruct((t, hk), BF16),
                   jax.ShapeDtypeStruct((b, MLA_HEADS, nj, V_ROWS, ts), BF16)],
        compiler_params=_params("arbitrary", "arbitrary"),
        name="mla_up",
    )(lat, sm, gq, gkv, wqt, wk, wvt, cos_t, sin_t, ck, sk)


ATTN_UNROLL = 16
V_ROWS = V_HEAD + 16


def _attn_kernel(qt_ref, k_ref, vt_ref, o_ref, m_ref, acc_ref, s_buf, c_buf, *, nk, tk, unroll):
    qt = qt_ref[0, 0, 0]
    m_ref[...] = jnp.full(m_ref.shape, -jnp.inf, F32)
    acc_ref[...] = jnp.zeros(acc_ref.shape, F32)

    def scores(j, buf):
        s = _dot(k_ref[pl.ds(pl.multiple_of(j * tk, tk), tk), :], qt)
        s_buf[buf] = s
        c_buf[buf] = jnp.max(s, axis=0, keepdims=True)

    scores(0, 0)

    def step(j, cur, nxt):
        scores(jnp.minimum(j + 1, nk - 1), nxt)
        m_old = m_ref[...]
        m_new = jnp.maximum(m_old, c_buf[cur])
        alpha = jnp.exp2(m_old - m_new)
        p = jnp.exp2(s_buf[cur] - m_new).astype(BF16)
        acc_ref[...] = alpha * acc_ref[...] + _dot(vt_ref[0, 0, j], p)
        m_ref[...] = m_new

    def body(jj, carry):
        for u in range(unroll):
            step(unroll * jj + u, u % 2, 1 - u % 2)
        return carry

    lax.fori_loop(0, nk // unroll, body, 0)
    acc = acc_ref[...]
    o_ref[...] = (acc[:V_HEAD] / acc[V_HEAD:V_HEAD + 1]).T.astype(o_ref.dtype)


def _attention(qt, k, vt, b, s):
    nq, tq = qt.shape[2], qt.shape[4]
    nk, tk = vt.shape[2], vt.shape[4]
    assert nk % 2 == 0, "the score buffers alternate statically"
    unroll = math.gcd(nk, ATTN_UNROLL)
    return pl.pallas_call(
        functools.partial(_attn_kernel, nk=nk, tk=tk, unroll=unroll),
        grid=(b, MLA_HEADS, nq),
        in_specs=[pl.BlockSpec((1, 1, 1, QK_PAD, tq), lambda bi, h, i: (bi, h, i, 0, 0)),
                  pl.BlockSpec((s, QK_PAD), lambda bi, h, i: (bi, h)),
                  pl.BlockSpec((1, 1, nk, V_ROWS, tk), lambda bi, h, i: (bi, h, 0, 0, 0))],
        out_specs=pl.BlockSpec((tq, V_HEAD), lambda bi, h, i: (bi * nq + i, h)),
        out_shape=jax.ShapeDtypeStruct((b * s, D_ATTN), BF16),
        scratch_shapes=[pltpu.VMEM((1, tq), F32), pltpu.VMEM((V_ROWS, tq), F32),
                        pltpu.VMEM((2, tk, tq), F32), pltpu.VMEM((2, 1, tq), F32)],
        compiler_params=_params("arbitrary", "arbitrary", "arbitrary"),
        name="attention",
    )(qt, k, vt)


def _rope_tables(s):
    inv_freq = np.float32(ROPE_THETA) ** (-np.arange(0, QK_ROPE, 2, dtype=np.float32) / np.float32(QK_ROPE))
    ang = np.arange(s, dtype=np.float32)[:, None] * inv_freq[None, :].astype(np.float32)
    return np.cos(ang.astype(np.float64)).astype(np.float32), np.sin(ang.astype(np.float64)).astype(np.float32)


def _prep_mla_weights(w_uq, w_ukv):
    half = QK_ROPE // 2
    scale = (QK_NOPE + QK_ROPE) ** -0.5 * math.log2(math.e)
    wqt = (w_uq * scale).T
    wkv = w_ukv.reshape(KV_LORA, MLA_HEADS, QK_NOPE + V_HEAD)
    wk = wkv[..., :QK_NOPE].reshape(KV_LORA, MLA_HEADS * QK_NOPE)
    wvt = wkv[..., QK_NOPE:].reshape(KV_LORA, MLA_HEADS * V_HEAD).T
    return wqt.astype(BF16), wk.astype(BF16), wvt.astype(BF16)


def _mla(lat, sm, g_q, g_kv, w_uq, w_ukv, b, s):
    ts = min(512, s)
    wqt, wk, wvt = _prep_mla_weights(w_uq, w_ukv)
    cos, sin = _rope_tables(s)
    zeros = np.zeros((s, LANE - QK_ROPE), np.float32)
    ck = np.concatenate([cos, cos, zeros], axis=-1)
    sk = np.concatenate([-sin, sin, zeros], axis=-1)
    qt, k, vt = _mla_up(lat, sm, g_q.reshape(1, -1), g_kv.reshape(1, -1), wqt, wk, wvt,
                        jnp.asarray(cos.T), jnp.asarray(sin.T), jnp.asarray(ck), jnp.asarray(sk), b, s, ts)
    return _attention(qt, k, vt, b, s)


HALO = 16
GATE_A = QK_ROPE
GATE_BT = QK_ROPE + 2 * GDN_HEADS
NG = 2 * GDN_HEADS


def _chunk_masks(n, rev):
    ri = lax.broadcasted_iota(jnp.int32, (n, n), 0)
    ci = lax.broadcasted_iota(jnp.int32, (n, n), 1)
    same = (ri // CHUNK) == (ci // CHUNK)
    if rev:
        return same, same & (ri <= ci), same & (ri < ci), ri == ci
    return same, same & (ri >= ci), same & (ri > ci), ri == ci


def _dot_exact(a, b):
    return jnp.dot(a, b, preferred_element_type=F32, precision=lax.Precision.HIGHEST)


def _dot_nt_exact(a, b):
    return lax.dot_general(a, b, (((1,), (1,)), ((), ())), preferred_element_type=F32,
                           precision=lax.Precision.HIGHEST)


def _softplus(x):
    return jnp.maximum(x, 0.0) + jnp.log(1.0 + jnp.exp(-jnp.abs(x)))


def _gdn_prep_kernel(qkv_ref, prev_ref, next_ref, sm_ref, smt_ref, cw_ref, alog_ref, dtb_ref,
                     alogt_ref, dtbt_ref, q_ref, k_ref, v_ref, kt_ref, gc_ref, beta_ref, gct_ref, glt_ref,
                     xs_ref, *, ts):
    j = pl.program_id(1)
    nj = pl.num_programs(1)
    xs_ref[0:HALO, :] = jnp.where(j > 0, prev_ref[...].astype(F32), 0.0)
    xs_ref[HALO:HALO + ts, :] = qkv_ref[...].astype(F32)
    xs_ref[HALO + ts:, :] = jnp.where(j < nj - 1, next_ref[...].astype(F32), 0.0)
    acc = None
    for tap in range(CONV_K):
        lo = HALO - CONV_K // 2 + tap
        term = xs_ref[lo:lo + ts, :] * cw_ref[tap:tap + 1, :]
        acc = term if acc is None else acc + term
    act = acc * jax.nn.sigmoid(acc)
    eye = (lax.broadcasted_iota(jnp.int32, (GDN_DK, GDN_DK), 0)
           == lax.broadcasted_iota(jnp.int32, (GDN_DK, GDN_DK), 1)).astype(BF16)
    hk = GDN_HEADS * GDN_DK
    for h in range(GDN_HEADS):
        qh = act[:, h * GDN_DK:(h + 1) * GDN_DK]
        kh = act[:, hk + h * GDN_DK:hk + (h + 1) * GDN_DK]
        qh = qh * (lax.rsqrt(jnp.sum(qh * qh, axis=-1, keepdims=True) + EPS) * GDN_DK ** -0.5)
        kh = (kh * lax.rsqrt(jnp.sum(kh * kh, axis=-1, keepdims=True) + EPS)).astype(BF16)
        q_ref[:, h * GDN_DK:(h + 1) * GDN_DK] = qh.astype(BF16)
        k_ref[:, h * GDN_DK:(h + 1) * GDN_DK] = kh
        kt_ref[h] = _dot_nt(eye, kh).astype(BF16)
    v_ref[...] = act[:, 2 * hk:].astype(BF16)

    sm = sm_ref[...]
    g = -jnp.exp(alog_ref[...]) * _softplus(sm[:, GATE_A:GATE_A + NG] + dtb_ref[...])
    beta_ref[...] = jax.nn.sigmoid(sm[:, GATE_BT:GATE_BT + NG])
    _, incl_f, _, _ = _chunk_masks(ts, False)
    _, incl_b, _, _ = _chunk_masks(ts, True)
    tri_f = incl_f.astype(F32)
    tri_b = incl_b.astype(F32)
    is_fwd = lax.broadcasted_iota(jnp.int32, (ts, NG), 1) < GDN_HEADS
    gc_ref[...] = jnp.where(is_fwd, _dot_exact(tri_f, g), _dot_exact(tri_b, g))
    smt = smt_ref[...]
    gt = -jnp.exp(alogt_ref[...]) * _softplus(smt[GATE_A:GATE_A + NG, :] + dtbt_ref[...])
    is_fwd_t = lax.broadcasted_iota(jnp.int32, (NG, ts), 0) < GDN_HEADS
    gct_ref[...] = jnp.where(is_fwd_t, _dot_nt_exact(gt, tri_f), _dot_nt_exact(gt, tri_b))
    same, _, _, _ = _chunk_masks(ts, False)
    glt_ref[...] = _dot_exact(gt, same.astype(F32))


def _gdn_prep(qkv, sm, smt, conv_w, a_log, dt_bias, b, s, ts):
    t = b * s
    nj = s // ts
    c = qkv.shape[1]
    hb = ts // HALO
    d = GDN_HEADS * GDN_DK
    tile = lambda n: pl.BlockSpec((ts, n), lambda bi, j: (bi * nj + j, 0))
    tile_t = lambda n: pl.BlockSpec((n, ts), lambda bi, j: (0, bi * nj + j))
    return pl.pallas_call(
        functools.partial(_gdn_prep_kernel, ts=ts),
        grid=(b, nj),
        in_specs=[tile(c),
                  pl.BlockSpec((HALO, c), lambda bi, j: (jnp.maximum((bi * nj + j) * hb - 1, 0), 0)),
                  pl.BlockSpec((HALO, c), lambda bi, j: (jnp.minimum((bi * nj + j + 1) * hb, t // HALO - 1), 0)),
                  tile(sm.shape[1]), tile_t(smt.shape[0]),
                  _resident(conv_w.shape), _resident((1, NG)), _resident((1, NG)),
                  _resident((NG, 1)), _resident((NG, 1))],
        out_specs=[tile(d), tile(d), tile(d),
                   pl.BlockSpec((GDN_HEADS, GDN_DK, ts), lambda bi, j: (0, 0, bi * nj + j)),
                   tile(NG), tile(NG), tile_t(NG), tile_t(NG)],
        out_shape=[jax.ShapeDtypeStruct((t, d), BF16)] * 3
        + [jax.ShapeDtypeStruct((GDN_HEADS, GDN_DK, t), BF16),
           jax.ShapeDtypeStruct((t, NG), F32), jax.ShapeDtypeStruct((t, NG), F32),
           jax.ShapeDtypeStruct((NG, t), F32), jax.ShapeDtypeStruct((NG, t), F32)],
        scratch_shapes=[pltpu.VMEM((ts + 2 * HALO, c), F32)],
        compiler_params=_params("arbitrary", "arbitrary"),
        name="gdn_prep",
    )(qkv, qkv, qkv, sm, smt, conv_w, a_log.reshape(1, NG), dt_bias.reshape(1, NG),
      a_log.reshape(NG, 1), dt_bias.reshape(NG, 1))


STEP = 2 * CHUNK


def _gdn_scan_kernel(*refs, revs):
    nd = len(revs)
    ins = [refs[8 * d:8 * d + 8] for d in range(nd)]
    outs = refs[8 * nd:9 * nd]
    s_ref = refs[9 * nd]

    @pl.when(pl.program_id(1) == 0)
    def _():
        s_ref[...] = jnp.zeros(s_ref.shape, F32)

    masks = [_chunk_masks(STEP, rev) for rev in revs]
    lane = lax.broadcasted_iota(jnp.int32, (1, STEP), 1)
    zeros = jnp.zeros((CHUNK, GDN_DV), F32)
    chains = [(d, h) for d in range(nd) for h in range(GDN_HEADS)]
    ids = range(len(chains))
    sl = [slice(h * GDN_DK, (h + 1) * GDN_DK) for _, h in chains]
    col = [(GDN_HEADS if revs[d] else 0) + h for d, h in chains]
    incl = [masks[d][1] for d, _ in chains]
    strict = [masks[d][2] for d, _ in chains]
    eye = masks[0][3].astype(F32)
    q_ref, k_ref, v_ref, kt_ref, gc_ref, beta_ref, gct_ref, glt_ref = (
        [ins[d][i] for d, _ in chains] for i in range(8))
    q = [q_ref[c][:, sl[c]] for c in ids]
    kt = [kt_ref[c][chains[c][1]] for c in ids]
    gcol = [gc_ref[c][:, col[c]:col[c] + 1] for c in ids]
    bcol = [beta_ref[c][:, col[c]:col[c] + 1] for c in ids]
    grow = [gct_ref[c][col[c]:col[c] + 1, :] for c in ids]
    glrow = [glt_ref[c][col[c]:col[c] + 1, :] for c in ids]
    decay = [jnp.exp(jnp.where(incl[c], gcol[c] - grow[c], -jnp.inf)) for c in ids]
    egc = [jnp.exp(gcol[c]) for c in ids]
    kb = [k_ref[c][:, sl[c]].astype(F32) * bcol[c] for c in ids]
    aq = [_dot(jnp.concatenate([kb[c].astype(BF16), q[c]], axis=0), kt[c]) for c in ids]
    power = [-jnp.where(strict[c], aq[c][:STEP] * decay[c], 0.0) for c in ids]
    intra = [(aq[c][STEP:] * decay[c]).astype(BF16) for c in ids]
    inv = [eye + power[c] for c in ids]
    for _ in range(int(math.log2(CHUNK)) - 1):
        pb = [power[c].astype(BF16) for c in ids]
        power = [_dot(pb[c], pb[c]) for c in ids]
        inv = [inv[c] + _dot(inv[c].astype(BF16), power[c].astype(BF16)) for c in ids]
    rhs = [jnp.concatenate([(v_ref[c][:, sl[c]].astype(F32) * bcol[c]).astype(BF16),
                            (kb[c] * egc[c]).astype(BF16)], axis=1) for c in ids]
    uw = [_dot(inv[c].astype(BF16), rhs[c]) for c in ids]
    qd = [(q[c].astype(F32) * egc[c]).astype(BF16) for c in ids]
    kdt = [(kt[c].astype(F32) * jnp.exp(glrow[c] - grow[c])).astype(BF16) for c in ids]
    state = [s_ref[c] for c in ids]
    for half in range(2):
        ch = [1 - half if revs[d] else half for d, _ in chains]
        r = [slice(ch[c] * CHUNK, (ch[c] + 1) * CHUNK) for c in ids]
        ws = [_dot(jnp.concatenate([uw[c][r[c], GDN_DV:].astype(BF16), qd[c][r[c]]], axis=0),
                   state[c].astype(BF16)) for c in ids]
        v_new = [uw[c][r[c], :GDN_DV] - ws[c][:CHUNK] for c in ids]
        vpad = [jnp.concatenate([v_new[c], zeros] if ch[c] == 0 else [zeros, v_new[c]], axis=0).astype(BF16)
                for c in ids]
        res = [_dot(jnp.concatenate([intra[c][r[c]], kdt[c]], axis=0), vpad[c]) for c in ids]
        for c in ids:
            outs[chains[c][0]][r[c], sl[c]] = (ws[c][CHUNK:] + res[c][:CHUNK]).astype(outs[0].dtype)
        gl = [jnp.sum(jnp.where(lane == ch[c] * CHUNK, glrow[c], 0.0), axis=-1, keepdims=True) for c in ids]
        state = [state[c] * jnp.exp(gl[c]) + res[c][CHUNK:] for c in ids]
    for c in ids:
        s_ref[c] = state[c]


def _gdn_scan(q, k, v, kt, gc, beta, gct, glt, b, s, revs):
    t = b * s
    n = s // STEP
    d = GDN_HEADS * GDN_DK
    in_specs, operands, out_specs = [], [], []
    for rev in revs:
        blk = (lambda bi, i: bi * n + (n - 1 - i)) if rev else (lambda bi, i: bi * n + i)
        tile = lambda w, blk=blk: pl.BlockSpec((STEP, w), lambda bi, i: (blk(bi, i), 0))
        tile_t = lambda w, blk=blk: pl.BlockSpec((w, STEP), lambda bi, i: (0, blk(bi, i)))
        in_specs += [tile(d), tile(d), tile(d),
                     pl.BlockSpec((GDN_HEADS, GDN_DK, STEP), lambda bi, i, blk=blk: (0, 0, blk(bi, i))),
                     tile(NG), tile(NG), tile_t(NG), tile_t(NG)]
        operands += [q, k, v, kt, gc, beta, gct, glt]
        out_specs.append(tile(d))
    return pl.pallas_call(
        functools.partial(_gdn_scan_kernel, revs=tuple(revs)),
        grid=(b, n),
        in_specs=in_specs,
        out_specs=out_specs,
        out_shape=[jax.ShapeDtypeStruct((t, d), BF16)] * len(revs),
        scratch_shapes=[pltpu.VMEM((len(revs) * GDN_HEADS, GDN_DK, GDN_DV), F32)],
        compiler_params=_params("arbitrary", "arbitrary"),
        name="gdn_scan",
    )(*operands)


def _gdn(qkv, sm, smt, conv_w, a_log, dt_bias, b, s):
    ts = min(256, s)
    q, k, v, kt, gc, beta, gct, glt = _gdn_prep(qkv, sm, smt, conv_w, a_log, dt_bias, b, s, ts)
    o_f, o_b = _gdn_scan(q, k, v, kt, gc, beta, gct, glt, b, s, (False, True))
    return o_f, o_b


def _outproj_kernel(x_ref, att_ref, of_ref, ob_ref, z_ref, ga_ref, gg_ref, wo_ref, nf_ref, wr_ref, br_ref,
                    y_ref, yn_ref, lg_ref):
    att = _rms(att_ref[...].astype(F32), ga_ref[...]).astype(BF16)
    o = of_ref[...].astype(F32) + ob_ref[...].astype(F32)
    z = z_ref[...].astype(F32)
    gg = gg_ref[...]
    lin = []
    for h in range(GDN_HEADS):
        sl = slice(h * GDN_DV, (h + 1) * GDN_DV)
        zh = z[:, sl]
        lin.append((_rms(o[:, sl], gg) * (zh * jax.nn.sigmoid(zh))).astype(BF16))
    lin = jnp.concatenate(lin, axis=-1)
    y = x_ref[...] + _dot(att, wo_ref[:D_ATTN, :]) + _dot(lin, wo_ref[D_ATTN:, :])
    y_ref[...] = y
    yn = _rms(y, nf_ref[...]).astype(BF16)
    _store_rows(yn_ref, _pack_halves(yn))
    lg_ref[...] = _dot(yn, wr_ref[...]) + br_ref[...]


def _outproj(x2d, att, o_f, o_b, z, g_attn, g_gdn, w_out, norm_ffn, w_r, b_r, tm):
    t, d = x2d.shape
    row = lambda n: pl.BlockSpec((tm, n), lambda i: (i, 0))
    return pl.pallas_call(
        _outproj_kernel,
        grid=(t // tm,),
        in_specs=[row(d), row(D_ATTN), row(D_GDN), row(D_GDN), row(D_GDN), _resident((1, D_ATTN)),
                  _resident((1, GDN_DV)), _resident(w_out.shape), _resident((1, d)), _resident(w_r.shape),
                  _resident((1, LANE))],
        out_specs=[row(d), pl.BlockSpec((tm * ROW_SUB, LANE), lambda i: (i, 0)), row(LANE)],
        out_shape=[jax.ShapeDtypeStruct((t, d), F32), jax.ShapeDtypeStruct((t * ROW_SUB, LANE), jnp.uint32),
                   jax.ShapeDtypeStruct((t, LANE), F32)],
        compiler_params=_params("arbitrary"),
        name="outproj",
    )(x2d, att, o_f, o_b, z, g_attn.reshape(1, -1), g_gdn.reshape(1, -1), w_out, norm_ffn.reshape(1, -1),
      w_r, b_r)


SLOT_ROWS = 8


def _router_kernel(lg_ref, gate_ref, slot_ref, cnt_ref, carry_ref):
    @pl.when(pl.program_id(0) == 0)
    def _():
        carry_ref[...] = jnp.zeros(carry_ref.shape, F32)

    lg = lg_ref[...]
    lane = lax.broadcasted_iota(jnp.int32, lg.shape, 1)
    neg = -jnp.inf

    def first_max(x):
        mx = jnp.max(x, axis=-1, keepdims=True)
        return mx, jnp.min(jnp.where(x == mx, lane, LANE), axis=-1, keepdims=True)

    gl = jnp.where(lane < N_GROUPS, lg, neg)
    gmax, grp = first_max(gl)
    grp_p = 1.0 / jnp.sum(jnp.exp(gl - gmax), axis=-1, keepdims=True)
    lo = N_GROUPS + grp * EXPERTS_PER_GROUP
    el = jnp.where((lane >= lo) & (lane < lo + EXPERTS_PER_GROUP), lg, neg)
    l1, i1 = first_max(el)
    l2, i2 = first_max(jnp.where(lane == i1, neg, el))
    e = jnp.exp(l2 - l1)
    g1 = grp_p / (1.0 + e)
    e1, e2 = i1 - N_GROUPS, i2 - N_GROUPS
    gate_ref[...] = jnp.where(lane == 0, g1, jnp.where(lane == 1, g1 * e, 0.0))
    tm = lg.shape[0]
    oh1, oh2 = lane == e1, lane == e2
    before = (lax.broadcasted_iota(jnp.int32, (tm, tm), 1)
              < lax.broadcasted_iota(jnp.int32, (tm, tm), 0)).astype(BF16)
    r1 = _dot(before, oh1.astype(BF16))
    r2 = _dot(before, oh2.astype(BF16))
    c1 = jnp.sum(oh1.astype(F32), axis=0, keepdims=True)
    c2 = jnp.sum(oh2.astype(F32), axis=0, keepdims=True)
    carry = carry_ref[...]
    rank1 = jnp.sum(jnp.where(oh1, r1 + carry, 0.0), axis=-1, keepdims=True)
    rank2 = jnp.sum(jnp.where(oh2, r2 + (carry + c1), 0.0), axis=-1, keepdims=True)
    info = jnp.where(lane == 0, e1.astype(F32), jnp.where(lane == 1, e2.astype(F32),
                     jnp.where(lane == 2, rank1, jnp.where(lane == 3, rank2, 0.0))))
    slot_ref[...] = info.T[:SLOT_ROWS, :].astype(jnp.int32)
    carry_ref[...] = carry + c1 + c2
    cnt_ref[...] = (carry + c1 + c2).astype(jnp.int32)


def _router(logits, tm):
    t = logits.shape[0]
    row = pl.BlockSpec((tm, LANE), lambda i: (i, 0))
    return pl.pallas_call(
        _router_kernel,
        grid=(t // tm,),
        in_specs=[row],
        out_specs=[row, pl.BlockSpec((SLOT_ROWS, tm), lambda i: (0, i)), pl.BlockSpec((1, LANE), lambda i: (0, 0))],
        out_shape=[jax.ShapeDtypeStruct((t, LANE), F32), jax.ShapeDtypeStruct((SLOT_ROWS, t), jnp.int32),
                   jax.ShapeDtypeStruct((1, LANE), jnp.int32)],
        scratch_shapes=[pltpu.VMEM((1, LANE), F32)],
        compiler_params=_params("arbitrary"),
        name="router",
    )(logits)


ROW_UNROLL = 8


def _row_copies(tm, copy):
    def start(t, c):
        for k in range(TOP_K):
            copy(t, k).start()
        return c

    def wait(t, c):
        for k in range(TOP_K):
            copy(t, k).wait()
        return c

    lax.fori_loop(0, tm, start, 0, unroll=ROW_UNROLL)
    lax.fori_loop(0, tm, wait, 0, unroll=ROW_UNROLL)


def _dispatch_kernel(*refs):
    dest_refs, (yn_ref, xb_in_ref, xb_ref, sem) = refs[:TOP_K], refs[TOP_K:]
    del xb_in_ref
    _row_copies(yn_ref.shape[0] // ROW_SUB, lambda t, k: pltpu.make_async_copy(
        yn_ref.at[_row_tile(t)], xb_ref.at[_row_tile(dest_refs[k][t])], sem))


def _dest_specs(tm):
    return [pl.BlockSpec((tm,), lambda i: (i,), memory_space=pltpu.SMEM) for _ in range(TOP_K)]


def _dispatch(dests, yn, xb, tm):
    t = yn.shape[0] // ROW_SUB
    return pl.pallas_call(
        _dispatch_kernel,
        grid=(t // tm,),
        in_specs=_dest_specs(tm) + [pl.BlockSpec((tm * ROW_SUB, LANE), lambda i: (i, 0)),
                                    pl.BlockSpec(memory_space=pl.ANY)],
        out_specs=pl.BlockSpec(memory_space=pl.ANY),
        out_shape=jax.ShapeDtypeStruct(xb.shape, xb.dtype),
        scratch_shapes=[pltpu.SemaphoreType.DMA],
        input_output_aliases={TOP_K + 1: 0},
        compiler_params=_params("arbitrary", has_side_effects=True),
        name="moe_dispatch",
    )(*dests, yn, xb)


MOE_BLOCK = 512


def _moe_kernel(be_ref, nu_ref, xb_ref, wg_ref, wu_ref, wd_ref, yb_ref, wg_bf, wu_bf, wd_bf):
    i = pl.program_id(0)

    @pl.when(i < nu_ref[0])
    def _():
        @pl.when((i == 0) | (be_ref[i] != be_ref[jnp.maximum(i - 1, 0)]))
        def _():
            wg_bf[...] = wg_ref[0].astype(BF16)
            wu_bf[...] = wu_ref[0].astype(BF16)
            wd_bf[...] = wd_ref[0].astype(BF16)

        lo, hi = _unpack_halves(_load_rows(xb_ref))
        lo, hi = lo.astype(BF16), hi.astype(BF16)
        w = lo.shape[1]
        a = _dot(lo, wg_bf[:w, :]) + _dot(hi, wg_bf[w:, :])
        u = _dot(lo, wu_bf[:w, :]) + _dot(hi, wu_bf[w:, :])
        hdn = (a * jax.nn.sigmoid(a) * u).astype(BF16)
        _store_rows(yb_ref, _pack_halves(_dot(hdn, wd_bf[...]).astype(BF16)))


def _moe_blocks(block_e, n_used, xb, w_gate, w_up, w_down):
    p = xb.shape[0] // ROW_SUB
    d = w_gate.shape[1]
    nb = p // MOE_BLOCK
    clamp = lambda i, nu: jnp.minimum(i, nu[0] - 1)
    rows = pl.BlockSpec((MOE_BLOCK * ROW_SUB, LANE), lambda i, be, nu: (clamp(i, nu), 0))
    grid_spec = pltpu.PrefetchScalarGridSpec(
        num_scalar_prefetch=2,
        grid=(nb,),
        in_specs=[rows,
                  pl.BlockSpec((1, d, D_EXPERT), lambda i, be, nu: (be[clamp(i, nu)], 0, 0)),
                  pl.BlockSpec((1, d, D_EXPERT), lambda i, be, nu: (be[clamp(i, nu)], 0, 0)),
                  pl.BlockSpec((1, D_EXPERT, d), lambda i, be, nu: (be[clamp(i, nu)], 0, 0))],
        out_specs=rows,
        scratch_shapes=[pltpu.VMEM((d, D_EXPERT), BF16), pltpu.VMEM((d, D_EXPERT), BF16),
                        pltpu.VMEM((D_EXPERT, d), BF16)],
    )
    return pl.pallas_call(
        _moe_kernel,
        grid_spec=grid_spec,
        out_shape=jax.ShapeDtypeStruct(xb.shape, jnp.uint32),
        compiler_params=_params("arbitrary"),
        name="moe_experts",
    )(block_e, n_used, xb, w_gate, w_up, w_down)


def _combine_kernel(*refs):
    dest_refs, (y_ref, gate_ref, g_ref, yb_ref, o_ref, buf, sem) = refs[:TOP_K], refs[TOP_K:]
    _row_copies(y_ref.shape[0], lambda t, k: pltpu.make_async_copy(
        yb_ref.at[_row_tile(dest_refs[k][t])], buf.at[k, _row_tile(t)], sem))
    gate = gate_ref[...]
    acc_lo = acc_hi = None
    for k in range(TOP_K):
        lo, hi = _unpack_halves(_load_rows(buf.at[k]))
        gk = gate[:, k:k + 1]
        acc_lo = gk * lo if acc_lo is None else acc_lo + gk * lo
        acc_hi = gk * hi if acc_hi is None else acc_hi + gk * hi
    y = y_ref[...] + jnp.concatenate([acc_lo, acc_hi], axis=-1)
    o_ref[...] = _rms(y, g_ref[...])


def _combine(dests, y, gate, g, yb, tm):
    t, d = y.shape
    return pl.pallas_call(
        _combine_kernel,
        grid=(t // tm,),
        in_specs=_dest_specs(tm) + [pl.BlockSpec((tm, d), lambda i: (i, 0)),
                                    pl.BlockSpec((tm, LANE), lambda i: (i, 0)),
                                    _resident((1, d)), pl.BlockSpec(memory_space=pl.ANY)],
        out_specs=pl.BlockSpec((tm, d), lambda i: (i, 0)),
        out_shape=jax.ShapeDtypeStruct((t, d), F32),
        scratch_shapes=[pltpu.VMEM((TOP_K, tm * ROW_SUB, LANE), jnp.uint32), pltpu.SemaphoreType.DMA],
        compiler_params=_params("arbitrary"),
        name="moe_combine",
    )(*dests, y, gate, g.reshape(1, -1), yb)


def _route_plan(counts, n_slots):
    padded = (counts + MOE_BLOCK - 1) // MOE_BLOCK * MOE_BLOCK
    pend = jnp.cumsum(padded)
    pstart = pend - padded
    n_blocks = -(-n_slots // MOE_BLOCK) + N_EXPERTS
    first_row = jnp.arange(n_blocks, dtype=jnp.int32) * MOE_BLOCK
    block_e = jnp.minimum(jnp.sum(pend[None, :] <= first_row[:, None], axis=1), N_EXPERTS - 1).astype(jnp.int32)
    n_used = (pend[-1] // MOE_BLOCK).astype(jnp.int32).reshape(1)
    return pstart.astype(jnp.int32), block_e, n_used, n_blocks


def _encoder_front(x, p, wts):
    b, s, d = x.shape
    x2d = x.reshape(b * s, d)
    tm = min(512, b * s)
    lat, sm, smt, qkv, z = _inproj(x2d, p["norm_mix"].reshape(1, -1), wts["w_lat"], wts["w_sm"], wts["w_smt"],
                                   wts["w_qkv"], wts["w_z"], tm)
    att = _mla(lat, sm, p["g_q_lora"], p["g_kv_lora"], p["w_uq"], p["w_ukv"], b, s)
    o_f, o_b = _gdn(qkv, sm, smt, p["conv_w"], p["a_log"], p["dt_bias"], b, s)
    return _outproj(x2d, att, o_f, o_b, z, p["g_attn_out"], p["g_gdn_out"], wts["w_out"], p["norm_ffn"],
                    wts["w_r"], wts["b_r"], tm)


def _prep_weights(p):
    w_in = p["w_in"]
    o = np.cumsum([0, Q_LORA, KV_LORA, QK_ROPE, GDN_QKV, D_GDN, NG, NG])
    d = w_in.shape[0]
    half = QK_ROPE // 2
    w_sm = jnp.concatenate([w_in[:, o[2]:o[3]], w_in[:, o[5]:o[7]], jnp.zeros((d, LANE - QK_ROPE - 2 * NG), F32),
                            w_in[:, o[2] + half:o[3]], w_in[:, o[2]:o[2] + half],
                            jnp.zeros((d, LANE - QK_ROPE), F32)], axis=1).astype(BF16)
    w_r = jnp.concatenate([p["w_router_group"], p["w_router_expert"],
                           jnp.zeros((d, LANE - N_GROUPS - N_EXPERTS), F32)], axis=1).astype(BF16)
    b_r = jnp.concatenate([p["b_router_group"], p["b_router_expert"],
                           jnp.zeros((LANE - N_GROUPS - N_EXPERTS,), F32)]).reshape(1, LANE)
    return dict(w_lat=w_in[:, :o[2]].astype(BF16), w_sm=w_sm, w_smt=w_sm[:, :LANE].T, w_qkv=w_in[:, o[3]:o[4]].astype(BF16),
                w_z=w_in[:, o[4]:o[5]].astype(BF16), w_out=p["w_out"].astype(BF16), w_r=w_r, b_r=b_r)


ROW_TILE = 512


def _encode(xs, p):
    wts = _prep_weights(p)
    fronts = [_encoder_front(x, p, wts) for x in xs]
    logits = jnp.concatenate([f[2] for f in fronts], axis=0)
    t = logits.shape[0]
    gate, slots, counts = _router(logits, min(ROW_TILE, t))
    pstart, block_e, n_used, n_blocks = _route_plan(counts[0, :N_EXPERTS], t * TOP_K)
    dests = [jnp.take(pstart, slots[k]) + slots[TOP_K + k] for k in range(TOP_K)]
    xb = jnp.zeros((n_blocks * MOE_BLOCK * ROW_SUB, LANE), jnp.uint32)
    bounds = np.cumsum([0] + [f[0].shape[0] for f in fronts])
    for f, t0, t1 in zip(fronts, bounds[:-1], bounds[1:]):
        xb = _dispatch([d[t0:t1] for d in dests], f[1], xb, min(ROW_TILE, t1 - t0))
    yb = _moe_blocks(block_e, n_used, xb, p["w_gate"], p["w_up"], p["w_down"])
    outs = []
    for x, f, t0, t1 in zip(xs, fronts, bounds[:-1], bounds[1:]):
        out = _combine([d[t0:t1] for d in dests], f[0], gate[t0:t1], p["norm_final"], yb, min(ROW_TILE, t1 - t0))
        outs.append(out.reshape(x.shape))
    return outs


def kernel(x_prompt, x_sample, norm_mix, w_in, g_q_lora, g_kv_lora, w_uq, w_ukv, g_attn_out, conv_w, a_log,
           dt_bias, g_gdn_out, w_out, norm_ffn, w_router_group, b_router_group, w_router_expert,
           b_router_expert, w_gate, w_up, w_down, norm_final):
    p = dict(norm_mix=norm_mix[0], w_in=w_in[0], g_q_lora=g_q_lora[0], g_kv_lora=g_kv_lora[0], w_uq=w_uq[0],
             w_ukv=w_ukv[0], g_attn_out=g_attn_out[0], conv_w=conv_w[0], a_log=a_log[0], dt_bias=dt_bias[0],
             g_gdn_out=g_gdn_out[0], w_out=w_out[0], norm_ffn=norm_ffn[0], w_router_group=w_router_group[0],
             b_router_group=b_router_group[0], w_router_expert=w_router_expert[0],
             b_router_expert=b_router_expert[0], w_gate=w_gate[0], w_up=w_up[0], w_down=w_down[0],
             norm_final=norm_final)
    y_prompt, y_sample = _encode([x_prompt, x_sample], p)
    return (y_prompt, y_sample)
```

```python
import functools
import math

import jax
import jax.numpy as jnp
import numpy as np
from jax import lax
from jax.experimental import pallas as pl
from jax.experimental.pallas import tpu as pltpu

F32 = jnp.float32
BF16 = jnp.bfloat16

D_MODEL = 2048
MLA_HEADS = 8
Q_LORA = 512
KV_LORA = 512
QK_NOPE = 128
QK_ROPE = 64
V_HEAD = 128
ROPE_THETA = 10000.0
GDN_HEADS = 8
GDN_DK = 128
GDN_DV = 128
GDN_QKV = GDN_HEADS * (2 * GDN_DK + GDN_DV)
CONV_K = 5
CHUNK = 64
D_ATTN = MLA_HEADS * V_HEAD
D_GDN = GDN_HEADS * GDN_DV
N_GROUPS = 8
EXPERTS_PER_GROUP = 8
N_EXPERTS = N_GROUPS * EXPERTS_PER_GROUP
TOP_K = 2
D_EXPERT = 512
EPS = 1e-6

LANE = 128
QK_PAD = 256
VMEM_LIMIT = 56 * 1024 * 1024


def _params(*sem, **kw):
    return pltpu.CompilerParams(dimension_semantics=sem, vmem_limit_bytes=VMEM_LIMIT, **kw)


def _resident(shape):
    return pl.BlockSpec(shape, lambda *_: (0,) * len(shape), pipeline_mode=pl.Buffered(1))


def _rms(x, g):
    return x * lax.rsqrt(jnp.mean(x * x, axis=-1, keepdims=True) + EPS) * g


def _dot(a, b):
    return jnp.dot(a, b, preferred_element_type=F32)


def _dot_nt(a, b):
    return lax.dot_general(a, b, (((1,), (1,)), ((), ())), preferred_element_type=F32)


def _pack_halves(x):
    w = x.shape[1] // 2
    lo = pltpu.bitcast(x[:, :w].astype(F32), jnp.uint32)
    hi = pltpu.bitcast(x[:, w:].astype(F32), jnp.uint32)
    return (hi & jnp.uint32(0xFFFF0000)) | (lo >> 16)


ROW_SUB = 8


def _row_tile(r):
    return pl.ds(pl.multiple_of(r * ROW_SUB, ROW_SUB), ROW_SUB)


def _store_rows(ref, u):
    n = u.shape[0]
    for s in range(ROW_SUB):
        ref[pl.ds(s, n, stride=ROW_SUB), :] = u[:, s * LANE:(s + 1) * LANE]


def _load_rows(ref):
    n = ref.shape[0] // ROW_SUB
    return jnp.concatenate([ref[pl.ds(s, n, stride=ROW_SUB), :] for s in range(ROW_SUB)], axis=1)


def _unpack_halves(u):
    lo = pltpu.bitcast(u << 16, F32)
    hi = pltpu.bitcast(u & jnp.uint32(0xFFFF0000), F32)
    return lo, hi


def _inproj_kernel(x_ref, g_ref, w_lat_ref, w_sm_ref, w_smt_ref, w_qkv_ref, w_z_ref,
                   lat_ref, sm_ref, smt_ref, qkv_ref, z_ref):
    xn = _rms(x_ref[...], g_ref[...]).astype(BF16)
    lat_ref[...] = _dot(xn, w_lat_ref[...]).astype(BF16)
    sm_ref[...] = _dot(xn, w_sm_ref[...])
    smt_ref[...] = _dot_nt(w_smt_ref[...], xn)
    qkv_ref[...] = _dot(xn, w_qkv_ref[...]).astype(BF16)
    z_ref[...] = _dot(xn, w_z_ref[...]).astype(BF16)


def _inproj(x2d, g, w_lat, w_sm, w_smt, w_qkv, w_z, tm):
    t, d = x2d.shape
    n_lat, n_sm, n_qkv, n_z = w_lat.shape[1], w_sm.shape[1], w_qkv.shape[1], w_z.shape[1]
    n_smt = w_smt.shape[0]
    row = lambda n: pl.BlockSpec((tm, n), lambda i: (i, 0))
    return pl.pallas_call(
        _inproj_kernel,
        grid=(t // tm,),
        in_specs=[row(d), _resident((1, d)), _resident(w_lat.shape), _resident(w_sm.shape),
                  _resident(w_smt.shape), _resident(w_qkv.shape), _resident(w_z.shape)],
        out_specs=[row(n_lat), row(n_sm), pl.BlockSpec((n_smt, tm), lambda i: (0, i)), row(n_qkv), row(n_z)],
        out_shape=[jax.ShapeDtypeStruct((t, n_lat), BF16), jax.ShapeDtypeStruct((t, n_sm), F32),
                   jax.ShapeDtypeStruct((n_smt, t), F32), jax.ShapeDtypeStruct((t, n_qkv), BF16),
                   jax.ShapeDtypeStruct((t, n_z), BF16)],
        compiler_params=_params("arbitrary"),
        name="inproj",
    )(x2d, g, w_lat, w_sm, w_smt, w_qkv, w_z)


def _mla_up_kernel(lat_ref, sm_ref, gq_ref, gkv_ref, wqt_ref, wk_ref, wvt_ref,
                   cos_t_ref, sin_t_ref, ck_ref, sk_ref, qt_ref, k_ref, vt_ref):
    lat = lat_ref[...].astype(F32)
    cqn = _rms(lat[:, :Q_LORA], gq_ref[...]).astype(BF16)
    ckvn = _rms(lat[:, Q_LORA:], gkv_ref[...]).astype(BF16)
    half = QK_ROPE // 2
    dqk = QK_NOPE + QK_ROPE
    qt = _dot_nt(wqt_ref[...], cqn)
    cos_t, sin_t = cos_t_ref[...], sin_t_ref[...]
    zero = jnp.zeros((QK_PAD - dqk, qt.shape[1]), BF16)
    for h in range(MLA_HEADS):
        lo = h * dqk
        x1 = qt[lo + QK_NOPE:lo + QK_NOPE + half, :]
        x2 = qt[lo + QK_NOPE + half:lo + dqk, :]
        qt_ref[0, h, 0, 0:QK_NOPE, :] = qt[lo:lo + QK_NOPE, :].astype(BF16)
        qt_ref[0, h, 0, QK_NOPE:QK_NOPE + half, :] = (x1 * cos_t - x2 * sin_t).astype(BF16)
        qt_ref[0, h, 0, QK_NOPE + half:dqk, :] = (x2 * cos_t + x1 * sin_t).astype(BF16)
        qt_ref[0, h, 0, dqk:, :] = zero
    kn = _dot(ckvn, wk_ref[...])
    sm = sm_ref[...]
    pe = (sm[:, :LANE] * ck_ref[...] + sm[:, LANE:] * sk_ref[...]).astype(BF16)
    for h in range(MLA_HEADS):
        k_ref[:, h * QK_PAD:h * QK_PAD + QK_NOPE] = kn[:, h * QK_NOPE:(h + 1) * QK_NOPE].astype(BF16)
        k_ref[:, h * QK_PAD + QK_NOPE:(h + 1) * QK_PAD] = pe
    vt = _dot_nt(wvt_ref[...], ckvn)
    for h in range(MLA_HEADS):
        vt_ref[0, h, 0, 0:V_HEAD, :] = vt[h * V_HEAD:(h + 1) * V_HEAD, :].astype(BF16)
        vt_ref[0, h, 0, V_HEAD:, :] = jnp.ones((V_ROWS - V_HEAD, vt.shape[1]), BF16)


def _mla_up(lat, sm, gq, gkv, wqt, wk, wvt, cos_t, sin_t, ck, sk, b, s, ts):
    t = b * s
    nj = s // ts
    hk = MLA_HEADS * QK_PAD
    return pl.pallas_call(
        _mla_up_kernel,
        grid=(b, nj),
        in_specs=[pl.BlockSpec((ts, lat.shape[1]), lambda bi, j: (bi * nj + j, 0)),
                  pl.BlockSpec((ts, sm.shape[1]), lambda bi, j: (bi * nj + j, 0)),
                  _resident(gq.shape), _resident(gkv.shape), _resident(wqt.shape), _resident(wk.shape),
                  _resident(wvt.shape),
                  pl.BlockSpec((QK_ROPE // 2, ts), lambda bi, j: (0, j)),
                  pl.BlockSpec((QK_ROPE // 2, ts), lambda bi, j: (0, j)),
                  pl.BlockSpec((ts, LANE), lambda bi, j: (j, 0)),
                  pl.BlockSpec((ts, LANE), lambda bi, j: (j, 0))],
        out_specs=[pl.BlockSpec((1, MLA_HEADS, 1, QK_PAD, ts), lambda bi, j: (bi, 0, j, 0, 0)),
                   pl.BlockSpec((ts, hk), lambda bi, j: (bi * nj + j, 0)),
                   pl.BlockSpec((1, MLA_HEADS, 1, V_ROWS, ts), lambda bi, j: (bi, 0, j, 0, 0))],
        out_shape=[jax.ShapeDtypeStruct((b, MLA_HEADS, nj, QK_PAD, ts), BF16),
                   jax.ShapeDtypeStruct((t, hk), BF16),
                   jax.ShapeDtypeStruct((b, MLA_HEADS, nj, V_ROWS, ts), BF16)],
        compiler_params=_params("arbitrary", "arbitrary"),
        name="mla_up",
    )(lat, sm, gq, gkv, wqt, wk, wvt, cos_t, sin_t, ck, sk)


ATTN_UNROLL = 8
ATTN_Q_TILES = 2
V_ROWS = V_HEAD + 16


def _attn_kernel(qt_ref, k_ref, vt_ref, o_ref, m_ref, acc_ref, s_buf, c_buf, *, nk, tk, unroll):
    qt = jnp.concatenate([qt_ref[0, 0, i] for i in range(qt_ref.shape[2])], axis=1)
    m_ref[...] = jnp.full(m_ref.shape, -jnp.inf, F32)
    acc_ref[...] = jnp.zeros(acc_ref.shape, F32)

    def scores(j, buf):
        s = _dot(k_ref[pl.ds(pl.multiple_of(j * tk, tk), tk), :], qt)
        s_buf[buf] = s
        c_buf[buf] = jnp.max(s, axis=0, keepdims=True)

    scores(0, 0)

    def step(j, cur, nxt):
        scores(jnp.minimum(j + 1, nk - 1), nxt)
        m_old = m_ref[...]
        m_new = jnp.maximum(m_old, c_buf[cur])
        alpha = jnp.exp2(m_old - m_new)
        p = jnp.exp2(s_buf[cur] - m_new).astype(BF16)
        acc_ref[...] = alpha * acc_ref[...] + _dot(vt_ref[0, 0, j], p)
        m_ref[...] = m_new

    def body(jj, carry):
        for u in range(unroll):
            step(unroll * jj + u, u % 2, 1 - u % 2)
        return carry

    lax.fori_loop(0, nk // unroll, body, 0)
    acc = acc_ref[...]
    o_ref[...] = (acc[:V_HEAD] / acc[V_HEAD:V_HEAD + 1]).T.astype(o_ref.dtype)


def _attention(qt, k, vt, b, s):
    nk, tk = vt.shape[2], vt.shape[4]
    qtiles = math.gcd(qt.shape[2], ATTN_Q_TILES)
    nq, tq = qt.shape[2] // qtiles, qt.shape[4] * qtiles
    assert nk % 2 == 0, "the score buffers alternate statically"
    unroll = math.gcd(nk, ATTN_UNROLL)
    return pl.pallas_call(
        functools.partial(_attn_kernel, nk=nk, tk=tk, unroll=unroll),
        grid=(b, MLA_HEADS, nq),
        in_specs=[pl.BlockSpec((1, 1, qtiles, QK_PAD, tq // qtiles), lambda bi, h, i: (bi, h, i, 0, 0)),
                  pl.BlockSpec((s, QK_PAD), lambda bi, h, i: (bi, h)),
                  pl.BlockSpec((1, 1, nk, V_ROWS, tk), lambda bi, h, i: (bi, h, 0, 0, 0))],
        out_specs=pl.BlockSpec((tq, V_HEAD), lambda bi, h, i: (bi * nq + i, h)),
        out_shape=jax.ShapeDtypeStruct((b * s, D_ATTN), BF16),
        scratch_shapes=[pltpu.VMEM((1, tq), F32), pltpu.VMEM((V_ROWS, tq), F32),
                        pltpu.VMEM((2, tk, tq), F32), pltpu.VMEM((2, 1, tq), F32)],
        compiler_params=_params("arbitrary", "arbitrary", "arbitrary"),
        name="attention",
    )(qt, k, vt)


def _rope_tables(s):
    inv_freq = np.float32(ROPE_THETA) ** (-np.arange(0, QK_ROPE, 2, dtype=np.float32) / np.float32(QK_ROPE))
    ang = np.arange(s, dtype=np.float32)[:, None] * inv_freq[None, :].astype(np.float32)
    return np.cos(ang.astype(np.float64)).astype(np.float32), np.sin(ang.astype(np.float64)).astype(np.float32)


def _prep_mla_weights(w_uq, w_ukv):
    half = QK_ROPE // 2
    scale = (QK_NOPE + QK_ROPE) ** -0.5 * math.log2(math.e)
    wqt = (w_uq * scale).T
    wkv = w_ukv.reshape(KV_LORA, MLA_HEADS, QK_NOPE + V_HEAD)
    wk = wkv[..., :QK_NOPE].reshape(KV_LORA, MLA_HEADS * QK_NOPE)
    wvt = wkv[..., QK_NOPE:].reshape(KV_LORA, MLA_HEADS * V_HEAD).T
    return wqt.astype(BF16), wk.astype(BF16), wvt.astype(BF16)


def _mla(lat, sm, g_q, g_kv, w_uq, w_ukv, b, s):
    ts = min(512, s)
    wqt, wk, wvt = _prep_mla_weights(w_uq, w_ukv)
    cos, sin = _rope_tables(s)
    zeros = np.zeros((s, LANE - QK_ROPE), np.float32)
    ck = np.concatenate([cos, cos, zeros], axis=-1)
    sk = np.concatenate([-sin, sin, zeros], axis=-1)
    qt, k, vt = _mla_up(lat, sm, g_q.reshape(1, -1), g_kv.reshape(1, -1), wqt, wk, wvt,
                        jnp.asarray(cos.T), jnp.asarray(sin.T), jnp.asarray(ck), jnp.asarray(sk), b, s, ts)
    return _attention(qt, k, vt, b, s)


HALO = 16
GATE_A = QK_ROPE
GATE_BT = QK_ROPE + 2 * GDN_HEADS
NG = 2 * GDN_HEADS


def _chunk_masks(n, rev):
    ri = lax.broadcasted_iota(jnp.int32, (n, n), 0)
    ci = lax.broadcasted_iota(jnp.int32, (n, n), 1)
    same = (ri // CHUNK) == (ci // CHUNK)
    if rev:
        return same, same & (ri <= ci), same & (ri < ci), ri == ci
    return same, same & (ri >= ci), same & (ri > ci), ri == ci


def _dot_exact(a, b):
    return jnp.dot(a, b, preferred_element_type=F32, precision=lax.Precision.HIGHEST)


def _dot_nt_exact(a, b):
    return lax.dot_general(a, b, (((1,), (1,)), ((), ())), preferred_element_type=F32,
                           precision=lax.Precision.HIGHEST)


def _softplus(x):
    return jnp.maximum(x, 0.0) + jnp.log(1.0 + jnp.exp(-jnp.abs(x)))


def _gdn_prep_kernel(qkv_ref, prev_ref, next_ref, sm_ref, smt_ref, cw_ref, alog_ref, dtb_ref,
                     alogt_ref, dtbt_ref, q_ref, k_ref, v_ref, kt_ref, gc_ref, beta_ref, gct_ref, glt_ref,
                     xs_ref, *, ts):
    j = pl.program_id(1)
    nj = pl.num_programs(1)
    xs_ref[0:HALO, :] = jnp.where(j > 0, prev_ref[...].astype(F32), 0.0)
    xs_ref[HALO:HALO + ts, :] = qkv_ref[...].astype(F32)
    xs_ref[HALO + ts:, :] = jnp.where(j < nj - 1, next_ref[...].astype(F32), 0.0)
    acc = None
    for tap in range(CONV_K):
        lo = HALO - CONV_K // 2 + tap
        term = xs_ref[lo:lo + ts, :] * cw_ref[tap:tap + 1, :]
        acc = term if acc is None else acc + term
    act = acc * jax.nn.sigmoid(acc)
    eye = (lax.broadcasted_iota(jnp.int32, (GDN_DK, GDN_DK), 0)
           == lax.broadcasted_iota(jnp.int32, (GDN_DK, GDN_DK), 1)).astype(BF16)
    hk = GDN_HEADS * GDN_DK
    for h in range(GDN_HEADS):
        qh = act[:, h * GDN_DK:(h + 1) * GDN_DK]
        kh = act[:, hk + h * GDN_DK:hk + (h + 1) * GDN_DK]
        qh = qh * (lax.rsqrt(jnp.sum(qh * qh, axis=-1, keepdims=True) + EPS) * GDN_DK ** -0.5)
        kh = (kh * lax.rsqrt(jnp.sum(kh * kh, axis=-1, keepdims=True) + EPS)).astype(BF16)
        q_ref[:, h * GDN_DK:(h + 1) * GDN_DK] = qh.astype(BF16)
        k_ref[:, h * GDN_DK:(h + 1) * GDN_DK] = kh
        kt_ref[h] = _dot_nt(eye, kh).astype(BF16)
    v_ref[...] = act[:, 2 * hk:].astype(BF16)

    sm = sm_ref[...]
    g = -jnp.exp(alog_ref[...]) * _softplus(sm[:, GATE_A:GATE_A + NG] + dtb_ref[...])
    beta_ref[...] = jax.nn.sigmoid(sm[:, GATE_BT:GATE_BT + NG])
    _, incl_f, _, _ = _chunk_masks(ts, False)
    _, incl_b, _, _ = _chunk_masks(ts, True)
    tri_f = incl_f.astype(F32)
    tri_b = incl_b.astype(F32)
    is_fwd = lax.broadcasted_iota(jnp.int32, (ts, NG), 1) < GDN_HEADS
    gc_ref[...] = jnp.where(is_fwd, _dot_exact(tri_f, g), _dot_exact(tri_b, g))
    smt = smt_ref[...]
    gt = -jnp.exp(alogt_ref[...]) * _softplus(smt[GATE_A:GATE_A + NG, :] + dtbt_ref[...])
    is_fwd_t = lax.broadcasted_iota(jnp.int32, (NG, ts), 0) < GDN_HEADS
    gct_ref[...] = jnp.where(is_fwd_t, _dot_nt_exact(gt, tri_f), _dot_nt_exact(gt, tri_b))
    same, _, _, _ = _chunk_masks(ts, False)
    glt_ref[...] = _dot_exact(gt, same.astype(F32))


def _gdn_prep(qkv, sm, smt, conv_w, a_log, dt_bias, b, s, ts):
    t = b * s
    nj = s // ts
    c = qkv.shape[1]
    hb = ts // HALO
    d = GDN_HEADS * GDN_DK
    tile = lambda n: pl.BlockSpec((ts, n), lambda bi, j: (bi * nj + j, 0))
    tile_t = lambda n: pl.BlockSpec((n, ts), lambda bi, j: (0, bi * nj + j))
    return pl.pallas_call(
        functools.partial(_gdn_prep_kernel, ts=ts),
        grid=(b, nj),
        in_specs=[tile(c),
                  pl.BlockSpec((HALO, c), lambda bi, j: (jnp.maximum((bi * nj + j) * hb - 1, 0), 0)),
                  pl.BlockSpec((HALO, c), lambda bi, j: (jnp.minimum((bi * nj + j + 1) * hb, t // HALO - 1), 0)),
                  tile(sm.shape[1]), tile_t(smt.shape[0]),
                  _resident(conv_w.shape), _resident((1, NG)), _resident((1, NG)),
                  _resident((NG, 1)), _resident((NG, 1))],
        out_specs=[tile(d), tile(d), tile(d),
                   pl.BlockSpec((GDN_HEADS, GDN_DK, ts), lambda bi, j: (0, 0, bi * nj + j)),
                   tile(NG), tile(NG), tile_t(NG), tile_t(NG)],
        out_shape=[jax.ShapeDtypeStruct((t, d), BF16)] * 3
        + [jax.ShapeDtypeStruct((GDN_HEADS, GDN_DK, t), BF16),
           jax.ShapeDtypeStruct((t, NG), F32), jax.ShapeDtypeStruct((t, NG), F32),
           jax.ShapeDtypeStruct((NG, t), F32), jax.ShapeDtypeStruct((NG, t), F32)],
        scratch_shapes=[pltpu.VMEM((ts + 2 * HALO, c), F32)],
        compiler_params=_params("arbitrary", "arbitrary"),
        name="gdn_prep",
    )(qkv, qkv, qkv, sm, smt, conv_w, a_log.reshape(1, NG), dt_bias.reshape(1, NG),
      a_log.reshape(NG, 1), dt_bias.reshape(NG, 1))


STEP = 2 * CHUNK


def _gdn_scan_kernel(*refs, revs):
    nd = len(revs)
    ins = [refs[8 * d:8 * d + 8] for d in range(nd)]
    outs = refs[8 * nd:9 * nd]
    s_ref = refs[9 * nd]

    @pl.when(pl.program_id(1) == 0)
    def _():
        s_ref[...] = jnp.zeros(s_ref.shape, F32)

    masks = [_chunk_masks(STEP, rev) for rev in revs]
    lane = lax.broadcasted_iota(jnp.int32, (1, STEP), 1)
    zeros = jnp.zeros((CHUNK, GDN_DV), F32)
    chains = [(d, h) for d in range(nd) for h in range(GDN_HEADS)]
    ids = range(len(chains))
    sl = [slice(h * GDN_DK, (h + 1) * GDN_DK) for _, h in chains]
    col = [(GDN_HEADS if revs[d] else 0) + h for d, h in chains]
    incl = [masks[d][1] for d, _ in chains]
    strict = [masks[d][2] for d, _ in chains]
    eye = masks[0][3].astype(F32)
    q_ref, k_ref, v_ref, kt_ref, gc_ref, beta_ref, gct_ref, glt_ref = (
        [ins[d][i] for d, _ in chains] for i in range(8))
    q = [q_ref[c][:, sl[c]] for c in ids]
    kt = [kt_ref[c][chains[c][1]] for c in ids]
    gcol = [gc_ref[c][:, col[c]:col[c] + 1] for c in ids]
    bcol = [beta_ref[c][:, col[c]:col[c] + 1] for c in ids]
    grow = [gct_ref[c][col[c]:col[c] + 1, :] for c in ids]
    glrow = [glt_ref[c][col[c]:col[c] + 1, :] for c in ids]
    decay = [jnp.exp(jnp.where(incl[c], gcol[c] - grow[c], -jnp.inf)) for c in ids]
    egc = [jnp.exp(gcol[c]) for c in ids]
    kb = [k_ref[c][:, sl[c]].astype(F32) * bcol[c] for c in ids]
    aq = [_dot(jnp.concatenate([kb[c].astype(BF16), q[c]], axis=0), kt[c]) for c in ids]
    power = [-jnp.where(strict[c], aq[c][:STEP] * decay[c], 0.0) for c in ids]
    intra = [(aq[c][STEP:] * decay[c]).astype(BF16) for c in ids]
    inv = [eye + power[c] for c in ids]
    for _ in range(int(math.log2(CHUNK)) - 1):
        pb = [power[c].astype(BF16) for c in ids]
        power = [_dot(pb[c], pb[c]) for c in ids]
        inv = [inv[c] + _dot(inv[c].astype(BF16), power[c].astype(BF16)) for c in ids]
    rhs = [jnp.concatenate([(v_ref[c][:, sl[c]].astype(F32) * bcol[c]).astype(BF16),
                            (kb[c] * egc[c]).astype(BF16)], axis=1) for c in ids]
    uw = [_dot(inv[c].astype(BF16), rhs[c]) for c in ids]
    qd = [(q[c].astype(F32) * egc[c]).astype(BF16) for c in ids]
    kdt = [(kt[c].astype(F32) * jnp.exp(glrow[c] - grow[c])).astype(BF16) for c in ids]
    state = [s_ref[c] for c in ids]
    for half in range(2):
        ch = [1 - half if revs[d] else half for d, _ in chains]
        r = [slice(ch[c] * CHUNK, (ch[c] + 1) * CHUNK) for c in ids]
        ws = [_dot(jnp.concatenate([uw[c][r[c], GDN_DV:].astype(BF16), qd[c][r[c]]], axis=0),
                   state[c].astype(BF16)) for c in ids]
        v_new = [uw[c][r[c], :GDN_DV] - ws[c][:CHUNK] for c in ids]
        vpad = [jnp.concatenate([v_new[c], zeros] if ch[c] == 0 else [zeros, v_new[c]], axis=0).astype(BF16)
                for c in ids]
        res = [_dot(jnp.concatenate([intra[c][r[c]], kdt[c]], axis=0), vpad[c]) for c in ids]
        for c in ids:
            outs[chains[c][0]][r[c], sl[c]] = (ws[c][CHUNK:] + res[c][:CHUNK]).astype(outs[0].dtype)
        gl = [jnp.sum(jnp.where(lane == ch[c] * CHUNK, glrow[c], 0.0), axis=-1, keepdims=True) for c in ids]
        state = [state[c] * jnp.exp(gl[c]) + res[c][CHUNK:] for c in ids]
    for c in ids:
        s_ref[c] = state[c]


def _gdn_scan(q, k, v, kt, gc, beta, gct, glt, b, s, revs):
    t = b * s
    n = s // STEP
    d = GDN_HEADS * GDN_DK
    in_specs, operands, out_specs = [], [], []
    for rev in revs:
        blk = (lambda bi, i: bi * n + (n - 1 - i)) if rev else (lambda bi, i: bi * n + i)
        tile = lambda w, blk=blk: pl.BlockSpec((STEP, w), lambda bi, i: (blk(bi, i), 0))
        tile_t = lambda w, blk=blk: pl.BlockSpec((w, STEP), lambda bi, i: (0, blk(bi, i)))
        in_specs += [tile(d), tile(d), tile(d),
                     pl.BlockSpec((GDN_HEADS, GDN_DK, STEP), lambda bi, i, blk=blk: (0, 0, blk(bi, i))),
                     tile(NG), tile(NG), tile_t(NG), tile_t(NG)]
        operands += [q, k, v, kt, gc, beta, gct, glt]
        out_specs.append(tile(d))
    return pl.pallas_call(
        functools.partial(_gdn_scan_kernel, revs=tuple(revs)),
        grid=(b, n),
        in_specs=in_specs,
        out_specs=out_specs,
        out_shape=[jax.ShapeDtypeStruct((t, d), BF16)] * len(revs),
        scratch_shapes=[pltpu.VMEM((len(revs) * GDN_HEADS, GDN_DK, GDN_DV), F32)],
        compiler_params=_params("arbitrary", "arbitrary"),
        name="gdn_scan",
    )(*operands)


def _gdn(qkv, sm, smt, conv_w, a_log, dt_bias, b, s):
    ts = min(256, s)
    q, k, v, kt, gc, beta, gct, glt = _gdn_prep(qkv, sm, smt, conv_w, a_log, dt_bias, b, s, ts)
    o_f, o_b = _gdn_scan(q, k, v, kt, gc, beta, gct, glt, b, s, (False, True))
    return o_f, o_b


def _outproj_kernel(x_ref, att_ref, of_ref, ob_ref, z_ref, ga_ref, gg_ref, wo_ref, nf_ref, wr_ref, br_ref,
                    y_ref, yn_ref, lg_ref):
    att = _rms(att_ref[...].astype(F32), ga_ref[...]).astype(BF16)
    o = of_ref[...].astype(F32) + ob_ref[...].astype(F32)
    z = z_ref[...].astype(F32)
    gg = gg_ref[...]
    lin = []
    for h in range(GDN_HEADS):
        sl = slice(h * GDN_DV, (h + 1) * GDN_DV)
        zh = z[:, sl]
        lin.append((_rms(o[:, sl], gg) * (zh * jax.nn.sigmoid(zh))).astype(BF16))
    lin = jnp.concatenate(lin, axis=-1)
    y = x_ref[...] + _dot(att, wo_ref[:D_ATTN, :]) + _dot(lin, wo_ref[D_ATTN:, :])
    y_ref[...] = y
    yn = _rms(y, nf_ref[...]).astype(BF16)
    _store_rows(yn_ref, _pack_halves(yn))
    lg_ref[...] = _dot(yn, wr_ref[...]) + br_ref[...]


def _outproj(x2d, att, o_f, o_b, z, g_attn, g_gdn, w_out, norm_ffn, w_r, b_r, tm):
    t, d = x2d.shape
    row = lambda n: pl.BlockSpec((tm, n), lambda i: (i, 0))
    return pl.pallas_call(
        _outproj_kernel,
        grid=(t // tm,),
        in_specs=[row(d), row(D_ATTN), row(D_GDN), row(D_GDN), row(D_GDN), _resident((1, D_ATTN)),
                  _resident((1, GDN_DV)), _resident(w_out.shape), _resident((1, d)), _resident(w_r.shape),
                  _resident((1, LANE))],
        out_specs=[row(d), pl.BlockSpec((tm * ROW_SUB, LANE), lambda i: (i, 0)), row(LANE)],
        out_shape=[jax.ShapeDtypeStruct((t, d), F32), jax.ShapeDtypeStruct((t * ROW_SUB, LANE), jnp.uint32),
                   jax.ShapeDtypeStruct((t, LANE), F32)],
        compiler_params=_params("arbitrary"),
        name="outproj",
    )(x2d, att, o_f, o_b, z, g_attn.reshape(1, -1), g_gdn.reshape(1, -1), w_out, norm_ffn.reshape(1, -1),
      w_r, b_r)


SLOT_ROWS = 8


def _router_kernel(lg_ref, gate_ref, slot_ref, cnt_ref, carry_ref):
    @pl.when(pl.program_id(0) == 0)
    def _():
        carry_ref[...] = jnp.zeros(carry_ref.shape, F32)

    lg = lg_ref[...]
    lane = lax.broadcasted_iota(jnp.int32, lg.shape, 1)
    neg = -jnp.inf

    def first_max(x):
        mx = jnp.max(x, axis=-1, keepdims=True)
        return mx, jnp.min(jnp.where(x == mx, lane, LANE), axis=-1, keepdims=True)

    gl = jnp.where(lane < N_GROUPS, lg, neg)
    gmax, grp = first_max(gl)
    grp_p = 1.0 / jnp.sum(jnp.exp(gl - gmax), axis=-1, keepdims=True)
    lo = N_GROUPS + grp * EXPERTS_PER_GROUP
    el = jnp.where((lane >= lo) & (lane < lo + EXPERTS_PER_GROUP), lg, neg)
    l1, i1 = first_max(el)
    l2, i2 = first_max(jnp.where(lane == i1, neg, el))
    e = jnp.exp(l2 - l1)
    g1 = grp_p / (1.0 + e)
    e1, e2 = i1 - N_GROUPS, i2 - N_GROUPS
    gate_ref[...] = jnp.where(lane == 0, g1, jnp.where(lane == 1, g1 * e, 0.0))
    tm = lg.shape[0]
    oh1, oh2 = lane == e1, lane == e2
    before = (lax.broadcasted_iota(jnp.int32, (tm, tm), 1)
              < lax.broadcasted_iota(jnp.int32, (tm, tm), 0)).astype(BF16)
    r1 = _dot(before, oh1.astype(BF16))
    r2 = _dot(before, oh2.astype(BF16))
    c1 = jnp.sum(oh1.astype(F32), axis=0, keepdims=True)
    c2 = jnp.sum(oh2.astype(F32), axis=0, keepdims=True)
    carry = carry_ref[...]
    rank1 = jnp.sum(jnp.where(oh1, r1 + carry, 0.0), axis=-1, keepdims=True)
    rank2 = jnp.sum(jnp.where(oh2, r2 + (carry + c1), 0.0), axis=-1, keepdims=True)
    info = jnp.where(lane == 0, e1.astype(F32), jnp.where(lane == 1, e2.astype(F32),
                     jnp.where(lane == 2, rank1, jnp.where(lane == 3, rank2, 0.0))))
    slot_ref[...] = info.T[:SLOT_ROWS, :].astype(jnp.int32)
    carry_ref[...] = carry + c1 + c2
    cnt_ref[...] = (carry + c1 + c2).astype(jnp.int32)


def _router(logits, tm):
    t = logits.shape[0]
    row = pl.BlockSpec((tm, LANE), lambda i: (i, 0))
    return pl.pallas_call(
        _router_kernel,
        grid=(t // tm,),
        in_specs=[row],
        out_specs=[row, pl.BlockSpec((SLOT_ROWS, tm), lambda i: (0, i)), pl.BlockSpec((1, LANE), lambda i: (0, 0))],
        out_shape=[jax.ShapeDtypeStruct((t, LANE), F32), jax.ShapeDtypeStruct((SLOT_ROWS, t), jnp.int32),
                   jax.ShapeDtypeStruct((1, LANE), jnp.int32)],
        scratch_shapes=[pltpu.VMEM((1, LANE), F32)],
        compiler_params=_params("arbitrary"),
        name="router",
    )(logits)


ROW_UNROLL = 8


def _start_rows(tm, copy):
    def start(t, c):
        for k in range(TOP_K):
            copy(t, k).start()
        return c

    lax.fori_loop(0, tm, start, 0, unroll=ROW_UNROLL)


def _wait_rows(tm, copy):
    def wait(t, c):
        for k in range(TOP_K):
            copy(t, k).wait()
        return c

    lax.fori_loop(0, tm, wait, 0, unroll=ROW_UNROLL)


def _row_copies(tm, copy):
    _start_rows(tm, copy)
    _wait_rows(tm, copy)


def _dispatch_kernel(*refs):
    dest_refs, yn_ref, (xb_ref, sem) = refs[:TOP_K], refs[TOP_K], refs[-2:]
    _row_copies(yn_ref.shape[0] // ROW_SUB, lambda t, k: pltpu.make_async_copy(
        yn_ref.at[_row_tile(t)], xb_ref.at[_row_tile(dest_refs[k][t])], sem))


def _dest_specs(tm):
    return [pl.BlockSpec((tm,), lambda i: (i,), memory_space=pltpu.SMEM) for _ in range(TOP_K)]


def _dispatch(dests, yn, xb, tm, n_rows):
    t = yn.shape[0] // ROW_SUB
    prev = [] if xb is None else [xb]
    return pl.pallas_call(
        _dispatch_kernel,
        grid=(t // tm,),
        in_specs=_dest_specs(tm) + [pl.BlockSpec((tm * ROW_SUB, LANE), lambda i: (i, 0))]
        + [pl.BlockSpec(memory_space=pl.ANY) for _ in prev],
        out_specs=pl.BlockSpec(memory_space=pl.ANY),
        out_shape=jax.ShapeDtypeStruct((n_rows * ROW_SUB, LANE), jnp.uint32),
        scratch_shapes=[pltpu.SemaphoreType.DMA],
        input_output_aliases={TOP_K + 1: 0} if prev else {},
        compiler_params=_params("arbitrary", has_side_effects=True),
        name="moe_dispatch",
    )(*dests, yn, *prev)


MOE_BLOCK = 512


def _moe_kernel(be_ref, nu_ref, nv_ref, xb_ref, wg_ref, wu_ref, wd_ref, yb_ref, wg_bf, wu_bf, wd_bf):
    i = pl.program_id(0)

    @pl.when(i < nu_ref[0])
    def _():
        @pl.when((i == 0) | (be_ref[i] != be_ref[jnp.maximum(i - 1, 0)]))
        def _():
            wg_bf[...] = wg_ref[0].astype(BF16)
            wu_bf[...] = wu_ref[0].astype(BF16)
            wd_bf[...] = wd_ref[0].astype(BF16)

        x = _load_rows(xb_ref)
        valid = lax.broadcasted_iota(jnp.int32, (x.shape[0], 1), 0) < nv_ref[i]
        lo, hi = _unpack_halves(jnp.where(valid, x, jnp.uint32(0)))
        lo, hi = lo.astype(BF16), hi.astype(BF16)
        w = lo.shape[1]
        a = _dot(lo, wg_bf[:w, :]) + _dot(hi, wg_bf[w:, :])
        u = _dot(lo, wu_bf[:w, :]) + _dot(hi, wu_bf[w:, :])
        hdn = (a * jax.nn.sigmoid(a) * u).astype(BF16)
        _store_rows(yb_ref, _pack_halves(_dot(hdn, wd_bf[...]).astype(BF16)))


def _moe_blocks(block_e, n_used, n_valid, xb, w_gate, w_up, w_down):
    p = xb.shape[0] // ROW_SUB
    d = w_gate.shape[1]
    nb = p // MOE_BLOCK
    clamp = lambda i, nu: jnp.minimum(i, nu[0] - 1)
    rows = pl.BlockSpec((MOE_BLOCK * ROW_SUB, LANE), lambda i, be, nu, nv: (clamp(i, nu), 0))
    grid_spec = pltpu.PrefetchScalarGridSpec(
        num_scalar_prefetch=3,
        grid=(nb,),
        in_specs=[rows,
                  pl.BlockSpec((1, d, D_EXPERT), lambda i, be, nu, nv: (be[clamp(i, nu)], 0, 0)),
                  pl.BlockSpec((1, d, D_EXPERT), lambda i, be, nu, nv: (be[clamp(i, nu)], 0, 0)),
                  pl.BlockSpec((1, D_EXPERT, d), lambda i, be, nu, nv: (be[clamp(i, nu)], 0, 0))],
        out_specs=rows,
        scratch_shapes=[pltpu.VMEM((d, D_EXPERT), BF16), pltpu.VMEM((d, D_EXPERT), BF16),
                        pltpu.VMEM((D_EXPERT, d), BF16)],
    )
    return pl.pallas_call(
        _moe_kernel,
        grid_spec=grid_spec,
        out_shape=jax.ShapeDtypeStruct(xb.shape, jnp.uint32),
        compiler_params=_params("arbitrary"),
        name="moe_experts",
    )(block_e, n_used, n_valid, xb, w_gate, w_up, w_down)


def _combine_kernel(*refs):
    dest_refs, next_refs = refs[:TOP_K], refs[TOP_K:2 * TOP_K]
    y_ref, gate_ref, g_ref, yb_ref, o_ref, buf, sems = refs[2 * TOP_K:]
    i, n, tm = pl.program_id(0), pl.num_programs(0), y_ref.shape[0]
    slot = i % 2

    def gather(rows, slot):
        return lambda t, k: pltpu.make_async_copy(
            yb_ref.at[_row_tile(rows[k][t])], buf.at[slot, k, _row_tile(t)], sems.at[slot])

    @pl.when(i == 0)
    def _():
        _start_rows(tm, gather(dest_refs, 0))

    @pl.when(i + 1 < n)
    def _():
        _start_rows(tm, gather(next_refs, 1 - slot))

    _wait_rows(tm, gather(dest_refs, slot))
    gate = gate_ref[...]
    acc_lo = acc_hi = None
    for k in range(TOP_K):
        lo, hi = _unpack_halves(_load_rows(buf.at[slot, k]))
        gk = gate[:, k:k + 1]
        acc_lo = gk * lo if acc_lo is None else acc_lo + gk * lo
        acc_hi = gk * hi if acc_hi is None else acc_hi + gk * hi
    y = y_ref[...] + jnp.concatenate([acc_lo, acc_hi], axis=-1)
    o_ref[...] = _rms(y, g_ref[...])


def _combine(dests, y, gate, g, yb, tm):
    t, d = y.shape
    n = t // tm
    next_specs = [pl.BlockSpec((tm,), lambda i: (jnp.minimum(i + 1, n - 1),), memory_space=pltpu.SMEM)
                  for _ in range(TOP_K)]
    return pl.pallas_call(
        _combine_kernel,
        grid=(n,),
        in_specs=_dest_specs(tm) + next_specs + [pl.BlockSpec((tm, d), lambda i: (i, 0)),
                                    pl.BlockSpec((tm, LANE), lambda i: (i, 0)),
                                    _resident((1, d)), pl.BlockSpec(memory_space=pl.ANY)],
        out_specs=pl.BlockSpec((tm, d), lambda i: (i, 0)),
        out_shape=jax.ShapeDtypeStruct((t, d), F32),
        scratch_shapes=[pltpu.VMEM((2, TOP_K, tm * ROW_SUB, LANE), jnp.uint32), pltpu.SemaphoreType.DMA((2,))],
        compiler_params=_params("arbitrary"),
        name="moe_combine",
    )(*dests, *dests, y, gate, g.reshape(1, -1), yb)


def _route_plan(counts, n_slots):
    padded = (counts + MOE_BLOCK - 1) // MOE_BLOCK * MOE_BLOCK
    pend = jnp.cumsum(padded)
    pstart = pend - padded
    n_blocks = -(-n_slots // MOE_BLOCK) + N_EXPERTS
    first_row = jnp.arange(n_blocks, dtype=jnp.int32) * MOE_BLOCK
    block_e = jnp.minimum(jnp.sum(pend[None, :] <= first_row[:, None], axis=1), N_EXPERTS - 1).astype(jnp.int32)
    n_used = (pend[-1] // MOE_BLOCK).astype(jnp.int32).reshape(1)
    n_valid = jnp.clip(jnp.take(pstart + counts, block_e) - first_row, 0, MOE_BLOCK).astype(jnp.int32)
    return pstart.astype(jnp.int32), block_e, n_used, n_valid, n_blocks


def _encoder_front(x, p, wts):
    b, s, d = x.shape
    x2d = x.reshape(b * s, d)
    tm = min(512, b * s)
    lat, sm, smt, qkv, z = _inproj(x2d, p["norm_mix"].reshape(1, -1), wts["w_lat"], wts["w_sm"], wts["w_smt"],
                                   wts["w_qkv"], wts["w_z"], tm)
    att = _mla(lat, sm, p["g_q_lora"], p["g_kv_lora"], p["w_uq"], p["w_ukv"], b, s)
    o_f, o_b = _gdn(qkv, sm, smt, p["conv_w"], p["a_log"], p["dt_bias"], b, s)
    return _outproj(x2d, att, o_f, o_b, z, p["g_attn_out"], p["g_gdn_out"], wts["w_out"], p["norm_ffn"],
                    wts["w_r"], wts["b_r"], tm)


def _prep_weights(p):
    w_in = p["w_in"]
    o = np.cumsum([0, Q_LORA, KV_LORA, QK_ROPE, GDN_QKV, D_GDN, NG, NG])
    d = w_in.shape[0]
    half = QK_ROPE // 2
    w_sm = jnp.concatenate([w_in[:, o[2]:o[3]], w_in[:, o[5]:o[7]], jnp.zeros((d, LANE - QK_ROPE - 2 * NG), F32),
                            w_in[:, o[2] + half:o[3]], w_in[:, o[2]:o[2] + half],
                            jnp.zeros((d, LANE - QK_ROPE), F32)], axis=1).astype(BF16)
    w_r = jnp.concatenate([p["w_router_group"], p["w_router_expert"],
                           jnp.zeros((d, LANE - N_GROUPS - N_EXPERTS), F32)], axis=1).astype(BF16)
    b_r = jnp.concatenate([p["b_router_group"], p["b_router_expert"],
                           jnp.zeros((LANE - N_GROUPS - N_EXPERTS,), F32)]).reshape(1, LANE)
    return dict(w_lat=w_in[:, :o[2]].astype(BF16), w_sm=w_sm, w_smt=w_sm[:, :LANE].T, w_qkv=w_in[:, o[3]:o[4]].astype(BF16),
                w_z=w_in[:, o[4]:o[5]].astype(BF16), w_out=p["w_out"].astype(BF16), w_r=w_r, b_r=b_r)


ROW_TILE = 512


def _encode(xs, p):
    wts = _prep_weights(p)
    fronts = [_encoder_front(x, p, wts) for x in xs]
    logits = jnp.concatenate([f[2] for f in fronts], axis=0)
    t = logits.shape[0]
    gate, slots, counts = _router(logits, min(ROW_TILE, t))
    pstart, block_e, n_used, n_valid, n_blocks = _route_plan(counts[0, :N_EXPERTS], t * TOP_K)
    dests = [jnp.take(pstart, slots[k]) + slots[TOP_K + k] for k in range(TOP_K)]
    xb = None
    bounds = np.cumsum([0] + [f[0].shape[0] for f in fronts])
    for f, t0, t1 in zip(fronts, bounds[:-1], bounds[1:]):
        xb = _dispatch([d[t0:t1] for d in dests], f[1], xb, min(ROW_TILE, t1 - t0), n_blocks * MOE_BLOCK)
    yb = _moe_blocks(block_e, n_used, n_valid, xb, p["w_gate"], p["w_up"], p["w_down"])
    outs = []
    for x, f, t0, t1 in zip(xs, fronts, bounds[:-1], bounds[1:]):
        out = _combine([d[t0:t1] for d in dests], f[0], gate[t0:t1], p["norm_final"], yb, min(ROW_TILE, t1 - t0))
        outs.append(out.reshape(x.shape))
    return outs


def kernel(x_prompt, x_sample, norm_mix, w_in, g_q_lora, g_kv_lora, w_uq, w_ukv, g_attn_out, conv_w, a_log,
           dt_bias, g_gdn_out, w_out, norm_ffn, w_router_group, b_router_group, w_router_expert,
           b_router_expert, w_gate, w_up, w_down, norm_final):
    p = dict(norm_mix=norm_mix[0], w_in=w_in[0], g_q_lora=g_q_lora[0], g_kv_lora=g_kv_lora[0], w_uq=w_uq[0],
             w_ukv=w_ukv[0], g_attn_out=g_attn_out[0], conv_w=conv_w[0], a_log=a_log[0], dt_bias=dt_bias[0],
             g_gdn_out=g_gdn_out[0], w_out=w_out[0], norm_ffn=norm_ffn[0], w_router_group=w_router_group[0],
             b_router_group=b_router_group[0], w_router_expert=w_router_expert[0],
             b_router_expert=b_router_expert[0], w_gate=w_gate[0], w_up=w_up[0], w_down=w_down[0],
             norm_final=norm_final)
    y_prompt, y_sample = _encode([x_prompt, x_sample], p)
    return (y_prompt, y_sample)
```

```python
import functools
import math

import jax
import jax.numpy as jnp
import numpy as np
from jax import lax
from jax.experimental import pallas as pl
from jax.experimental.pallas import tpu as pltpu

F32 = jnp.float32
BF16 = jnp.bfloat16

D_MODEL = 2048
MLA_HEADS = 8
Q_LORA = 512
KV_LORA = 512
QK_NOPE = 128
QK_ROPE = 64
V_HEAD = 128
ROPE_THETA = 10000.0
GDN_HEADS = 8
GDN_DK = 128
GDN_DV = 128
GDN_QKV = GDN_HEADS * (2 * GDN_DK + GDN_DV)
CONV_K = 5
CHUNK = 64
D_ATTN = MLA_HEADS * V_HEAD
D_GDN = GDN_HEADS * GDN_DV
N_GROUPS = 8
EXPERTS_PER_GROUP = 8
N_EXPERTS = N_GROUPS * EXPERTS_PER_GROUP
TOP_K = 2
D_EXPERT = 512
EPS = 1e-6

LANE = 128
QK_PAD = 256
VMEM_LIMIT = 56 * 1024 * 1024


def _params(*sem, **kw):
    return pltpu.CompilerParams(dimension_semantics=sem, vmem_limit_bytes=VMEM_LIMIT, **kw)


def _resident(shape):
    return pl.BlockSpec(shape, lambda *_: (0,) * len(shape), pipeline_mode=pl.Buffered(1))


def _rms(x, g):
    return x * lax.rsqrt(jnp.mean(x * x, axis=-1, keepdims=True) + EPS) * g


def _dot(a, b):
    return jnp.dot(a, b, preferred_element_type=F32)


def _dot_nt(a, b):
    return lax.dot_general(a, b, (((1,), (1,)), ((), ())), preferred_element_type=F32)


def _pack_halves(x):
    w = x.shape[1] // 2
    lo = pltpu.bitcast(x[:, :w].astype(F32), jnp.uint32)
    hi = pltpu.bitcast(x[:, w:].astype(F32), jnp.uint32)
    return (hi & jnp.uint32(0xFFFF0000)) | (lo >> 16)


ROW_SUB = 8


def _row_tile(r):
    return pl.ds(pl.multiple_of(r * ROW_SUB, ROW_SUB), ROW_SUB)


def _store_rows(ref, u):
    n = u.shape[0]
    for s in range(ROW_SUB):
        ref[pl.ds(s, n, stride=ROW_SUB), :] = u[:, s * LANE:(s + 1) * LANE]


def _load_rows(ref):
    n = ref.shape[0] // ROW_SUB
    return jnp.concatenate([ref[pl.ds(s, n, stride=ROW_SUB), :] for s in range(ROW_SUB)], axis=1)


def _unpack_halves(u):
    lo = pltpu.bitcast(u << 16, F32)
    hi = pltpu.bitcast(u & jnp.uint32(0xFFFF0000), F32)
    return lo, hi


def _inproj_kernel(x_ref, g_ref, w_lat_ref, w_sm_ref, w_smt_ref, w_qkv_ref, w_z_ref,
                   lat_ref, sm_ref, smt_ref, qkv_ref, z_ref):
    xn = _rms(x_ref[...], g_ref[...]).astype(BF16)
    lat_ref[...] = _dot(xn, w_lat_ref[...]).astype(BF16)
    sm_ref[...] = _dot(xn, w_sm_ref[...])
    smt_ref[...] = _dot_nt(w_smt_ref[...], xn)
    qkv_ref[...] = _dot(xn, w_qkv_ref[...]).astype(BF16)
    z_ref[...] = _dot(xn, w_z_ref[...]).astype(BF16)


def _inproj(x2d, g, w_lat, w_sm, w_smt, w_qkv, w_z, tm):
    t, d = x2d.shape
    n_lat, n_sm, n_qkv, n_z = w_lat.shape[1], w_sm.shape[1], w_qkv.shape[1], w_z.shape[1]
    n_smt = w_smt.shape[0]
    row = lambda n: pl.BlockSpec((tm, n), lambda i: (i, 0))
    return pl.pallas_call(
        _inproj_kernel,
        grid=(t // tm,),
        in_specs=[row(d), _resident((1, d)), _resident(w_lat.shape), _resident(w_sm.shape),
                  _resident(w_smt.shape), _resident(w_qkv.shape), _resident(w_z.shape)],
        out_specs=[row(n_lat), row(n_sm), pl.BlockSpec((n_smt, tm), lambda i: (0, i)), row(n_qkv), row(n_z)],
        out_shape=[jax.ShapeDtypeStruct((t, n_lat), BF16), jax.ShapeDtypeStruct((t, n_sm), F32),
                   jax.ShapeDtypeStruct((n_smt, t), F32), jax.ShapeDtypeStruct((t, n_qkv), BF16),
                   jax.ShapeDtypeStruct((t, n_z), BF16)],
        compiler_params=_params("arbitrary"),
        name="inproj",
    )(x2d, g, w_lat, w_sm, w_smt, w_qkv, w_z)


def _mla_up_kernel(lat_ref, sm_ref, gq_ref, gkv_ref, wqt_ref, wk_ref, wvt_ref,
                   cos_t_ref, sin_t_ref, ck_ref, sk_ref, qt_ref, k_ref, vt_ref):
    lat = lat_ref[...].astype(F32)
    cqn = _rms(lat[:, :Q_LORA], gq_ref[...]).astype(BF16)
    ckvn = _rms(lat[:, Q_LORA:], gkv_ref[...]).astype(BF16)
    half = QK_ROPE // 2
    dqk = QK_NOPE + QK_ROPE
    qt = _dot_nt(wqt_ref[...], cqn)
    cos_t, sin_t = cos_t_ref[...], sin_t_ref[...]
    zero = jnp.zeros((QK_PAD - dqk, qt.shape[1]), BF16)
    for h in range(MLA_HEADS):
        lo = h * dqk
        x1 = qt[lo + QK_NOPE:lo + QK_NOPE + half, :]
        x2 = qt[lo + QK_NOPE + half:lo + dqk, :]
        qt_ref[0, h, 0, 0:QK_NOPE, :] = qt[lo:lo + QK_NOPE, :].astype(BF16)
        qt_ref[0, h, 0, QK_NOPE:QK_NOPE + half, :] = (x1 * cos_t - x2 * sin_t).astype(BF16)
        qt_ref[0, h, 0, QK_NOPE + half:dqk, :] = (x2 * cos_t + x1 * sin_t).astype(BF16)
        qt_ref[0, h, 0, dqk:, :] = zero
    kn = _dot(ckvn, wk_ref[...])
    sm = sm_ref[...]
    pe = (sm[:, :LANE] * ck_ref[...] + sm[:, LANE:] * sk_ref[...]).astype(BF16)
    for h in range(MLA_HEADS):
        k_ref[:, h * QK_PAD:h * QK_PAD + QK_NOPE] = kn[:, h * QK_NOPE:(h + 1) * QK_NOPE].astype(BF16)
        k_ref[:, h * QK_PAD + QK_NOPE:(h + 1) * QK_PAD] = pe
    vt = _dot_nt(wvt_ref[...], ckvn)
    for h in range(MLA_HEADS):
        vt_ref[0, h, 0, 0:V_HEAD, :] = vt[h * V_HEAD:(h + 1) * V_HEAD, :].astype(BF16)
        vt_ref[0, h, 0, V_HEAD:, :] = jnp.ones((V_ROWS - V_HEAD, vt.shape[1]), BF16)


def _mla_up(lat, sm, gq, gkv, wqt, wk, wvt, cos_t, sin_t, ck, sk, b, s, ts):
    t = b * s
    nj = s // ts
    hk = MLA_HEADS * QK_PAD
    return pl.pallas_call(
        _mla_up_kernel,
        grid=(b, nj),
        in_specs=[pl.BlockSpec((ts, lat.shape[1]), lambda bi, j: (bi * nj + j, 0)),
                  pl.BlockSpec((ts, sm.shape[1]), lambda bi, j: (bi * nj + j, 0)),
                  _resident(gq.shape), _resident(gkv.shape), _resident(wqt.shape), _resident(wk.shape),
                  _resident(wvt.shape),
                  pl.BlockSpec((QK_ROPE // 2, ts), lambda bi, j: (0, j)),
                  pl.BlockSpec((QK_ROPE // 2, ts), lambda bi, j: (0, j)),
                  pl.BlockSpec((ts, LANE), lambda bi, j: (j, 0)),
                  pl.BlockSpec((ts, LANE), lambda bi, j: (j, 0))],
        out_specs=[pl.BlockSpec((1, MLA_HEADS, 1, QK_PAD, ts), lambda bi, j: (bi, 0, j, 0, 0)),
                   pl.BlockSpec((ts, hk), lambda bi, j: (bi * nj + j, 0)),
                   pl.BlockSpec((1, MLA_HEADS, 1, V_ROWS, ts), lambda bi, j: (bi, 0, j, 0, 0))],
        out_shape=[jax.ShapeDtypeStruct((b, MLA_HEADS, nj, QK_PAD, ts), BF16),
                   jax.ShapeDtypeStruct((t, hk), BF16),
                   jax.ShapeDtypeStruct((b, MLA_HEADS, nj, V_ROWS, ts), BF16)],
        compiler_params=_params("arbitrary", "arbitrary"),
        name="mla_up",
    )(lat, sm, gq, gkv, wqt, wk, wvt, cos_t, sin_t, ck, sk)


ATTN_UNROLL = 8
ATTN_Q_TILES = 2
V_ROWS = V_HEAD + 16


def _attn_kernel(qt_ref, k_ref, vt_ref, o_ref, m_ref, acc_ref, s_buf, c_buf, *, nk, tk, unroll):
    qt = jnp.concatenate([qt_ref[0, 0, i] for i in range(qt_ref.shape[2])], axis=1)
    m_ref[...] = jnp.full(m_ref.shape, -jnp.inf, F32)
    acc_ref[...] = jnp.zeros(acc_ref.shape, F32)

    def scores(j, buf):
        s = _dot(k_ref[pl.ds(pl.multiple_of(j * tk, tk), tk), :], qt)
        s_buf[buf] = s
        c_buf[buf] = jnp.max(s, axis=0, keepdims=True)

    scores(0, 0)

    def step(j, cur, nxt):
        scores(jnp.minimum(j + 1, nk - 1), nxt)
        m_old = m_ref[...]
        m_new = jnp.maximum(m_old, c_buf[cur])
        alpha = jnp.exp2(m_old - m_new)
        p = jnp.exp2(s_buf[cur] - m_new).astype(BF16)
        acc_ref[...] = alpha * acc_ref[...] + _dot(vt_ref[0, 0, j], p)
        m_ref[...] = m_new

    def body(jj, carry):
        for u in range(unroll):
            step(unroll * jj + u, u % 2, 1 - u % 2)
        return carry

    lax.fori_loop(0, nk // unroll, body, 0)
    acc = acc_ref[...]
    o_ref[...] = (acc[:V_HEAD] / acc[V_HEAD:V_HEAD + 1]).T.astype(o_ref.dtype)


def _attention(qt, k, vt, b, s):
    nk, tk = vt.shape[2], vt.shape[4]
    qtiles = math.gcd(qt.shape[2], ATTN_Q_TILES)
    nq, tq = qt.shape[2] // qtiles, qt.shape[4] * qtiles
    assert nk % 2 == 0, "the score buffers alternate statically"
    unroll = math.gcd(nk, ATTN_UNROLL)
    return pl.pallas_call(
        functools.partial(_attn_kernel, nk=nk, tk=tk, unroll=unroll),
        grid=(b, MLA_HEADS, nq),
        in_specs=[pl.BlockSpec((1, 1, qtiles, QK_PAD, tq // qtiles), lambda bi, h, i: (bi, h, i, 0, 0)),
                  pl.BlockSpec((s, QK_PAD), lambda bi, h, i: (bi, h)),
                  pl.BlockSpec((1, 1, nk, V_ROWS, tk), lambda bi, h, i: (bi, h, 0, 0, 0))],
        out_specs=pl.BlockSpec((tq, V_HEAD), lambda bi, h, i: (bi * nq + i, h)),
        out_shape=jax.ShapeDtypeStruct((b * s, D_ATTN), BF16),
        scratch_shapes=[pltpu.VMEM((1, tq), F32), pltpu.VMEM((V_ROWS, tq), F32),
                        pltpu.VMEM((2, tk, tq), F32), pltpu.VMEM((2, 1, tq), F32)],
        compiler_params=_params("arbitrary", "arbitrary", "arbitrary"),
        name="attention",
    )(qt, k, vt)


def _rope_tables(s):
    inv_freq = np.float32(ROPE_THETA) ** (-np.arange(0, QK_ROPE, 2, dtype=np.float32) / np.float32(QK_ROPE))
    ang = np.arange(s, dtype=np.float32)[:, None] * inv_freq[None, :].astype(np.float32)
    return np.cos(ang.astype(np.float64)).astype(np.float32), np.sin(ang.astype(np.float64)).astype(np.float32)


def _prep_mla_weights(w_uq, w_ukv):
    half = QK_ROPE // 2
    scale = (QK_NOPE + QK_ROPE) ** -0.5 * math.log2(math.e)
    wqt = (w_uq * scale).T
    wkv = w_ukv.reshape(KV_LORA, MLA_HEADS, QK_NOPE + V_HEAD)
    wk = wkv[..., :QK_NOPE].reshape(KV_LORA, MLA_HEADS * QK_NOPE)
    wvt = wkv[..., QK_NOPE:].reshape(KV_LORA, MLA_HEADS * V_HEAD).T
    return wqt.astype(BF16), wk.astype(BF16), wvt.astype(BF16)


def _mla(lat, sm, g_q, g_kv, w_uq, w_ukv, b, s):
    ts = min(512, s)
    wqt, wk, wvt = _prep_mla_weights(w_uq, w_ukv)
    cos, sin = _rope_tables(s)
    zeros = np.zeros((s, LANE - QK_ROPE), np.float32)
    ck = np.concatenate([cos, cos, zeros], axis=-1)
    sk = np.concatenate([-sin, sin, zeros], axis=-1)
    qt, k, vt = _mla_up(lat, sm, g_q.reshape(1, -1), g_kv.reshape(1, -1), wqt, wk, wvt,
                        jnp.asarray(cos.T), jnp.asarray(sin.T), jnp.asarray(ck), jnp.asarray(sk), b, s, ts)
    return _attention(qt, k, vt, b, s)


HALO = 16
GATE_A = QK_ROPE
GATE_BT = QK_ROPE + 2 * GDN_HEADS
NG = 2 * GDN_HEADS


def _chunk_masks(n, rev):
    ri = lax.broadcasted_iota(jnp.int32, (n, n), 0)
    ci = lax.broadcasted_iota(jnp.int32, (n, n), 1)
    same = (ri // CHUNK) == (ci // CHUNK)
    if rev:
        return same, same & (ri <= ci), same & (ri < ci), ri == ci
    return same, same & (ri >= ci), same & (ri > ci), ri == ci


def _dot_exact(a, b):
    return jnp.dot(a, b, preferred_element_type=F32, precision=lax.Precision.HIGHEST)


def _dot_nt_exact(a, b):
    return lax.dot_general(a, b, (((1,), (1,)), ((), ())), preferred_element_type=F32,
                           precision=lax.Precision.HIGHEST)


def _softplus(x):
    return jnp.maximum(x, 0.0) + jnp.log(1.0 + jnp.exp(-jnp.abs(x)))


def _gdn_prep_kernel(qkv_ref, prev_ref, next_ref, sm_ref, smt_ref, cw_ref, alog_ref, dtb_ref,
                     alogt_ref, dtbt_ref, q_ref, k_ref, v_ref, kt_ref, gc_ref, beta_ref, gct_ref, glt_ref,
                     xs_ref, *, ts):
    j = pl.program_id(1)
    nj = pl.num_programs(1)
    xs_ref[0:HALO, :] = jnp.where(j > 0, prev_ref[...].astype(F32), 0.0)
    xs_ref[HALO:HALO + ts, :] = qkv_ref[...].astype(F32)
    xs_ref[HALO + ts:, :] = jnp.where(j < nj - 1, next_ref[...].astype(F32), 0.0)
    acc = None
    for tap in range(CONV_K):
        lo = HALO - CONV_K // 2 + tap
        term = xs_ref[lo:lo + ts, :] * cw_ref[tap:tap + 1, :]
        acc = term if acc is None else acc + term
    act = acc * jax.nn.sigmoid(acc)
    eye = (lax.broadcasted_iota(jnp.int32, (GDN_DK, GDN_DK), 0)
           == lax.broadcasted_iota(jnp.int32, (GDN_DK, GDN_DK), 1)).astype(BF16)
    hk = GDN_HEADS * GDN_DK
    for h in range(GDN_HEADS):
        qh = act[:, h * GDN_DK:(h + 1) * GDN_DK]
        kh = act[:, hk + h * GDN_DK:hk + (h + 1) * GDN_DK]
        qh = qh * (lax.rsqrt(jnp.sum(qh * qh, axis=-1, keepdims=True) + EPS) * GDN_DK ** -0.5)
        kh = (kh * lax.rsqrt(jnp.sum(kh * kh, axis=-1, keepdims=True) + EPS)).astype(BF16)
        q_ref[:, h * GDN_DK:(h + 1) * GDN_DK] = qh.astype(BF16)
        k_ref[:, h * GDN_DK:(h + 1) * GDN_DK] = kh
        kt_ref[h] = _dot_nt(eye, kh).astype(BF16)
    v_ref[...] = act[:, 2 * hk:].astype(BF16)

    sm = sm_ref[...]
    g = -jnp.exp(alog_ref[...]) * _softplus(sm[:, GATE_A:GATE_A + NG] + dtb_ref[...])
    beta_ref[...] = jax.nn.sigmoid(sm[:, GATE_BT:GATE_BT + NG])
    _, incl_f, _, _ = _chunk_masks(ts, False)
    _, incl_b, _, _ = _chunk_masks(ts, True)
    tri_f = incl_f.astype(F32)
    tri_b = incl_b.astype(F32)
    is_fwd = lax.broadcasted_iota(jnp.int32, (ts, NG), 1) < GDN_HEADS
    gc_ref[...] = jnp.where(is_fwd, _dot_exact(tri_f, g), _dot_exact(tri_b, g))
    smt = smt_ref[...]
    gt = -jnp.exp(alogt_ref[...]) * _softplus(smt[GATE_A:GATE_A + NG, :] + dtbt_ref[...])
    is_fwd_t = lax.broadcasted_iota(jnp.int32, (NG, ts), 0) < GDN_HEADS
    gct_ref[...] = jnp.where(is_fwd_t, _dot_nt_exact(gt, tri_f), _dot_nt_exact(gt, tri_b))
    same, _, _, _ = _chunk_masks(ts, False)
    glt_ref[...] = _dot_exact(gt, same.astype(F32))


def _gdn_prep(qkv, sm, smt, conv_w, a_log, dt_bias, b, s, ts):
    t = b * s
    nj = s // ts
    c = qkv.shape[1]
    hb = ts // HALO
    d = GDN_HEADS * GDN_DK
    tile = lambda n: pl.BlockSpec((ts, n), lambda bi, j: (bi * nj + j, 0))
    tile_t = lambda n: pl.BlockSpec((n, ts), lambda bi, j: (0, bi * nj + j))
    return pl.pallas_call(
        functools.partial(_gdn_prep_kernel, ts=ts),
        grid=(b, nj),
        in_specs=[tile(c),
                  pl.BlockSpec((HALO, c), lambda bi, j: (jnp.maximum((bi * nj + j) * hb - 1, 0), 0)),
                  pl.BlockSpec((HALO, c), lambda bi, j: (jnp.minimum((bi * nj + j + 1) * hb, t // HALO - 1), 0)),
                  tile(sm.shape[1]), tile_t(smt.shape[0]),
                  _resident(conv_w.shape), _resident((1, NG)), _resident((1, NG)),
                  _resident((NG, 1)), _resident((NG, 1))],
        out_specs=[tile(d), tile(d), tile(d),
                   pl.BlockSpec((GDN_HEADS, GDN_DK, ts), lambda bi, j: (0, 0, bi * nj + j)),
                   tile(NG), tile(NG), tile_t(NG), tile_t(NG)],
        out_shape=[jax.ShapeDtypeStruct((t, d), BF16)] * 3
        + [jax.ShapeDtypeStruct((GDN_HEADS, GDN_DK, t), BF16),
           jax.ShapeDtypeStruct((t, NG), F32), jax.ShapeDtypeStruct((t, NG), F32),
           jax.ShapeDtypeStruct((NG, t), F32), jax.ShapeDtypeStruct((NG, t), F32)],
        scratch_shapes=[pltpu.VMEM((ts + 2 * HALO, c), F32)],
        compiler_params=_params("arbitrary", "arbitrary"),
        name="gdn_prep",
    )(qkv, qkv, qkv, sm, smt, conv_w, a_log.reshape(1, NG), dt_bias.reshape(1, NG),
      a_log.reshape(NG, 1), dt_bias.reshape(NG, 1))


STEP = 2 * CHUNK


def _gdn_scan_kernel(*refs, revs):
    nd = len(revs)
    ins = [refs[8 * d:8 * d + 8] for d in range(nd)]
    outs = refs[8 * nd:9 * nd]
    s_ref = refs[9 * nd]

    @pl.when(pl.program_id(1) == 0)
    def _():
        s_ref[...] = jnp.zeros(s_ref.shape, F32)

    masks = [_chunk_masks(STEP, rev) for rev in revs]
    lane = lax.broadcasted_iota(jnp.int32, (1, STEP), 1)
    zeros = jnp.zeros((CHUNK, GDN_DV), F32)
    chains = [(d, h) for d in range(nd) for h in range(GDN_HEADS)]
    ids = range(len(chains))
    sl = [slice(h * GDN_DK, (h + 1) * GDN_DK) for _, h in chains]
    col = [(GDN_HEADS if revs[d] else 0) + h for d, h in chains]
    incl = [masks[d][1] for d, _ in chains]
    strict = [masks[d][2] for d, _ in chains]
    eye = masks[0][3].astype(F32)
    left = lax.broadcasted_iota(jnp.int32, (CHUNK, STEP), 1) < CHUNK
    q_ref, k_ref, v_ref, kt_ref, gc_ref, beta_ref, gct_ref, glt_ref = (
        [ins[d][i] for d, _ in chains] for i in range(8))
    q = [q_ref[c][:, sl[c]] for c in ids]
    kt = [kt_ref[c][chains[c][1]] for c in ids]
    gcol = [gc_ref[c][:, col[c]:col[c] + 1] for c in ids]
    bcol = [beta_ref[c][:, col[c]:col[c] + 1] for c in ids]
    grow = [gct_ref[c][col[c]:col[c] + 1, :] for c in ids]
    glrow = [glt_ref[c][col[c]:col[c] + 1, :] for c in ids]
    decay = [jnp.exp(jnp.where(incl[c], gcol[c] - grow[c], -jnp.inf)) for c in ids]
    egc = [jnp.exp(gcol[c]) for c in ids]
    kb = [k_ref[c][:, sl[c]].astype(F32) * bcol[c] for c in ids]
    aq = [_dot(jnp.concatenate([kb[c].astype(BF16), q[c]], axis=0), kt[c]) for c in ids]
    nil = [-jnp.where(strict[c], aq[c][:STEP] * decay[c], 0.0) for c in ids]
    intra = [(aq[c][STEP:] * decay[c]).astype(BF16) for c in ids]
    wide = lambda m: m[:CHUNK] + m[CHUNK:]
    blocks = lambda w: jnp.concatenate([jnp.where(left, w, 0.0), jnp.where(left, 0.0, w)], axis=0)
    pw = [wide(nil[c]) for c in ids]
    iw = [wide(eye) + pw[c] for c in ids]
    pw = [_dot(pw[c].astype(BF16), nil[c].astype(BF16)) for c in ids]
    for _ in range(int(math.log2(CHUNK)) - 2):
        pd = [blocks(pw[c]).astype(BF16) for c in ids]
        both = [_dot(jnp.concatenate([iw[c], pw[c]], axis=0).astype(BF16), pd[c]) for c in ids]
        iw = [iw[c] + both[c][:CHUNK] for c in ids]
        pw = [both[c][CHUNK:] for c in ids]
    iw = [iw[c] + _dot(iw[c].astype(BF16), blocks(pw[c]).astype(BF16)) for c in ids]
    inv = [blocks(iw[c]) for c in ids]
    rhs = [jnp.concatenate([(v_ref[c][:, sl[c]].astype(F32) * bcol[c]).astype(BF16),
                            (kb[c] * egc[c]).astype(BF16)], axis=1) for c in ids]
    uw = [_dot(inv[c].astype(BF16), rhs[c]) for c in ids]
    qd = [(q[c].astype(F32) * egc[c]).astype(BF16) for c in ids]
    kdt = [(kt[c].astype(F32) * jnp.exp(glrow[c] - grow[c])).astype(BF16) for c in ids]
    state = [s_ref[c] for c in ids]
    for half in range(2):
        ch = [1 - half if revs[d] else half for d, _ in chains]
        r = [slice(ch[c] * CHUNK, (ch[c] + 1) * CHUNK) for c in ids]
        ws = [_dot(jnp.concatenate([uw[c][r[c], GDN_DV:].astype(BF16), qd[c][r[c]]], axis=0),
                   state[c].astype(BF16)) for c in ids]
        v_new = [uw[c][r[c], :GDN_DV] - ws[c][:CHUNK] for c in ids]
        vpad = [jnp.concatenate([v_new[c], zeros] if ch[c] == 0 else [zeros, v_new[c]], axis=0).astype(BF16)
                for c in ids]
        res = [_dot(jnp.concatenate([intra[c][r[c]], kdt[c]], axis=0), vpad[c]) for c in ids]
        for c in ids:
            outs[chains[c][0]][r[c], sl[c]] = (ws[c][CHUNK:] + res[c][:CHUNK]).astype(outs[0].dtype)
        gl = [jnp.sum(jnp.where(lane == ch[c] * CHUNK, glrow[c], 0.0), axis=-1, keepdims=True) for c in ids]
        state = [state[c] * jnp.exp(gl[c]) + res[c][CHUNK:] for c in ids]
    for c in ids:
        s_ref[c] = state[c]


def _gdn_scan(q, k, v, kt, gc, beta, gct, glt, b, s, revs):
    t = b * s
    n = s // STEP
    d = GDN_HEADS * GDN_DK
    in_specs, operands, out_specs = [], [], []
    for rev in revs:
        blk = (lambda bi, i: bi * n + (n - 1 - i)) if rev else (lambda bi, i: bi * n + i)
        tile = lambda w, blk=blk: pl.BlockSpec((STEP, w), lambda bi, i: (blk(bi, i), 0))
        tile_t = lambda w, blk=blk: pl.BlockSpec((w, STEP), lambda bi, i: (0, blk(bi, i)))
        in_specs += [tile(d), tile(d), tile(d),
                     pl.BlockSpec((GDN_HEADS, GDN_DK, STEP), lambda bi, i, blk=blk: (0, 0, blk(bi, i))),
                     tile(NG), tile(NG), tile_t(NG), tile_t(NG)]
        operands += [q, k, v, kt, gc, beta, gct, glt]
        out_specs.append(tile(d))
    return pl.pallas_call(
        functools.partial(_gdn_scan_kernel, revs=tuple(revs)),
        grid=(b, n),
        in_specs=in_specs,
        out_specs=out_specs,
        out_shape=[jax.ShapeDtypeStruct((t, d), BF16)] * len(revs),
        scratch_shapes=[pltpu.VMEM((len(revs) * GDN_HEADS, GDN_DK, GDN_DV), F32)],
        compiler_params=_params("arbitrary", "arbitrary"),
        name="gdn_scan",
    )(*operands)


def _gdn(qkv, sm, smt, conv_w, a_log, dt_bias, b, s):
    ts = min(256, s)
    q, k, v, kt, gc, beta, gct, glt = _gdn_prep(qkv, sm, smt, conv_w, a_log, dt_bias, b, s, ts)
    o_f, o_b = _gdn_scan(q, k, v, kt, gc, beta, gct, glt, b, s, (False, True))
    return o_f, o_b


def _outproj_kernel(x_ref, att_ref, of_ref, ob_ref, z_ref, ga_ref, gg_ref, wo_ref, nf_ref, wr_ref, br_ref,
                    y_ref, yn_ref, lg_ref):
    att = _rms(att_ref[...].astype(F32), ga_ref[...]).astype(BF16)
    o = of_ref[...].astype(F32) + ob_ref[...].astype(F32)
    z = z_ref[...].astype(F32)
    gg = gg_ref[...]
    lin = []
    for h in range(GDN_HEADS):
        sl = slice(h * GDN_DV, (h + 1) * GDN_DV)
        zh = z[:, sl]
        lin.append((_rms(o[:, sl], gg) * (zh * jax.nn.sigmoid(zh))).astype(BF16))
    lin = jnp.concatenate(lin, axis=-1)
    y = x_ref[...] + _dot(att, wo_ref[:D_ATTN, :]) + _dot(lin, wo_ref[D_ATTN:, :])
    y_ref[...] = y
    yn = _rms(y, nf_ref[...]).astype(BF16)
    _store_rows(yn_ref, _pack_halves(yn))
    lg_ref[...] = _dot(yn, wr_ref[...]) + br_ref[...]


def _outproj(x2d, att, o_f, o_b, z, g_attn, g_gdn, w_out, norm_ffn, w_r, b_r, tm):
    t, d = x2d.shape
    row = lambda n: pl.BlockSpec((tm, n), lambda i: (i, 0))
    return pl.pallas_call(
        _outproj_kernel,
        grid=(t // tm,),
        in_specs=[row(d), row(D_ATTN), row(D_GDN), row(D_GDN), row(D_GDN), _resident((1, D_ATTN)),
                  _resident((1, GDN_DV)), _resident(w_out.shape), _resident((1, d)), _resident(w_r.shape),
                  _resident((1, LANE))],
        out_specs=[row(d), pl.BlockSpec((tm * ROW_SUB, LANE), lambda i: (i, 0)), row(LANE)],
        out_shape=[jax.ShapeDtypeStruct((t, d), F32), jax.ShapeDtypeStruct((t * ROW_SUB, LANE), jnp.uint32),
                   jax.ShapeDtypeStruct((t, LANE), F32)],
        compiler_params=_params("arbitrary"),
        name="outproj",
    )(x2d, att, o_f, o_b, z, g_attn.reshape(1, -1), g_gdn.reshape(1, -1), w_out, norm_ffn.reshape(1, -1),
      w_r, b_r)


SLOT_ROWS = 8


def _router_kernel(lg_ref, gate_ref, slot_ref, cnt_ref, carry_ref):
    @pl.when(pl.program_id(0) == 0)
    def _():
        carry_ref[...] = jnp.zeros(carry_ref.shape, F32)

    lg = lg_ref[...]
    lane = lax.broadcasted_iota(jnp.int32, lg.shape, 1)
    neg = -jnp.inf

    def first_max(x):
        mx = jnp.max(x, axis=-1, keepdims=True)
        return mx, jnp.min(jnp.where(x == mx, lane, LANE), axis=-1, keepdims=True)

    gl = jnp.where(lane < N_GROUPS, lg, neg)
    gmax, grp = first_max(gl)
    grp_p = 1.0 / jnp.sum(jnp.exp(gl - gmax), axis=-1, keepdims=True)
    lo = N_GROUPS + grp * EXPERTS_PER_GROUP
    el = jnp.where((lane >= lo) & (lane < lo + EXPERTS_PER_GROUP), lg, neg)
    l1, i1 = first_max(el)
    l2, i2 = first_max(jnp.where(lane == i1, neg, el))
    e = jnp.exp(l2 - l1)
    g1 = grp_p / (1.0 + e)
    e1, e2 = i1 - N_GROUPS, i2 - N_GROUPS
    gate_ref[...] = jnp.where(lane == 0, g1, jnp.where(lane == 1, g1 * e, 0.0))
    tm = lg.shape[0]
    oh1, oh2 = lane == e1, lane == e2
    before = (lax.broadcasted_iota(jnp.int32, (tm, tm), 1)
              < lax.broadcasted_iota(jnp.int32, (tm, tm), 0)).astype(BF16)
    r1 = _dot(before, oh1.astype(BF16))
    r2 = _dot(before, oh2.astype(BF16))
    c1 = jnp.sum(oh1.astype(F32), axis=0, keepdims=True)
    c2 = jnp.sum(oh2.astype(F32), axis=0, keepdims=True)
    carry = carry_ref[...]
    rank1 = jnp.sum(jnp.where(oh1, r1 + carry, 0.0), axis=-1, keepdims=True)
    rank2 = jnp.sum(jnp.where(oh2, r2 + (carry + c1), 0.0), axis=-1, keepdims=True)
    info = jnp.where(lane == 0, e1.astype(F32), jnp.where(lane == 1, e2.astype(F32),
                     jnp.where(lane == 2, rank1, jnp.where(lane == 3, rank2, 0.0))))
    slot_ref[...] = info.T[:SLOT_ROWS, :].astype(jnp.int32)
    carry_ref[...] = carry + c1 + c2
    cnt_ref[...] = (carry + c1 + c2).astype(jnp.int32)


def _router(logits, tm):
    t = logits.shape[0]
    row = pl.BlockSpec((tm, LANE), lambda i: (i, 0))
    return pl.pallas_call(
        _router_kernel,
        grid=(t // tm,),
        in_specs=[row],
        out_specs=[row, pl.BlockSpec((SLOT_ROWS, tm), lambda i: (0, i)), pl.BlockSpec((1, LANE), lambda i: (0, 0))],
        out_shape=[jax.ShapeDtypeStruct((t, LANE), F32), jax.ShapeDtypeStruct((SLOT_ROWS, t), jnp.int32),
                   jax.ShapeDtypeStruct((1, LANE), jnp.int32)],
        scratch_shapes=[pltpu.VMEM((1, LANE), F32)],
        compiler_params=_params("arbitrary"),
        name="router",
    )(logits)


ROW_UNROLL = 8


def _start_rows(tm, copy):
    def start(t, c):
        for k in range(TOP_K):
            copy(t, k).start()
        return c

    lax.fori_loop(0, tm, start, 0, unroll=ROW_UNROLL)


def _wait_rows(tm, copy):
    def wait(t, c):
        for k in range(TOP_K):
            copy(t, k).wait()
        return c

    lax.fori_loop(0, tm, wait, 0, unroll=ROW_UNROLL)


def _row_copies(tm, copy):
    _start_rows(tm, copy)
    _wait_rows(tm, copy)


def _dispatch_kernel(*refs):
    dest_refs, yn_ref, (xb_ref, sem) = refs[:TOP_K], refs[TOP_K], refs[-2:]
    _row_copies(yn_ref.shape[0] // ROW_SUB, lambda t, k: pltpu.make_async_copy(
        yn_ref.at[_row_tile(t)], xb_ref.at[_row_tile(dest_refs[k][t])], sem))


def _dest_specs(tm):
    return [pl.BlockSpec((tm,), lambda i: (i,), memory_space=pltpu.SMEM) for _ in range(TOP_K)]


def _dispatch(dests, yn, xb, tm, n_rows):
    t = yn.shape[0] // ROW_SUB
    prev = [] if xb is None else [xb]
    return pl.pallas_call(
        _dispatch_kernel,
        grid=(t // tm,),
        in_specs=_dest_specs(tm) + [pl.BlockSpec((tm * ROW_SUB, LANE), lambda i: (i, 0))]
        + [pl.BlockSpec(memory_space=pl.ANY) for _ in prev],
        out_specs=pl.BlockSpec(memory_space=pl.ANY),
        out_shape=jax.ShapeDtypeStruct((n_rows * ROW_SUB, LANE), jnp.uint32),
        scratch_shapes=[pltpu.SemaphoreType.DMA],
        input_output_aliases={TOP_K + 1: 0} if prev else {},
        compiler_params=_params("arbitrary", has_side_effects=True),
        name="moe_dispatch",
    )(*dests, yn, *prev)


MOE_BLOCK = 512


def _moe_kernel(be_ref, nu_ref, nv_ref, xb_ref, wg_ref, wu_ref, wd_ref, yb_ref, wg_bf, wu_bf, wd_bf):
    i = pl.program_id(0)

    @pl.when(i < nu_ref[0])
    def _():
        @pl.when((i == 0) | (be_ref[i] != be_ref[jnp.maximum(i - 1, 0)]))
        def _():
            wg_bf[...] = wg_ref[0].astype(BF16)
            wu_bf[...] = wu_ref[0].astype(BF16)
            wd_bf[...] = wd_ref[0].astype(BF16)

        x = _load_rows(xb_ref)
        valid = lax.broadcasted_iota(jnp.int32, (x.shape[0], 1), 0) < nv_ref[i]
        lo, hi = _unpack_halves(jnp.where(valid, x, jnp.uint32(0)))
        lo, hi = lo.astype(BF16), hi.astype(BF16)
        w = lo.shape[1]
        a = _dot(lo, wg_bf[:w, :]) + _dot(hi, wg_bf[w:, :])
        u = _dot(lo, wu_bf[:w, :]) + _dot(hi, wu_bf[w:, :])
        hdn = (a * jax.nn.sigmoid(a) * u).astype(BF16)
        _store_rows(yb_ref, _pack_halves(_dot(hdn, wd_bf[...]).astype(BF16)))


def _moe_blocks(block_e, n_used, n_valid, xb, w_gate, w_up, w_down):
    p = xb.shape[0] // ROW_SUB
    d = w_gate.shape[1]
    nb = p // MOE_BLOCK
    clamp = lambda i, nu: jnp.minimum(i, nu[0] - 1)
    rows = pl.BlockSpec((MOE_BLOCK * ROW_SUB, LANE), lambda i, be, nu, nv: (clamp(i, nu), 0))
    grid_spec = pltpu.PrefetchScalarGridSpec(
        num_scalar_prefetch=3,
        grid=(nb,),
        in_specs=[rows,
                  pl.BlockSpec((1, d, D_EXPERT), lambda i, be, nu, nv: (be[clamp(i, nu)], 0, 0)),
                  pl.BlockSpec((1, d, D_EXPERT), lambda i, be, nu, nv: (be[clamp(i, nu)], 0, 0)),
                  pl.BlockSpec((1, D_EXPERT, d), lambda i, be, nu, nv: (be[clamp(i, nu)], 0, 0))],
        out_specs=rows,
        scratch_shapes=[pltpu.VMEM((d, D_EXPERT), BF16), pltpu.VMEM((d, D_EXPERT), BF16),
                        pltpu.VMEM((D_EXPERT, d), BF16)],
    )
    return pl.pallas_call(
        _moe_kernel,
        grid_spec=grid_spec,
        out_shape=jax.ShapeDtypeStruct(xb.shape, jnp.uint32),
        compiler_params=_params("arbitrary"),
        name="moe_experts",
    )(block_e, n_used, n_valid, xb, w_gate, w_up, w_down)


def _combine_kernel(*refs):
    dest_refs, next_refs = refs[:TOP_K], refs[TOP_K:2 * TOP_K]
    y_ref, gate_ref, g_ref, yb_ref, o_ref, buf, sems = refs[2 * TOP_K:]
    i, n, tm = pl.program_id(0), pl.num_programs(0), y_ref.shape[0]
    slot = i % 2

    def gather(rows, slot):
        return lambda t, k: pltpu.make_async_copy(
            yb_ref.at[_row_tile(rows[k][t])], buf.at[slot, k, _row_tile(t)], sems.at[slot])

    @pl.when(i == 0)
    def _():
        _start_rows(tm, gather(dest_refs, 0))

    @pl.when(i + 1 < n)
    def _():
        _start_rows(tm, gather(next_refs, 1 - slot))

    _wait_rows(tm, gather(dest_refs, slot))
    gate = gate_ref[...]
    acc_lo = acc_hi = None
    for k in range(TOP_K):
        lo, hi = _unpack_halves(_load_rows(buf.at[slot, k]))
        gk = gate[:, k:k + 1]
        acc_lo = gk * lo if acc_lo is None else acc_lo + gk * lo
        acc_hi = gk * hi if acc_hi is None else acc_hi + gk * hi
    y = y_ref[...] + jnp.concatenate([acc_lo, acc_hi], axis=-1)
    o_ref[...] = _rms(y, g_ref[...])


def _combine(dests, y, gate, g, yb, tm):
    t, d = y.shape
    n = t // tm
    next_specs = [pl.BlockSpec((tm,), lambda i: (jnp.minimum(i + 1, n - 1),), memory_space=pltpu.SMEM)
                  for _ in range(TOP_K)]
    return pl.pallas_call(
        _combine_kernel,
        grid=(n,),
        in_specs=_dest_specs(tm) + next_specs + [pl.BlockSpec((tm, d), lambda i: (i, 0)),
                                    pl.BlockSpec((tm, LANE), lambda i: (i, 0)),
                                    _resident((1, d)), pl.BlockSpec(memory_space=pl.ANY)],
        out_specs=pl.BlockSpec((tm, d), lambda i: (i, 0)),
        out_shape=jax.ShapeDtypeStruct((t, d), F32),
        scratch_shapes=[pltpu.VMEM((2, TOP_K, tm * ROW_SUB, LANE), jnp.uint32), pltpu.SemaphoreType.DMA((2,))],
        compiler_params=_params("arbitrary"),
        name="moe_combine",
    )(*dests, *dests, y, gate, g.reshape(1, -1), yb)


def _route_plan(counts, n_slots):
    padded = (counts + MOE_BLOCK - 1) // MOE_BLOCK * MOE_BLOCK
    pend = jnp.cumsum(padded)
    pstart = pend - padded
    n_blocks = -(-n_slots // MOE_BLOCK) + N_EXPERTS
    first_row = jnp.arange(n_blocks, dtype=jnp.int32) * MOE_BLOCK
    block_e = jnp.minimum(jnp.sum(pend[None, :] <= first_row[:, None], axis=1), N_EXPERTS - 1).astype(jnp.int32)
    n_used = (pend[-1] // MOE_BLOCK).astype(jnp.int32).reshape(1)
    n_valid = jnp.clip(jnp.take(pstart + counts, block_e) - first_row, 0, MOE_BLOCK).astype(jnp.int32)
    return pstart.astype(jnp.int32), block_e, n_used, n_valid, n_blocks


def _encoder_front(x, p, wts):
    b, s, d = x.shape
    x2d = x.reshape(b * s, d)
    tm = min(512, b * s)
    lat, sm, smt, qkv, z = _inproj(x2d, p["norm_mix"].reshape(1, -1), wts["w_lat"], wts["w_sm"], wts["w_smt"],
                                   wts["w_qkv"], wts["w_z"], tm)
    att = _mla(lat, sm, p["g_q_lora"], p["g_kv_lora"], p["w_uq"], p["w_ukv"], b, s)
    o_f, o_b = _gdn(qkv, sm, smt, p["conv_w"], p["a_log"], p["dt_bias"], b, s)
    return _outproj(x2d, att, o_f, o_b, z, p["g_attn_out"], p["g_gdn_out"], wts["w_out"], p["norm_ffn"],
                    wts["w_r"], wts["b_r"], tm)


def _prep_weights(p):
    w_in = p["w_in"]
    o = np.cumsum([0, Q_LORA, KV_LORA, QK_ROPE, GDN_QKV, D_GDN, NG, NG])
    d = w_in.shape[0]
    half = QK_ROPE // 2
    w_sm = jnp.concatenate([w_in[:, o[2]:o[3]], w_in[:, o[5]:o[7]], jnp.zeros((d, LANE - QK_ROPE - 2 * NG), F32),
                            w_in[:, o[2] + half:o[3]], w_in[:, o[2]:o[2] + half],
                            jnp.zeros((d, LANE - QK_ROPE), F32)], axis=1).astype(BF16)
    w_r = jnp.concatenate([p["w_router_group"], p["w_router_expert"],
                           jnp.zeros((d, LANE - N_GROUPS - N_EXPERTS), F32)], axis=1).astype(BF16)
    b_r = jnp.concatenate([p["b_router_group"], p["b_router_expert"],
                           jnp.zeros((LANE - N_GROUPS - N_EXPERTS,), F32)]).reshape(1, LANE)
    return dict(w_lat=w_in[:, :o[2]].astype(BF16), w_sm=w_sm, w_smt=w_sm[:, :LANE].T, w_qkv=w_in[:, o[3]:o[4]].astype(BF16),
                w_z=w_in[:, o[4]:o[5]].astype(BF16), w_out=p["w_out"].astype(BF16), w_r=w_r, b_r=b_r)


ROW_TILE = 512


def _encode(xs, p):
    wts = _prep_weights(p)
    fronts = [_encoder_front(x, p, wts) for x in xs]
    logits = jnp.concatenate([f[2] for f in fronts], axis=0)
    t = logits.shape[0]
    gate, slots, counts = _router(logits, min(ROW_TILE, t))
    pstart, block_e, n_used, n_valid, n_blocks = _route_plan(counts[0, :N_EXPERTS], t * TOP_K)
    dests = [jnp.take(pstart, slots[k]) + slots[TOP_K + k] for k in range(TOP_K)]
    xb = None
    bounds = np.cumsum([0] + [f[0].shape[0] for f in fronts])
    for f, t0, t1 in zip(fronts, bounds[:-1], bounds[1:]):
        xb = _dispatch([d[t0:t1] for d in dests], f[1], xb, min(ROW_TILE, t1 - t0), n_blocks * MOE_BLOCK)
    yb = _moe_blocks(block_e, n_used, n_valid, xb, p["w_gate"], p["w_up"], p["w_down"])
    outs = []
    for x, f, t0, t1 in zip(xs, fronts, bounds[:-1], bounds[1:]):
        out = _combine([d[t0:t1] for d in dests], f[0], gate[t0:t1], p["norm_final"], yb, min(ROW_TILE, t1 - t0))
        outs.append(out.reshape(x.shape))
    return outs


def kernel(x_prompt, x_sample, norm_mix, w_in, g_q_lora, g_kv_lora, w_uq, w_ukv, g_attn_out, conv_w, a_log,
           dt_bias, g_gdn_out, w_out, norm_ffn, w_router_group, b_router_group, w_router_expert,
           b_router_expert, w_gate, w_up, w_down, norm_final):
    p = dict(norm_mix=norm_mix[0], w_in=w_in[0], g_q_lora=g_q_lora[0], g_kv_lora=g_kv_lora[0], w_uq=w_uq[0],
             w_ukv=w_ukv[0], g_attn_out=g_attn_out[0], conv_w=conv_w[0], a_log=a_log[0], dt_bias=dt_bias[0],
             g_gdn_out=g_gdn_out[0], w_out=w_out[0], norm_ffn=norm_ffn[0], w_router_group=w_router_group[0],
             b_router_group=b_router_group[0], w_router_expert=w_router_expert[0],
             b_router_expert=b_router_expert[0], w_gate=w_gate[0], w_up=w_up[0], w_down=w_down[0],
             norm_final=norm_final)
    y_prompt, y_sample = _encode([x_prompt, x_sample], p)
    return (y_prompt, y_sample)
```

```python
import functools
import math

import jax
import jax.numpy as jnp
import numpy as np
from jax import lax
from jax.experimental import pallas as pl
from jax.experimental.pallas import tpu as pltpu

F32 = jnp.float32
BF16 = jnp.bfloat16

D_MODEL = 2048
MLA_HEADS = 8
Q_LORA = 512
KV_LORA = 512
QK_NOPE = 128
QK_ROPE = 64
V_HEAD = 128
ROPE_THETA = 10000.0
GDN_HEADS = 8
GDN_DK = 128
GDN_DV = 128
GDN_QKV = GDN_HEADS * (2 * GDN_DK + GDN_DV)
CONV_K = 5
CHUNK = 64
D_ATTN = MLA_HEADS * V_HEAD
D_GDN = GDN_HEADS * GDN_DV
N_GROUPS = 8
EXPERTS_PER_GROUP = 8
N_EXPERTS = N_GROUPS * EXPERTS_PER_GROUP
TOP_K = 2
D_EXPERT = 512
EPS = 1e-6

LANE = 128
QK_PAD = 256
VMEM_LIMIT = 56 * 1024 * 1024


def _params(*sem, **kw):
    return pltpu.CompilerParams(dimension_semantics=sem, vmem_limit_bytes=VMEM_LIMIT, **kw)


def _resident(shape):
    return pl.BlockSpec(shape, lambda *_: (0,) * len(shape), pipeline_mode=pl.Buffered(1))


def _rms(x, g):
    return x * lax.rsqrt(jnp.mean(x * x, axis=-1, keepdims=True) + EPS) * g


def _dot(a, b):
    return jnp.dot(a, b, preferred_element_type=F32)


def _dot_nt(a, b):
    return lax.dot_general(a, b, (((1,), (1,)), ((), ())), preferred_element_type=F32)


def _pack_halves(x):
    w = x.shape[1] // 2
    lo = pltpu.bitcast(x[:, :w].astype(F32), jnp.uint32)
    hi = pltpu.bitcast(x[:, w:].astype(F32), jnp.uint32)
    return (hi & jnp.uint32(0xFFFF0000)) | (lo >> 16)


ROW_SUB = 8


def _row_tile(r):
    return pl.ds(pl.multiple_of(r * ROW_SUB, ROW_SUB), ROW_SUB)


def _store_rows(ref, u):
    n = u.shape[0]
    for s in range(ROW_SUB):
        ref[pl.ds(s, n, stride=ROW_SUB), :] = u[:, s * LANE:(s + 1) * LANE]


def _load_rows(ref):
    n = ref.shape[0] // ROW_SUB
    return jnp.concatenate([ref[pl.ds(s, n, stride=ROW_SUB), :] for s in range(ROW_SUB)], axis=1)


def _unpack_halves(u):
    lo = pltpu.bitcast(u << 16, F32)
    hi = pltpu.bitcast(u & jnp.uint32(0xFFFF0000), F32)
    return lo, hi


def _inproj_kernel(x_ref, g_ref, w_lat_ref, w_sm_ref, w_smt_ref, w_qkv_ref, w_z_ref,
                   lat_ref, sm_ref, smt_ref, qkv_ref, z_ref):
    xn = _rms(x_ref[...], g_ref[...]).astype(BF16)
    lat_ref[...] = _dot(xn, w_lat_ref[...]).astype(BF16)
    sm_ref[...] = _dot(xn, w_sm_ref[...])
    smt_ref[...] = _dot_nt(w_smt_ref[...], xn)
    qkv_ref[...] = _dot(xn, w_qkv_ref[...]).astype(BF16)
    z_ref[...] = _dot(xn, w_z_ref[...]).astype(BF16)


def _inproj(x2d, g, w_lat, w_sm, w_smt, w_qkv, w_z, tm):
    t, d = x2d.shape
    n_lat, n_sm, n_qkv, n_z = w_lat.shape[1], w_sm.shape[1], w_qkv.shape[1], w_z.shape[1]
    n_smt = w_smt.shape[0]
    row = lambda n: pl.BlockSpec((tm, n), lambda i: (i, 0))
    return pl.pallas_call(
        _inproj_kernel,
        grid=(t // tm,),
        in_specs=[row(d), _resident((1, d)), _resident(w_lat.shape), _resident(w_sm.shape),
                  _resident(w_smt.shape), _resident(w_qkv.shape), _resident(w_z.shape)],
        out_specs=[row(n_lat), row(n_sm), pl.BlockSpec((n_smt, tm), lambda i: (0, i)), row(n_qkv), row(n_z)],
        out_shape=[jax.ShapeDtypeStruct((t, n_lat), BF16), jax.ShapeDtypeStruct((t, n_sm), F32),
                   jax.ShapeDtypeStruct((n_smt, t), F32), jax.ShapeDtypeStruct((t, n_qkv), BF16),
                   jax.ShapeDtypeStruct((t, n_z), BF16)],
        compiler_params=_params("arbitrary"),
        name="inproj",
    )(x2d, g, w_lat, w_sm, w_smt, w_qkv, w_z)


def _mla_up_kernel(lat_ref, sm_ref, gq_ref, gkv_ref, wqt_ref, wk_ref, wvt_ref,
                   cos_t_ref, sin_t_ref, ck_ref, sk_ref, qt_ref, k_ref, vt_ref):
    lat = lat_ref[...].astype(F32)
    cqn = _rms(lat[:, :Q_LORA], gq_ref[...]).astype(BF16)
    ckvn = _rms(lat[:, Q_LORA:], gkv_ref[...]).astype(BF16)
    half = QK_ROPE // 2
    dqk = QK_NOPE + QK_ROPE
    qt = _dot_nt(wqt_ref[...], cqn)
    cos_t, sin_t = cos_t_ref[...], sin_t_ref[...]
    zero = jnp.zeros((QK_PAD - dqk, qt.shape[1]), BF16)
    for h in range(MLA_HEADS):
        lo = h * dqk
        x1 = qt[lo + QK_NOPE:lo + QK_NOPE + half, :]
        x2 = qt[lo + QK_NOPE + half:lo + dqk, :]
        qt_ref[0, h, 0, 0:QK_NOPE, :] = qt[lo:lo + QK_NOPE, :].astype(BF16)
        qt_ref[0, h, 0, QK_NOPE:QK_NOPE + half, :] = (x1 * cos_t - x2 * sin_t).astype(BF16)
        qt_ref[0, h, 0, QK_NOPE + half:dqk, :] = (x2 * cos_t + x1 * sin_t).astype(BF16)
        qt_ref[0, h, 0, dqk:, :] = zero
    kn = _dot(ckvn, wk_ref[...])
    sm = sm_ref[...]
    pe = (sm[:, :LANE] * ck_ref[...] + sm[:, LANE:] * sk_ref[...]).astype(BF16)
    for h in range(MLA_HEADS):
        k_ref[:, h * QK_PAD:h * QK_PAD + QK_NOPE] = kn[:, h * QK_NOPE:(h + 1) * QK_NOPE].astype(BF16)
        k_ref[:, h * QK_PAD + QK_NOPE:(h + 1) * QK_PAD] = pe
    vt = _dot_nt(wvt_ref[...], ckvn)
    for h in range(MLA_HEADS):
        vt_ref[0, h, 0, 0:V_HEAD, :] = vt[h * V_HEAD:(h + 1) * V_HEAD, :].astype(BF16)
        vt_ref[0, h, 0, V_HEAD:, :] = jnp.ones((V_ROWS - V_HEAD, vt.shape[1]), BF16)


def _mla_up(lat, sm, gq, gkv, wqt, wk, wvt, cos_t, sin_t, ck, sk, b, s, ts):
    t = b * s
    nj = s // ts
    hk = MLA_HEADS * QK_PAD
    return pl.pallas_call(
        _mla_up_kernel,
        grid=(b, nj),
        in_specs=[pl.BlockSpec((ts, lat.shape[1]), lambda bi, j: (bi * nj + j, 0)),
                  pl.BlockSpec((ts, sm.shape[1]), lambda bi, j: (bi * nj + j, 0)),
                  _resident(gq.shape), _resident(gkv.shape), _resident(wqt.shape), _resident(wk.shape),
                  _resident(wvt.shape),
                  pl.BlockSpec((QK_ROPE // 2, ts), lambda bi, j: (0, j)),
                  pl.BlockSpec((QK_ROPE // 2, ts), lambda bi, j: (0, j)),
                  pl.BlockSpec((ts, LANE), lambda bi, j: (j, 0)),
                  pl.BlockSpec((ts, LANE), lambda bi, j: (j, 0))],
        out_specs=[pl.BlockSpec((1, MLA_HEADS, 1, QK_PAD, ts), lambda bi, j: (bi, 0, j, 0, 0)),
                   pl.BlockSpec((ts, hk), lambda bi, j: (bi * nj + j, 0)),
                   pl.BlockSpec((1, MLA_HEADS, 1, V_ROWS, ts), lambda bi, j: (bi, 0, j, 0, 0))],
        out_shape=[jax.ShapeDtypeStruct((b, MLA_HEADS, nj, QK_PAD, ts), BF16),
                   jax.ShapeDtypeStruct((t, hk), BF16),
                   jax.ShapeDtypeStruct((b, MLA_HEADS, nj, V_ROWS, ts), BF16)],
        compiler_params=_params("arbitrary", "arbitrary"),
        name="mla_up",
    )(lat, sm, gq, gkv, wqt, wk, wvt, cos_t, sin_t, ck, sk)


ATTN_UNROLL = 8
ATTN_Q_ROWS = 1024
ATTN_SHORT_SEQ = 2048
V_ROWS = V_HEAD + 16


def _attn_kernel(qt_ref, k_ref, vt_ref, o_ref, m_ref, acc_ref, s_buf, c_buf, *, nk, tk, unroll):
    qt = jnp.concatenate([qt_ref[0, 0, i] for i in range(qt_ref.shape[2])], axis=1)
    m_ref[...] = jnp.full(m_ref.shape, -jnp.inf, F32)
    acc_ref[...] = jnp.zeros(acc_ref.shape, F32)

    def scores(j, buf):
        s = _dot(k_ref[pl.ds(pl.multiple_of(j * tk, tk), tk), :], qt)
        s_buf[buf] = s
        c_buf[buf] = jnp.max(s, axis=0, keepdims=True)

    scores(0, 0)

    def step(j, cur, nxt):
        scores(jnp.minimum(j + 1, nk - 1), nxt)
        m_old = m_ref[...]
        m_new = jnp.maximum(m_old, c_buf[cur])
        alpha = jnp.exp2(m_old - m_new)
        p = jnp.exp2(s_buf[cur] - m_new).astype(BF16)
        acc_ref[...] = alpha * acc_ref[...] + _dot(vt_ref[0, 0, j], p)
        m_ref[...] = m_new

    def body(jj, carry):
        for u in range(unroll):
            step(unroll * jj + u, u % 2, 1 - u % 2)
        return carry

    lax.fori_loop(0, nk // unroll, body, 0)
    acc = acc_ref[...]
    o_ref[...] = (acc[:V_HEAD] / acc[V_HEAD:V_HEAD + 1]).T.astype(o_ref.dtype)


def _attention(qt, k, vt, b, s):
    nk, tk = vt.shape[2], vt.shape[4]
    qtiles = math.gcd(qt.shape[2], max(1, (s if s <= ATTN_SHORT_SEQ else ATTN_Q_ROWS) // qt.shape[4]))
    nq, tq = qt.shape[2] // qtiles, qt.shape[4] * qtiles
    assert nk % 2 == 0, "the score buffers alternate statically"
    unroll = math.gcd(nk, ATTN_UNROLL)
    return pl.pallas_call(
        functools.partial(_attn_kernel, nk=nk, tk=tk, unroll=unroll),
        grid=(b, MLA_HEADS, nq),
        in_specs=[pl.BlockSpec((1, 1, qtiles, QK_PAD, tq // qtiles), lambda bi, h, i: (bi, h, i, 0, 0)),
                  pl.BlockSpec((s, QK_PAD), lambda bi, h, i: (bi, h)),
                  pl.BlockSpec((1, 1, nk, V_ROWS, tk), lambda bi, h, i: (bi, h, 0, 0, 0))],
        out_specs=pl.BlockSpec((tq, V_HEAD), lambda bi, h, i: (bi * nq + i, h)),
        out_shape=jax.ShapeDtypeStruct((b * s, D_ATTN), BF16),
        scratch_shapes=[pltpu.VMEM((1, tq), F32), pltpu.VMEM((V_ROWS, tq), F32),
                        pltpu.VMEM((2, tk, tq), F32), pltpu.VMEM((2, 1, tq), F32)],
        compiler_params=_params("arbitrary", "arbitrary", "arbitrary"),
        name="attention",
    )(qt, k, vt)


def _rope_tables(s):
    inv_freq = np.float32(ROPE_THETA) ** (-np.arange(0, QK_ROPE, 2, dtype=np.float32) / np.float32(QK_ROPE))
    ang = np.arange(s, dtype=np.float32)[:, None] * inv_freq[None, :].astype(np.float32)
    return np.cos(ang.astype(np.float64)).astype(np.float32), np.sin(ang.astype(np.float64)).astype(np.float32)


def _prep_mla_weights(w_uq, w_ukv):
    half = QK_ROPE // 2
    scale = (QK_NOPE + QK_ROPE) ** -0.5 * math.log2(math.e)
    wqt = (w_uq * scale).T
    wkv = w_ukv.reshape(KV_LORA, MLA_HEADS, QK_NOPE + V_HEAD)
    wk = wkv[..., :QK_NOPE].reshape(KV_LORA, MLA_HEADS * QK_NOPE)
    wvt = wkv[..., QK_NOPE:].reshape(KV_LORA, MLA_HEADS * V_HEAD).T
    return wqt.astype(BF16), wk.astype(BF16), wvt.astype(BF16)


def _mla(lat, sm, g_q, g_kv, w_uq, w_ukv, b, s):
    ts = min(512, s)
    wqt, wk, wvt = _prep_mla_weights(w_uq, w_ukv)
    cos, sin = _rope_tables(s)
    zeros = np.zeros((s, LANE - QK_ROPE), np.float32)
    ck = np.concatenate([cos, cos, zeros], axis=-1)
    sk = np.concatenate([-sin, sin, zeros], axis=-1)
    qt, k, vt = _mla_up(lat, sm, g_q.reshape(1, -1), g_kv.reshape(1, -1), wqt, wk, wvt,
                        jnp.asarray(cos.T), jnp.asarray(sin.T), jnp.asarray(ck), jnp.asarray(sk), b, s, ts)
    return _attention(qt, k, vt, b, s)


HALO = 16
GATE_A = QK_ROPE
GATE_BT = QK_ROPE + 2 * GDN_HEADS
NG = 2 * GDN_HEADS


def _chunk_masks(n, rev):
    ri = lax.broadcasted_iota(jnp.int32, (n, n), 0)
    ci = lax.broadcasted_iota(jnp.int32, (n, n), 1)
    same = (ri // CHUNK) == (ci // CHUNK)
    if rev:
        return same, same & (ri <= ci), same & (ri < ci), ri == ci
    return same, same & (ri >= ci), same & (ri > ci), ri == ci


def _dot_exact(a, b):
    return jnp.dot(a, b, preferred_element_type=F32, precision=lax.Precision.HIGHEST)


def _dot_nt_exact(a, b):
    return lax.dot_general(a, b, (((1,), (1,)), ((), ())), preferred_element_type=F32,
                           precision=lax.Precision.HIGHEST)


def _softplus(x):
    return jnp.maximum(x, 0.0) + jnp.log(1.0 + jnp.exp(-jnp.abs(x)))


def _gdn_prep_kernel(qkv_ref, prev_ref, next_ref, sm_ref, smt_ref, cw_ref, alog_ref, dtb_ref,
                     alogt_ref, dtbt_ref, q_ref, k_ref, v_ref, kt_ref, gc_ref, beta_ref, gct_ref, glt_ref,
                     xs_ref, *, ts):
    j = pl.program_id(1)
    nj = pl.num_programs(1)
    xs_ref[0:HALO, :] = jnp.where(j > 0, prev_ref[...].astype(F32), 0.0)
    xs_ref[HALO:HALO + ts, :] = qkv_ref[...].astype(F32)
    xs_ref[HALO + ts:, :] = jnp.where(j < nj - 1, next_ref[...].astype(F32), 0.0)
    acc = None
    for tap in range(CONV_K):
        lo = HALO - CONV_K // 2 + tap
        term = xs_ref[lo:lo + ts, :] * cw_ref[tap:tap + 1, :]
        acc = term if acc is None else acc + term
    act = acc * jax.nn.sigmoid(acc)
    eye = (lax.broadcasted_iota(jnp.int32, (GDN_DK, GDN_DK), 0)
           == lax.broadcasted_iota(jnp.int32, (GDN_DK, GDN_DK), 1)).astype(BF16)
    hk = GDN_HEADS * GDN_DK
    for h in range(GDN_HEADS):
        qh = act[:, h * GDN_DK:(h + 1) * GDN_DK]
        kh = act[:, hk + h * GDN_DK:hk + (h + 1) * GDN_DK]
        qh = qh * (lax.rsqrt(jnp.sum(qh * qh, axis=-1, keepdims=True) + EPS) * GDN_DK ** -0.5)
        kh = (kh * lax.rsqrt(jnp.sum(kh * kh, axis=-1, keepdims=True) + EPS)).astype(BF16)
        q_ref[:, h * GDN_DK:(h + 1) * GDN_DK] = qh.astype(BF16)
        k_ref[:, h * GDN_DK:(h + 1) * GDN_DK] = kh
        kt_ref[h] = _dot_nt(eye, kh).astype(BF16)
    v_ref[...] = act[:, 2 * hk:].astype(BF16)

    sm = sm_ref[...]
    g = -jnp.exp(alog_ref[...]) * _softplus(sm[:, GATE_A:GATE_A + NG] + dtb_ref[...])
    beta_ref[...] = jax.nn.sigmoid(sm[:, GATE_BT:GATE_BT + NG])
    _, incl_f, _, _ = _chunk_masks(ts, False)
    _, incl_b, _, _ = _chunk_masks(ts, True)
    tri_f = incl_f.astype(F32)
    tri_b = incl_b.astype(F32)
    is_fwd = lax.broadcasted_iota(jnp.int32, (ts, NG), 1) < GDN_HEADS
    gc_ref[...] = jnp.where(is_fwd, _dot_exact(tri_f, g), _dot_exact(tri_b, g))
    smt = smt_ref[...]
    gt = -jnp.exp(alogt_ref[...]) * _softplus(smt[GATE_A:GATE_A + NG, :] + dtbt_ref[...])
    is_fwd_t = lax.broadcasted_iota(jnp.int32, (NG, ts), 0) < GDN_HEADS
    gct_ref[...] = jnp.where(is_fwd_t, _dot_nt_exact(gt, tri_f), _dot_nt_exact(gt, tri_b))
    same, _, _, _ = _chunk_masks(ts, False)
    glt_ref[...] = _dot_exact(gt, same.astype(F32))


def _gdn_prep(qkv, sm, smt, conv_w, a_log, dt_bias, b, s, ts):
    t = b * s
    nj = s // ts
    c = qkv.shape[1]
    hb = ts // HALO
    d = GDN_HEADS * GDN_DK
    tile = lambda n: pl.BlockSpec((ts, n), lambda bi, j: (bi * nj + j, 0))
    tile_t = lambda n: pl.BlockSpec((n, ts), lambda bi, j: (0, bi * nj + j))
    return pl.pallas_call(
        functools.partial(_gdn_prep_kernel, ts=ts),
        grid=(b, nj),
        in_specs=[tile(c),
                  pl.BlockSpec((HALO, c), lambda bi, j: (jnp.maximum((bi * nj + j) * hb - 1, 0), 0)),
                  pl.BlockSpec((HALO, c), lambda bi, j: (jnp.minimum((bi * nj + j + 1) * hb, t // HALO - 1), 0)),
                  tile(sm.shape[1]), tile_t(smt.shape[0]),
                  _resident(conv_w.shape), _resident((1, NG)), _resident((1, NG)),
                  _resident((NG, 1)), _resident((NG, 1))],
        out_specs=[tile(d), tile(d), tile(d),
                   pl.BlockSpec((GDN_HEADS, GDN_DK, ts), lambda bi, j: (0, 0, bi * nj + j)),
                   tile(NG), tile(NG), tile_t(NG), tile_t(NG)],
        out_shape=[jax.ShapeDtypeStruct((t, d), BF16)] * 3
        + [jax.ShapeDtypeStruct((GDN_HEADS, GDN_DK, t), BF16),
           jax.ShapeDtypeStruct((t, NG), F32), jax.ShapeDtypeStruct((t, NG), F32),
           jax.ShapeDtypeStruct((NG, t), F32), jax.ShapeDtypeStruct((NG, t), F32)],
        scratch_shapes=[pltpu.VMEM((ts + 2 * HALO, c), F32)],
        compiler_params=_params("arbitrary", "arbitrary"),
        name="gdn_prep",
    )(qkv, qkv, qkv, sm, smt, conv_w, a_log.reshape(1, NG), dt_bias.reshape(1, NG),
      a_log.reshape(NG, 1), dt_bias.reshape(NG, 1))


STEP = 2 * CHUNK


def _gdn_scan_kernel(*refs, revs):
    nd = len(revs)
    ins = [refs[8 * d:8 * d + 8] for d in range(nd)]
    outs = refs[8 * nd:9 * nd]
    s_ref = refs[9 * nd]

    @pl.when(pl.program_id(1) == 0)
    def _():
        s_ref[...] = jnp.zeros(s_ref.shape, F32)

    masks = [_chunk_masks(STEP, rev) for rev in revs]
    lane = lax.broadcasted_iota(jnp.int32, (1, STEP), 1)
    zeros = jnp.zeros((CHUNK, GDN_DV), F32)
    chains = [(d, h) for d in range(nd) for h in range(GDN_HEADS)]
    ids = range(len(chains))
    sl = [slice(h * GDN_DK, (h + 1) * GDN_DK) for _, h in chains]
    col = [(GDN_HEADS if revs[d] else 0) + h for d, h in chains]
    incl = [masks[d][1] for d, _ in chains]
    strict = [masks[d][2] for d, _ in chains]
    eye = masks[0][3].astype(F32)
    left = lax.broadcasted_iota(jnp.int32, (CHUNK, STEP), 1) < CHUNK
    q_ref, k_ref, v_ref, kt_ref, gc_ref, beta_ref, gct_ref, glt_ref = (
        [ins[d][i] for d, _ in chains] for i in range(8))
    q = [q_ref[c][:, sl[c]] for c in ids]
    kt = [kt_ref[c][chains[c][1]] for c in ids]
    gcol = [gc_ref[c][:, col[c]:col[c] + 1] for c in ids]
    bcol = [beta_ref[c][:, col[c]:col[c] + 1] for c in ids]
    grow = [gct_ref[c][col[c]:col[c] + 1, :] for c in ids]
    glrow = [glt_ref[c][col[c]:col[c] + 1, :] for c in ids]
    decay = [jnp.exp(jnp.where(incl[c], gcol[c] - grow[c], -jnp.inf)) for c in ids]
    egc = [jnp.exp(gcol[c]) for c in ids]
    kb = [k_ref[c][:, sl[c]].astype(F32) * bcol[c] for c in ids]
    aq = [_dot(jnp.concatenate([kb[c].astype(BF16), q[c]], axis=0), kt[c]) for c in ids]
    nil = [-jnp.where(strict[c], aq[c][:STEP] * decay[c], 0.0) for c in ids]
    intra = [(aq[c][STEP:] * decay[c]).astype(BF16) for c in ids]
    wide = lambda m: m[:CHUNK] + m[CHUNK:]
    blocks = lambda w: jnp.concatenate([jnp.where(left, w, 0.0), jnp.where(left, 0.0, w)], axis=0)
    pw = [wide(nil[c]) for c in ids]
    iw = [wide(eye) + pw[c] for c in ids]
    pw = [_dot(pw[c].astype(BF16), nil[c].astype(BF16)) for c in ids]
    for _ in range(int(math.log2(CHUNK)) - 2):
        pd = [blocks(pw[c]).astype(BF16) for c in ids]
        both = [_dot(jnp.concatenate([iw[c], pw[c]], axis=0).astype(BF16), pd[c]) for c in ids]
        iw = [iw[c] + both[c][:CHUNK] for c in ids]
        pw = [both[c][CHUNK:] for c in ids]
    iw = [iw[c] + _dot(iw[c].astype(BF16), blocks(pw[c]).astype(BF16)) for c in ids]
    inv = [blocks(iw[c]) for c in ids]
    rhs = [jnp.concatenate([(v_ref[c][:, sl[c]].astype(F32) * bcol[c]).astype(BF16),
                            (kb[c] * egc[c]).astype(BF16)], axis=1) for c in ids]
    uw = [_dot(inv[c].astype(BF16), rhs[c]) for c in ids]
    qd = [(q[c].astype(F32) * egc[c]).astype(BF16) for c in ids]
    kdt = [(kt[c].astype(F32) * jnp.exp(glrow[c] - grow[c])).astype(BF16) for c in ids]
    state = [s_ref[c] for c in ids]
    for half in range(2):
        ch = [1 - half if revs[d] else half for d, _ in chains]
        r = [slice(ch[c] * CHUNK, (ch[c] + 1) * CHUNK) for c in ids]
        ws = [_dot(jnp.concatenate([uw[c][r[c], GDN_DV:].astype(BF16), qd[c][r[c]]], axis=0),
                   state[c].astype(BF16)) for c in ids]
        v_new = [uw[c][r[c], :GDN_DV] - ws[c][:CHUNK] for c in ids]
        vpad = [jnp.concatenate([v_new[c], zeros] if ch[c] == 0 else [zeros, v_new[c]], axis=0).astype(BF16)
                for c in ids]
        res = [_dot(jnp.concatenate([intra[c][r[c]], kdt[c]], axis=0), vpad[c]) for c in ids]
        for c in ids:
            outs[chains[c][0]][r[c], sl[c]] = (ws[c][CHUNK:] + res[c][:CHUNK]).astype(outs[0].dtype)
        gl = [jnp.sum(jnp.where(lane == ch[c] * CHUNK, glrow[c], 0.0), axis=-1, keepdims=True) for c in ids]
        state = [state[c] * jnp.exp(gl[c]) + res[c][CHUNK:] for c in ids]
    for c in ids:
        s_ref[c] = state[c]


def _gdn_scan(q, k, v, kt, gc, beta, gct, glt, b, s, revs):
    t = b * s
    n = s // STEP
    d = GDN_HEADS * GDN_DK
    in_specs, operands, out_specs = [], [], []
    for rev in revs:
        blk = (lambda bi, i: bi * n + (n - 1 - i)) if rev else (lambda bi, i: bi * n + i)
        tile = lambda w, blk=blk: pl.BlockSpec((STEP, w), lambda bi, i: (blk(bi, i), 0))
        tile_t = lambda w, blk=blk: pl.BlockSpec((w, STEP), lambda bi, i: (0, blk(bi, i)))
        in_specs += [tile(d), tile(d), tile(d),
                     pl.BlockSpec((GDN_HEADS, GDN_DK, STEP), lambda bi, i, blk=blk: (0, 0, blk(bi, i))),
                     tile(NG), tile(NG), tile_t(NG), tile_t(NG)]
        operands += [q, k, v, kt, gc, beta, gct, glt]
        out_specs.append(tile(d))
    return pl.pallas_call(
        functools.partial(_gdn_scan_kernel, revs=tuple(revs)),
        grid=(b, n),
        in_specs=in_specs,
        out_specs=out_specs,
        out_shape=[jax.ShapeDtypeStruct((t, d), BF16)] * len(revs),
        scratch_shapes=[pltpu.VMEM((len(revs) * GDN_HEADS, GDN_DK, GDN_DV), F32)],
        compiler_params=_params("arbitrary", "arbitrary"),
        name="gdn_scan",
    )(*operands)


def _gdn(qkv, sm, smt, conv_w, a_log, dt_bias, b, s):
    ts = min(256, s)
    q, k, v, kt, gc, beta, gct, glt = _gdn_prep(qkv, sm, smt, conv_w, a_log, dt_bias, b, s, ts)
    o_f, o_b = _gdn_scan(q, k, v, kt, gc, beta, gct, glt, b, s, (False, True))
    return o_f, o_b


def _outproj_kernel(x_ref, att_ref, of_ref, ob_ref, z_ref, ga_ref, gg_ref, wo_ref, nf_ref, wr_ref, br_ref,
                    y_ref, yn_ref, lg_ref):
    att = _rms(att_ref[...].astype(F32), ga_ref[...]).astype(BF16)
    o = of_ref[...].astype(F32) + ob_ref[...].astype(F32)
    z = z_ref[...].astype(F32)
    gg = gg_ref[...]
    lin = []
    for h in range(GDN_HEADS):
        sl = slice(h * GDN_DV, (h + 1) * GDN_DV)
        zh = z[:, sl]
        lin.append((_rms(o[:, sl], gg) * (zh * jax.nn.sigmoid(zh))).astype(BF16))
    lin = jnp.concatenate(lin, axis=-1)
    y = x_ref[...] + _dot(att, wo_ref[:D_ATTN, :]) + _dot(lin, wo_ref[D_ATTN:, :])
    y_ref[...] = y
    yn = _rms(y, nf_ref[...]).astype(BF16)
    _store_rows(yn_ref, _pack_halves(yn))
    lg_ref[...] = _dot(yn, wr_ref[...]) + br_ref[...]


def _outproj(x2d, att, o_f, o_b, z, g_attn, g_gdn, w_out, norm_ffn, w_r, b_r, tm):
    t, d = x2d.shape
    row = lambda n: pl.BlockSpec((tm, n), lambda i: (i, 0))
    return pl.pallas_call(
        _outproj_kernel,
        grid=(t // tm,),
        in_specs=[row(d), row(D_ATTN), row(D_GDN), row(D_GDN), row(D_GDN), _resident((1, D_ATTN)),
                  _resident((1, GDN_DV)), _resident(w_out.shape), _resident((1, d)), _resident(w_r.shape),
                  _resident((1, LANE))],
        out_specs=[row(d), pl.BlockSpec((tm * ROW_SUB, LANE), lambda i: (i, 0)), row(LANE)],
        out_shape=[jax.ShapeDtypeStruct((t, d), F32), jax.ShapeDtypeStruct((t * ROW_SUB, LANE), jnp.uint32),
                   jax.ShapeDtypeStruct((t, LANE), F32)],
        compiler_params=_params("arbitrary"),
        name="outproj",
    )(x2d, att, o_f, o_b, z, g_attn.reshape(1, -1), g_gdn.reshape(1, -1), w_out, norm_ffn.reshape(1, -1),
      w_r, b_r)


SLOT_ROWS = 8


def _router_kernel(lg_ref, gate_ref, slot_ref, cnt_ref, carry_ref):
    @pl.when(pl.program_id(0) == 0)
    def _():
        carry_ref[...] = jnp.zeros(carry_ref.shape, F32)

    lg = lg_ref[...]
    lane = lax.broadcasted_iota(jnp.int32, lg.shape, 1)
    neg = -jnp.inf

    def first_max(x):
        mx = jnp.max(x, axis=-1, keepdims=True)
        return mx, jnp.min(jnp.where(x == mx, lane, LANE), axis=-1, keepdims=True)

    gl = jnp.where(lane < N_GROUPS, lg, neg)
    gmax, grp = first_max(gl)
    grp_p = 1.0 / jnp.sum(jnp.exp(gl - gmax), axis=-1, keepdims=True)
    lo = N_GROUPS + grp * EXPERTS_PER_GROUP
    el = jnp.where((lane >= lo) & (lane < lo + EXPERTS_PER_GROUP), lg, neg)
    l1, i1 = first_max(el)
    l2, i2 = first_max(jnp.where(lane == i1, neg, el))
    e = jnp.exp(l2 - l1)
    g1 = grp_p / (1.0 + e)
    e1, e2 = i1 - N_GROUPS, i2 - N_GROUPS
    gate_ref[...] = jnp.where(lane == 0, g1, jnp.where(lane == 1, g1 * e, 0.0))
    tm = lg.shape[0]
    oh1, oh2 = lane == e1, lane == e2
    before = (lax.broadcasted_iota(jnp.int32, (tm, tm), 1)
              < lax.broadcasted_iota(jnp.int32, (tm, tm), 0)).astype(BF16)
    r1 = _dot(before, oh1.astype(BF16))
    r2 = _dot(before, oh2.astype(BF16))
    c1 = jnp.sum(oh1.astype(F32), axis=0, keepdims=True)
    c2 = jnp.sum(oh2.astype(F32), axis=0, keepdims=True)
    carry = carry_ref[...]
    rank1 = jnp.sum(jnp.where(oh1, r1 + carry, 0.0), axis=-1, keepdims=True)
    rank2 = jnp.sum(jnp.where(oh2, r2 + (carry + c1), 0.0), axis=-1, keepdims=True)
    info = jnp.where(lane == 0, e1.astype(F32), jnp.where(lane == 1, e2.astype(F32),
                     jnp.where(lane == 2, rank1, jnp.where(lane == 3, rank2, 0.0))))
    slot_ref[...] = info.T[:SLOT_ROWS, :].astype(jnp.int32)
    carry_ref[...] = carry + c1 + c2
    cnt_ref[...] = (carry + c1 + c2).astype(jnp.int32)


def _router(logits, tm):
    t = logits.shape[0]
    row = pl.BlockSpec((tm, LANE), lambda i: (i, 0))
    return pl.pallas_call(
        _router_kernel,
        grid=(t // tm,),
        in_specs=[row],
        out_specs=[row, pl.BlockSpec((SLOT_ROWS, tm), lambda i: (0, i)), pl.BlockSpec((1, LANE), lambda i: (0, 0))],
        out_shape=[jax.ShapeDtypeStruct((t, LANE), F32), jax.ShapeDtypeStruct((SLOT_ROWS, t), jnp.int32),
                   jax.ShapeDtypeStruct((1, LANE), jnp.int32)],
        scratch_shapes=[pltpu.VMEM((1, LANE), F32)],
        compiler_params=_params("arbitrary"),
        name="router",
    )(logits)


ROW_UNROLL = 8


def _start_rows(tm, copy):
    def start(t, c):
        for k in range(TOP_K):
            copy(t, k).start()
        return c

    lax.fori_loop(0, tm, start, 0, unroll=ROW_UNROLL)


def _wait_rows(tm, copy):
    def wait(t, c):
        for k in range(TOP_K):
            copy(t, k).wait()
        return c

    lax.fori_loop(0, tm, wait, 0, unroll=ROW_UNROLL)


def _row_copies(tm, copy):
    _start_rows(tm, copy)
    _wait_rows(tm, copy)


def _dispatch_kernel(*refs):
    row, yn_ref, (xb_ref, sem) = _slot_row(refs[:N_SLOT_REFS]), refs[N_SLOT_REFS], refs[-2:]
    _row_copies(yn_ref.shape[0] // ROW_SUB, lambda t, k: pltpu.make_async_copy(
        yn_ref.at[_row_tile(t)], xb_ref.at[_row_tile(row(t, k))], sem))


N_SLOT_REFS = 2 * TOP_K + 1


def _slot_specs(tm, step=lambda i: i):
    tile = pl.BlockSpec((tm,), lambda i: (step(i),), memory_space=pltpu.SMEM)
    return [tile] * (2 * TOP_K) + [pl.BlockSpec(memory_space=pltpu.SMEM)]


def _slot_row(refs):
    eid, rank, pstart = refs[:TOP_K], refs[TOP_K:2 * TOP_K], refs[2 * TOP_K]
    return lambda t, k: pstart[eid[k][t]] + rank[k][t]


def _dispatch(slots, yn, xb, tm, n_rows):
    t = yn.shape[0] // ROW_SUB
    prev = [] if xb is None else [xb]
    return pl.pallas_call(
        _dispatch_kernel,
        grid=(t // tm,),
        in_specs=_slot_specs(tm) + [pl.BlockSpec((tm * ROW_SUB, LANE), lambda i: (i, 0))]
        + [pl.BlockSpec(memory_space=pl.ANY) for _ in prev],
        out_specs=pl.BlockSpec(memory_space=pl.ANY),
        out_shape=jax.ShapeDtypeStruct((n_rows * ROW_SUB, LANE), jnp.uint32),
        scratch_shapes=[pltpu.SemaphoreType.DMA],
        input_output_aliases={N_SLOT_REFS + 1: 0} if prev else {},
        compiler_params=_params("arbitrary", has_side_effects=True),
        name="moe_dispatch",
    )(*slots, yn, *prev)


MOE_BLOCK = 512


def _moe_kernel(be_ref, nu_ref, nv_ref, xb_ref, wg_ref, wu_ref, wd_ref, yb_ref, wg_bf, wu_bf, wd_bf):
    i = pl.program_id(0)

    @pl.when(i < nu_ref[0])
    def _():
        @pl.when((i == 0) | (be_ref[i] != be_ref[jnp.maximum(i - 1, 0)]))
        def _():
            wg_bf[...] = wg_ref[0].astype(BF16)
            wu_bf[...] = wu_ref[0].astype(BF16)
            wd_bf[...] = wd_ref[0].astype(BF16)

        x = _load_rows(xb_ref)
        valid = lax.broadcasted_iota(jnp.int32, (x.shape[0], 1), 0) < nv_ref[i]
        lo, hi = _unpack_halves(jnp.where(valid, x, jnp.uint32(0)))
        lo, hi = lo.astype(BF16), hi.astype(BF16)
        w = lo.shape[1]
        a = _dot(lo, wg_bf[:w, :]) + _dot(hi, wg_bf[w:, :])
        u = _dot(lo, wu_bf[:w, :]) + _dot(hi, wu_bf[w:, :])
        hdn = (a * jax.nn.sigmoid(a) * u).astype(BF16)
        _store_rows(yb_ref, _pack_halves(_dot(hdn, wd_bf[...]).astype(BF16)))


def _moe_blocks(block_e, n_used, n_valid, xb, w_gate, w_up, w_down):
    p = xb.shape[0] // ROW_SUB
    d = w_gate.shape[1]
    nb = p // MOE_BLOCK
    clamp = lambda i, nu: jnp.minimum(i, nu[0] - 1)
    rows = pl.BlockSpec((MOE_BLOCK * ROW_SUB, LANE), lambda i, be, nu, nv: (clamp(i, nu), 0))
    grid_spec = pltpu.PrefetchScalarGridSpec(
        num_scalar_prefetch=3,
        grid=(nb,),
        in_specs=[rows,
                  pl.BlockSpec((1, d, D_EXPERT), lambda i, be, nu, nv: (be[clamp(i, nu)], 0, 0)),
                  pl.BlockSpec((1, d, D_EXPERT), lambda i, be, nu, nv: (be[clamp(i, nu)], 0, 0)),
                  pl.BlockSpec((1, D_EXPERT, d), lambda i, be, nu, nv: (be[clamp(i, nu)], 0, 0))],
        out_specs=rows,
        scratch_shapes=[pltpu.VMEM((d, D_EXPERT), BF16), pltpu.VMEM((d, D_EXPERT), BF16),
                        pltpu.VMEM((D_EXPERT, d), BF16)],
    )
    return pl.pallas_call(
        _moe_kernel,
        grid_spec=grid_spec,
        out_shape=jax.ShapeDtypeStruct(xb.shape, jnp.uint32),
        compiler_params=_params("arbitrary"),
        name="moe_experts",
    )(block_e, n_used, n_valid, xb, w_gate, w_up, w_down)


def _combine_kernel(*refs):
    row, next_row = _slot_row(refs[:N_SLOT_REFS]), _slot_row(refs[N_SLOT_REFS:2 * N_SLOT_REFS])
    y_ref, gate_ref, g_ref, yb_ref, o_ref, buf, sems = refs[2 * N_SLOT_REFS:]
    i, n, tm = pl.program_id(0), pl.num_programs(0), y_ref.shape[0]
    slot = i % 2

    def gather(row, slot):
        return lambda t, k: pltpu.make_async_copy(
            yb_ref.at[_row_tile(row(t, k))], buf.at[slot, k, _row_tile(t)], sems.at[slot])

    @pl.when(i == 0)
    def _():
        _start_rows(tm, gather(row, 0))

    @pl.when(i + 1 < n)
    def _():
        _start_rows(tm, gather(next_row, 1 - slot))

    _wait_rows(tm, gather(row, slot))
    gate = gate_ref[...]
    acc_lo = acc_hi = None
    for k in range(TOP_K):
        lo, hi = _unpack_halves(_load_rows(buf.at[slot, k]))
        gk = gate[:, k:k + 1]
        acc_lo = gk * lo if acc_lo is None else acc_lo + gk * lo
        acc_hi = gk * hi if acc_hi is None else acc_hi + gk * hi
    y = y_ref[...] + jnp.concatenate([acc_lo, acc_hi], axis=-1)
    o_ref[...] = _rms(y, g_ref[...])


def _combine(slots, y, gate, g, yb, tm):
    t, d = y.shape
    n = t // tm
    return pl.pallas_call(
        _combine_kernel,
        grid=(n,),
        in_specs=_slot_specs(tm) + _slot_specs(tm, lambda i: jnp.minimum(i + 1, n - 1))
        + [pl.BlockSpec((tm, d), lambda i: (i, 0)),
                                    pl.BlockSpec((tm, LANE), lambda i: (i, 0)),
                                    _resident((1, d)), pl.BlockSpec(memory_space=pl.ANY)],
        out_specs=pl.BlockSpec((tm, d), lambda i: (i, 0)),
        out_shape=jax.ShapeDtypeStruct((t, d), F32),
        scratch_shapes=[pltpu.VMEM((2, TOP_K, tm * ROW_SUB, LANE), jnp.uint32), pltpu.SemaphoreType.DMA((2,))],
        compiler_params=_params("arbitrary"),
        name="moe_combine",
    )(*slots, *slots, y, gate, g.reshape(1, -1), yb)


def _route_plan(counts, n_slots):
    padded = (counts + MOE_BLOCK - 1) // MOE_BLOCK * MOE_BLOCK
    pend = jnp.cumsum(padded)
    pstart = pend - padded
    n_blocks = -(-n_slots // MOE_BLOCK) + N_EXPERTS
    first_row = jnp.arange(n_blocks, dtype=jnp.int32) * MOE_BLOCK
    block_e = jnp.minimum(jnp.sum(pend[None, :] <= first_row[:, None], axis=1), N_EXPERTS - 1).astype(jnp.int32)
    n_used = (pend[-1] // MOE_BLOCK).astype(jnp.int32).reshape(1)
    n_valid = jnp.clip(jnp.take(pstart + counts, block_e) - first_row, 0, MOE_BLOCK).astype(jnp.int32)
    return pstart.astype(jnp.int32), block_e, n_used, n_valid, n_blocks


def _encoder_front(x, p, wts):
    b, s, d = x.shape
    x2d = x.reshape(b * s, d)
    tm = min(512, b * s)
    lat, sm, smt, qkv, z = _inproj(x2d, p["norm_mix"].reshape(1, -1), wts["w_lat"], wts["w_sm"], wts["w_smt"],
                                   wts["w_qkv"], wts["w_z"], tm)
    att = _mla(lat, sm, p["g_q_lora"], p["g_kv_lora"], p["w_uq"], p["w_ukv"], b, s)
    o_f, o_b = _gdn(qkv, sm, smt, p["conv_w"], p["a_log"], p["dt_bias"], b, s)
    return _outproj(x2d, att, o_f, o_b, z, p["g_attn_out"], p["g_gdn_out"], wts["w_out"], p["norm_ffn"],
                    wts["w_r"], wts["b_r"], tm)


def _prep_weights(p):
    w_in = p["w_in"]
    o = np.cumsum([0, Q_LORA, KV_LORA, QK_ROPE, GDN_QKV, D_GDN, NG, NG])
    d = w_in.shape[0]
    half = QK_ROPE // 2
    w_sm = jnp.concatenate([w_in[:, o[2]:o[3]], w_in[:, o[5]:o[7]], jnp.zeros((d, LANE - QK_ROPE - 2 * NG), F32),
                            w_in[:, o[2] + half:o[3]], w_in[:, o[2]:o[2] + half],
                            jnp.zeros((d, LANE - QK_ROPE), F32)], axis=1).astype(BF16)
    w_r = jnp.concatenate([p["w_router_group"], p["w_router_expert"],
                           jnp.zeros((d, LANE - N_GROUPS - N_EXPERTS), F32)], axis=1).astype(BF16)
    b_r = jnp.concatenate([p["b_router_group"], p["b_router_expert"],
                           jnp.zeros((LANE - N_GROUPS - N_EXPERTS,), F32)]).reshape(1, LANE)
    return dict(w_lat=w_in[:, :o[2]].astype(BF16), w_sm=w_sm, w_smt=w_sm[:, :LANE].T, w_qkv=w_in[:, o[3]:o[4]].astype(BF16),
                w_z=w_in[:, o[4]:o[5]].astype(BF16), w_out=p["w_out"].astype(BF16), w_r=w_r, b_r=b_r)


ROW_TILE = 512


def _encode(xs, p):
    wts = _prep_weights(p)
    fronts = [_encoder_front(x, p, wts) for x in xs]
    logits = jnp.concatenate([f[2] for f in fronts], axis=0)
    t = logits.shape[0]
    gate, slots, counts = _router(logits, min(ROW_TILE, t))
    pstart, block_e, n_used, n_valid, n_blocks = _route_plan(counts[0, :N_EXPERTS], t * TOP_K)
    xb = None
    bounds = np.cumsum([0] + [f[0].shape[0] for f in fronts])
    for f, t0, t1 in zip(fronts, bounds[:-1], bounds[1:]):
        xb = _dispatch([slots[r, t0:t1] for r in range(2 * TOP_K)] + [pstart], f[1], xb, min(ROW_TILE, t1 - t0), n_blocks * MOE_BLOCK)
    yb = _moe_blocks(block_e, n_used, n_valid, xb, p["w_gate"], p["w_up"], p["w_down"])
    outs = []
    for x, f, t0, t1 in zip(xs, fronts, bounds[:-1], bounds[1:]):
        out = _combine([slots[r, t0:t1] for r in range(2 * TOP_K)] + [pstart], f[0], gate[t0:t1], p["norm_final"], yb, min(ROW_TILE, t1 - t0))
        outs.append(out.reshape(x.shape))
    return outs


def kernel(x_prompt, x_sample, norm_mix, w_in, g_q_lora, g_kv_lora, w_uq, w_ukv, g_attn_out, conv_w, a_log,
           dt_bias, g_gdn_out, w_out, norm_ffn, w_router_group, b_router_group, w_router_expert,
           b_router_expert, w_gate, w_up, w_down, norm_final):
    p = dict(norm_mix=norm_mix[0], w_in=w_in[0], g_q_lora=g_q_lora[0], g_kv_lora=g_kv_lora[0], w_uq=w_uq[0],
             w_ukv=w_ukv[0], g_attn_out=g_attn_out[0], conv_w=conv_w[0], a_log=a_log[0], dt_bias=dt_bias[0],
             g_gdn_out=g_gdn_out[0], w_out=w_out[0], norm_ffn=norm_ffn[0], w_router_group=w_router_group[0],
             b_router_group=b_router_group[0], w_router_expert=w_router_expert[0],
             b_router_expert=b_router_expert[0], w_gate=w_gate[0], w_up=w_up[0], w_down=w_down[0],
             norm_final=norm_final)
    y_prompt, y_sample = _encode([x_prompt, x_sample], p)
    return (y_prompt, y_sample)
```

```python
import functools
import math

import jax
import jax.numpy as jnp
import numpy as np
from jax import lax
from jax.experimental import pallas as pl
from jax.experimental.pallas import tpu as pltpu

F32 = jnp.float32
BF16 = jnp.bfloat16

D_MODEL = 2048
MLA_HEADS = 8
Q_LORA = 512
KV_LORA = 512
QK_NOPE = 128
QK_ROPE = 64
V_HEAD = 128
ROPE_THETA = 10000.0
GDN_HEADS = 8
GDN_DK = 128
GDN_DV = 128
GDN_QKV = GDN_HEADS * (2 * GDN_DK + GDN_DV)
CONV_K = 5
CHUNK = 64
D_ATTN = MLA_HEADS * V_HEAD
D_GDN = GDN_HEADS * GDN_DV
N_GROUPS = 8
EXPERTS_PER_GROUP = 8
N_EXPERTS = N_GROUPS * EXPERTS_PER_GROUP
TOP_K = 2
D_EXPERT = 512
EPS = 1e-6

LANE = 128
QK_PAD = 256
VMEM_LIMIT = 56 * 1024 * 1024
TOKEN_TILE = 512
GDN_PREP_TILE = 256


def _params(*sem, **kw):
    return pltpu.CompilerParams(dimension_semantics=sem, vmem_limit_bytes=VMEM_LIMIT, **kw)


def _resident(shape):
    return pl.BlockSpec(shape, lambda *_: (0,) * len(shape), pipeline_mode=pl.Buffered(1))


def _rms(x, g):
    return x * lax.rsqrt(jnp.mean(x * x, axis=-1, keepdims=True) + EPS) * g


def _dot(a, b):
    return jnp.dot(a, b, preferred_element_type=F32)


def _dot_nt(a, b):
    return lax.dot_general(a, b, (((1,), (1,)), ((), ())), preferred_element_type=F32)


def _pack_halves(x):
    w = x.shape[1] // 2
    lo = pltpu.bitcast(x[:, :w].astype(F32), jnp.uint32)
    hi = pltpu.bitcast(x[:, w:].astype(F32), jnp.uint32)
    return (hi & jnp.uint32(0xFFFF0000)) | (lo >> 16)


ROW_SUB = 8


def _row_tile(r):
    return pl.ds(pl.multiple_of(r * ROW_SUB, ROW_SUB), ROW_SUB)


def _store_rows(ref, u):
    n = u.shape[0]
    for s in range(ROW_SUB):
        ref[pl.ds(s, n, stride=ROW_SUB), :] = u[:, s * LANE:(s + 1) * LANE]


def _load_rows(ref):
    n = ref.shape[0] // ROW_SUB
    return jnp.concatenate([ref[pl.ds(s, n, stride=ROW_SUB), :] for s in range(ROW_SUB)], axis=1)


def _unpack_halves(u):
    lo = pltpu.bitcast(u << 16, F32)
    hi = pltpu.bitcast(u & jnp.uint32(0xFFFF0000), F32)
    return lo, hi


def _inproj_kernel(x_ref, g_ref, w_lat_ref, w_sm_ref, w_smt_ref, w_qkv_ref, w_z_ref,
                   lat_ref, sm_ref, smt_ref, qkv_ref, z_ref):
    xn = _rms(x_ref[...], g_ref[...]).astype(BF16)
    lat_ref[...] = _dot(xn, w_lat_ref[...]).astype(BF16)
    sm_ref[...] = _dot(xn, w_sm_ref[...])
    smt_ref[...] = _dot_nt(w_smt_ref[...], xn)
    qkv_ref[...] = _dot(xn, w_qkv_ref[...]).astype(BF16)
    z_ref[...] = _dot(xn, w_z_ref[...]).astype(BF16)


def _inproj(x2d, g, w_lat, w_sm, w_smt, w_qkv, w_z, tm):
    t, d = x2d.shape
    n_lat, n_sm, n_qkv, n_z = w_lat.shape[1], w_sm.shape[1], w_qkv.shape[1], w_z.shape[1]
    n_smt = w_smt.shape[0]
    row = lambda n: pl.BlockSpec((tm, n), lambda i: (i, 0))
    return pl.pallas_call(
        _inproj_kernel,
        grid=(t // tm,),
        in_specs=[row(d), _resident((1, d)), _resident(w_lat.shape), _resident(w_sm.shape),
                  _resident(w_smt.shape), _resident(w_qkv.shape), _resident(w_z.shape)],
        out_specs=[row(n_lat), row(n_sm), pl.BlockSpec((n_smt, tm), lambda i: (0, i)), row(n_qkv), row(n_z)],
        out_shape=[jax.ShapeDtypeStruct((t, n_lat), BF16), jax.ShapeDtypeStruct((t, n_sm), F32),
                   jax.ShapeDtypeStruct((n_smt, t), F32), jax.ShapeDtypeStruct((t, n_qkv), BF16),
                   jax.ShapeDtypeStruct((t, n_z), BF16)],
        compiler_params=_params("arbitrary"),
        name="inproj",
    )(x2d, g, w_lat, w_sm, w_smt, w_qkv, w_z)


def _mla_up_kernel(lat_ref, sm_ref, gq_ref, gkv_ref, wqt_ref, wk_ref, wvt_ref,
                   cos_t_ref, sin_t_ref, ck_ref, sk_ref, qt_ref, k_ref, vt_ref):
    lat = lat_ref[...].astype(F32)
    cqn = _rms(lat[:, :Q_LORA], gq_ref[...]).astype(BF16)
    ckvn = _rms(lat[:, Q_LORA:], gkv_ref[...]).astype(BF16)
    half = QK_ROPE // 2
    dqk = QK_NOPE + QK_ROPE
    qt = _dot_nt(wqt_ref[...], cqn)
    cos_t, sin_t = cos_t_ref[...], sin_t_ref[...]
    zero = jnp.zeros((QK_PAD - dqk, qt.shape[1]), BF16)
    for h in range(MLA_HEADS):
        lo = h * dqk
        x1 = qt[lo + QK_NOPE:lo + QK_NOPE + half, :]
        x2 = qt[lo + QK_NOPE + half:lo + dqk, :]
        qt_ref[0, h, 0, 0:QK_NOPE, :] = qt[lo:lo + QK_NOPE, :].astype(BF16)
        qt_ref[0, h, 0, QK_NOPE:QK_NOPE + half, :] = (x1 * cos_t - x2 * sin_t).astype(BF16)
        qt_ref[0, h, 0, QK_NOPE + half:dqk, :] = (x2 * cos_t + x1 * sin_t).astype(BF16)
        qt_ref[0, h, 0, dqk:, :] = zero
    kn = _dot(ckvn, wk_ref[...])
    sm = sm_ref[...]
    pe = (sm[:, :LANE] * ck_ref[...] + sm[:, LANE:] * sk_ref[...]).astype(BF16)
    for h in range(MLA_HEADS):
        k_ref[:, h * QK_PAD:h * QK_PAD + QK_NOPE] = kn[:, h * QK_NOPE:(h + 1) * QK_NOPE].astype(BF16)
        k_ref[:, h * QK_PAD + QK_NOPE:(h + 1) * QK_PAD] = pe
    vt = _dot_nt(wvt_ref[...], ckvn)
    for h in range(MLA_HEADS):
        vt_ref[0, h, 0, 0:V_HEAD, :] = vt[h * V_HEAD:(h + 1) * V_HEAD, :].astype(BF16)
        vt_ref[0, h, 0, V_HEAD:, :] = jnp.ones((V_ROWS - V_HEAD, vt.shape[1]), BF16)


def _mla_up(lat, sm, gq, gkv, wqt, wk, wvt, cos_t, sin_t, ck, sk, b, s, ts):
    t = b * s
    nj = s // ts
    hk = MLA_HEADS * QK_PAD
    return pl.pallas_call(
        _mla_up_kernel,
        grid=(b, nj),
        in_specs=[pl.BlockSpec((ts, lat.shape[1]), lambda bi, j: (bi * nj + j, 0)),
                  pl.BlockSpec((ts, sm.shape[1]), lambda bi, j: (bi * nj + j, 0)),
                  _resident(gq.shape), _resident(gkv.shape), _resident(wqt.shape), _resident(wk.shape),
                  _resident(wvt.shape),
                  pl.BlockSpec((QK_ROPE // 2, ts), lambda bi, j: (0, j)),
                  pl.BlockSpec((QK_ROPE // 2, ts), lambda bi, j: (0, j)),
                  pl.BlockSpec((ts, LANE), lambda bi, j: (j, 0)),
                  pl.BlockSpec((ts, LANE), lambda bi, j: (j, 0))],
        out_specs=[pl.BlockSpec((1, MLA_HEADS, 1, QK_PAD, ts), lambda bi, j: (bi, 0, j, 0, 0)),
                   pl.BlockSpec((ts, hk), lambda bi, j: (bi * nj + j, 0)),
                   pl.BlockSpec((1, MLA_HEADS, 1, V_ROWS, ts), lambda bi, j: (bi, 0, j, 0, 0))],
        out_shape=[jax.ShapeDtypeStruct((b, MLA_HEADS, nj, QK_PAD, ts), BF16),
                   jax.ShapeDtypeStruct((t, hk), BF16),
                   jax.ShapeDtypeStruct((b, MLA_HEADS, nj, V_ROWS, ts), BF16)],
        compiler_params=_params("arbitrary", "arbitrary"),
        name="mla_up",
    )(lat, sm, gq, gkv, wqt, wk, wvt, cos_t, sin_t, ck, sk)


ATTN_UNROLL = 8
ATTN_Q_ROWS = 1024
ATTN_SHORT_SEQ = 2048
V_ROWS = V_HEAD + 16


def _attn_kernel(qt_ref, k_ref, vt_ref, o_ref, m_ref, acc_ref, s_buf, c_buf, *, nk, tk, unroll):
    qt = jnp.concatenate([qt_ref[0, 0, i] for i in range(qt_ref.shape[2])], axis=1)
    m_ref[...] = jnp.full(m_ref.shape, -jnp.inf, F32)
    acc_ref[...] = jnp.zeros(acc_ref.shape, F32)

    def scores(j, buf):
        s = _dot(k_ref[pl.ds(pl.multiple_of(j * tk, tk), tk), :], qt)
        s_buf[buf] = s
        c_buf[buf] = jnp.max(s, axis=0, keepdims=True)

    scores(0, 0)

    def step(j, cur, nxt):
        scores(jnp.minimum(j + 1, nk - 1), nxt)
        m_old = m_ref[...]
        m_new = jnp.maximum(m_old, c_buf[cur])
        alpha = jnp.exp2(m_old - m_new)
        p = jnp.exp2(s_buf[cur] - m_new).astype(BF16)
        acc_ref[...] = alpha * acc_ref[...] + _dot(vt_ref[0, 0, j], p)
        m_ref[...] = m_new

    def body(jj, carry):
        for u in range(unroll):
            step(unroll * jj + u, u % 2, 1 - u % 2)
        return carry

    lax.fori_loop(0, nk // unroll, body, 0)
    acc = acc_ref[...]
    o_ref[...] = (acc[:V_HEAD] / acc[V_HEAD:V_HEAD + 1]).T.astype(o_ref.dtype)


def _attention(qt, k, vt, b, s):
    nk, tk = vt.shape[2], vt.shape[4]
    qtiles = math.gcd(qt.shape[2], max(1, (s if s <= ATTN_SHORT_SEQ else ATTN_Q_ROWS) // qt.shape[4]))
    nq, tq = qt.shape[2] // qtiles, qt.shape[4] * qtiles
    assert nk % 2 == 0, "the score buffers alternate statically"
    unroll = math.gcd(nk, ATTN_UNROLL)
    return pl.pallas_call(
        functools.partial(_attn_kernel, nk=nk, tk=tk, unroll=unroll),
        grid=(b, MLA_HEADS, nq),
        in_specs=[pl.BlockSpec((1, 1, qtiles, QK_PAD, tq // qtiles), lambda bi, h, i: (bi, h, i, 0, 0)),
                  pl.BlockSpec((s, QK_PAD), lambda bi, h, i: (bi, h)),
                  pl.BlockSpec((1, 1, nk, V_ROWS, tk), lambda bi, h, i: (bi, h, 0, 0, 0))],
        out_specs=pl.BlockSpec((tq, V_HEAD), lambda bi, h, i: (bi * nq + i, h)),
        out_shape=jax.ShapeDtypeStruct((b * s, D_ATTN), BF16),
        scratch_shapes=[pltpu.VMEM((1, tq), F32), pltpu.VMEM((V_ROWS, tq), F32),
                        pltpu.VMEM((2, tk, tq), F32), pltpu.VMEM((2, 1, tq), F32)],
        compiler_params=_params("arbitrary", "arbitrary", "arbitrary"),
        name="attention",
    )(qt, k, vt)


def _rope_tables(s):
    inv_freq = np.float32(ROPE_THETA) ** (-np.arange(0, QK_ROPE, 2, dtype=np.float32) / np.float32(QK_ROPE))
    ang = np.arange(s, dtype=np.float32)[:, None] * inv_freq[None, :].astype(np.float32)
    return np.cos(ang.astype(np.float64)).astype(np.float32), np.sin(ang.astype(np.float64)).astype(np.float32)


def _prep_mla_weights(w_uq, w_ukv):
    scale = (QK_NOPE + QK_ROPE) ** -0.5 * math.log2(math.e)
    wqt = (w_uq * scale).T
    wkv = w_ukv.reshape(KV_LORA, MLA_HEADS, QK_NOPE + V_HEAD)
    wk = wkv[..., :QK_NOPE].reshape(KV_LORA, MLA_HEADS * QK_NOPE)
    wvt = wkv[..., QK_NOPE:].reshape(KV_LORA, MLA_HEADS * V_HEAD).T
    return wqt.astype(BF16), wk.astype(BF16), wvt.astype(BF16)


def _mla(lat, sm, g_q, g_kv, w_uq, w_ukv, b, s):
    ts = min(TOKEN_TILE, s)
    wqt, wk, wvt = _prep_mla_weights(w_uq, w_ukv)
    cos, sin = _rope_tables(s)
    zeros = np.zeros((s, LANE - QK_ROPE), np.float32)
    ck = np.concatenate([cos, cos, zeros], axis=-1)
    sk = np.concatenate([-sin, sin, zeros], axis=-1)
    qt, k, vt = _mla_up(lat, sm, g_q.reshape(1, -1), g_kv.reshape(1, -1), wqt, wk, wvt,
                        jnp.asarray(cos.T), jnp.asarray(sin.T), jnp.asarray(ck), jnp.asarray(sk), b, s, ts)
    return _attention(qt, k, vt, b, s)


HALO = 16
GATE_A = QK_ROPE
GATE_BT = QK_ROPE + 2 * GDN_HEADS
NG = 2 * GDN_HEADS


def _chunk_masks(n, rev):
    ri = lax.broadcasted_iota(jnp.int32, (n, n), 0)
    ci = lax.broadcasted_iota(jnp.int32, (n, n), 1)
    same = (ri // CHUNK) == (ci // CHUNK)
    if rev:
        return same, same & (ri <= ci), same & (ri < ci), ri == ci
    return same, same & (ri >= ci), same & (ri > ci), ri == ci


def _dot_exact(a, b):
    return jnp.dot(a, b, preferred_element_type=F32, precision=lax.Precision.HIGHEST)


def _dot_nt_exact(a, b):
    return lax.dot_general(a, b, (((1,), (1,)), ((), ())), preferred_element_type=F32,
                           precision=lax.Precision.HIGHEST)


def _softplus(x):
    return jnp.maximum(x, 0.0) + jnp.log(1.0 + jnp.exp(-jnp.abs(x)))


def _gdn_prep_kernel(qkv_ref, prev_ref, next_ref, sm_ref, smt_ref, cw_ref, alog_ref, dtb_ref,
                     alogt_ref, dtbt_ref, q_ref, k_ref, v_ref, kt_ref, gc_ref, beta_ref, gct_ref, glt_ref,
                     xs_ref, *, ts):
    j = pl.program_id(1)
    nj = pl.num_programs(1)
    xs_ref[0:HALO, :] = jnp.where(j > 0, prev_ref[...].astype(F32), 0.0)
    xs_ref[HALO:HALO + ts, :] = qkv_ref[...].astype(F32)
    xs_ref[HALO + ts:, :] = jnp.where(j < nj - 1, next_ref[...].astype(F32), 0.0)
    acc = None
    for tap in range(CONV_K):
        lo = HALO - CONV_K // 2 + tap
        term = xs_ref[lo:lo + ts, :] * cw_ref[tap:tap + 1, :]
        acc = term if acc is None else acc + term
    act = acc * jax.nn.sigmoid(acc)
    eye = (lax.broadcasted_iota(jnp.int32, (GDN_DK, GDN_DK), 0)
           == lax.broadcasted_iota(jnp.int32, (GDN_DK, GDN_DK), 1)).astype(BF16)
    hk = GDN_HEADS * GDN_DK
    for h in range(GDN_HEADS):
        qh = act[:, h * GDN_DK:(h + 1) * GDN_DK]
        kh = act[:, hk + h * GDN_DK:hk + (h + 1) * GDN_DK]
        qh = qh * (lax.rsqrt(jnp.sum(qh * qh, axis=-1, keepdims=True) + EPS) * GDN_DK ** -0.5)
        kh = (kh * lax.rsqrt(jnp.sum(kh * kh, axis=-1, keepdims=True) + EPS)).astype(BF16)
        q_ref[:, h * GDN_DK:(h + 1) * GDN_DK] = qh.astype(BF16)
        k_ref[:, h * GDN_DK:(h + 1) * GDN_DK] = kh
        kt_ref[h] = _dot_nt(eye, kh).astype(BF16)
    v_ref[...] = act[:, 2 * hk:].astype(BF16)

    sm = sm_ref[...]
    g = -jnp.exp(alog_ref[...]) * _softplus(sm[:, GATE_A:GATE_A + NG] + dtb_ref[...])
    beta_ref[...] = jax.nn.sigmoid(sm[:, GATE_BT:GATE_BT + NG])
    _, incl_f, _, _ = _chunk_masks(ts, False)
    _, incl_b, _, _ = _chunk_masks(ts, True)
    tri_f = incl_f.astype(F32)
    tri_b = incl_b.astype(F32)
    is_fwd = lax.broadcasted_iota(jnp.int32, (ts, NG), 1) < GDN_HEADS
    gc_ref[...] = jnp.where(is_fwd, _dot_exact(tri_f, g), _dot_exact(tri_b, g))
    smt = smt_ref[...]
    gt = -jnp.exp(alogt_ref[...]) * _softplus(smt[GATE_A:GATE_A + NG, :] + dtbt_ref[...])
    is_fwd_t = lax.broadcasted_iota(jnp.int32, (NG, ts), 0) < GDN_HEADS
    gct_ref[...] = jnp.where(is_fwd_t, _dot_nt_exact(gt, tri_f), _dot_nt_exact(gt, tri_b))
    same, _, _, _ = _chunk_masks(ts, False)
    glt_ref[...] = _dot_exact(gt, same.astype(F32))


def _gdn_prep(qkv, sm, smt, conv_w, a_log, dt_bias, b, s, ts):
    t = b * s
    nj = s // ts
    c = qkv.shape[1]
    hb = ts // HALO
    d = GDN_HEADS * GDN_DK
    tile = lambda n: pl.BlockSpec((ts, n), lambda bi, j: (bi * nj + j, 0))
    tile_t = lambda n: pl.BlockSpec((n, ts), lambda bi, j: (0, bi * nj + j))
    return pl.pallas_call(
        functools.partial(_gdn_prep_kernel, ts=ts),
        grid=(b, nj),
        in_specs=[tile(c),
                  pl.BlockSpec((HALO, c), lambda bi, j: (jnp.maximum((bi * nj + j) * hb - 1, 0), 0)),
                  pl.BlockSpec((HALO, c), lambda bi, j: (jnp.minimum((bi * nj + j + 1) * hb, t // HALO - 1), 0)),
                  tile(sm.shape[1]), tile_t(smt.shape[0]),
                  _resident(conv_w.shape), _resident((1, NG)), _resident((1, NG)),
                  _resident((NG, 1)), _resident((NG, 1))],
        out_specs=[tile(d), tile(d), tile(d),
                   pl.BlockSpec((GDN_HEADS, GDN_DK, ts), lambda bi, j: (0, 0, bi * nj + j)),
                   tile(NG), tile(NG), tile_t(NG), tile_t(NG)],
        out_shape=[jax.ShapeDtypeStruct((t, d), BF16)] * 3
        + [jax.ShapeDtypeStruct((GDN_HEADS, GDN_DK, t), BF16),
           jax.ShapeDtypeStruct((t, NG), F32), jax.ShapeDtypeStruct((t, NG), F32),
           jax.ShapeDtypeStruct((NG, t), F32), jax.ShapeDtypeStruct((NG, t), F32)],
        scratch_shapes=[pltpu.VMEM((ts + 2 * HALO, c), F32)],
        compiler_params=_params("arbitrary", "arbitrary"),
        name="gdn_prep",
    )(qkv, qkv, qkv, sm, smt, conv_w, a_log.reshape(1, NG), dt_bias.reshape(1, NG),
      a_log.reshape(NG, 1), dt_bias.reshape(NG, 1))


STEP = 2 * CHUNK


def _gdn_scan_kernel(*refs, revs):
    nd = len(revs)
    ins = [refs[8 * d:8 * d + 8] for d in range(nd)]
    outs = refs[8 * nd:9 * nd]
    s_ref = refs[9 * nd]

    @pl.when(pl.program_id(1) == 0)
    def _():
        s_ref[...] = jnp.zeros(s_ref.shape, F32)

    masks = [_chunk_masks(STEP, rev) for rev in revs]
    lane = lax.broadcasted_iota(jnp.int32, (1, STEP), 1)
    zeros = jnp.zeros((CHUNK, GDN_DV), F32)
    chains = [(d, h) for d in range(nd) for h in range(GDN_HEADS)]
    ids = range(len(chains))
    sl = [slice(h * GDN_DK, (h + 1) * GDN_DK) for _, h in chains]
    col = [(GDN_HEADS if revs[d] else 0) + h for d, h in chains]
    incl = [masks[d][1] for d, _ in chains]
    strict = [masks[d][2] for d, _ in chains]
    eye = masks[0][3].astype(F32)
    left = lax.broadcasted_iota(jnp.int32, (CHUNK, STEP), 1) < CHUNK
    q_ref, k_ref, v_ref, kt_ref, gc_ref, beta_ref, gct_ref, glt_ref = (
        [ins[d][i] for d, _ in chains] for i in range(8))
    q = [q_ref[c][:, sl[c]] for c in ids]
    kt = [kt_ref[c][chains[c][1]] for c in ids]
    gcol = [gc_ref[c][:, col[c]:col[c] + 1] for c in ids]
    bcol = [beta_ref[c][:, col[c]:col[c] + 1] for c in ids]
    grow = [gct_ref[c][col[c]:col[c] + 1, :] for c in ids]
    glrow = [glt_ref[c][col[c]:col[c] + 1, :] for c in ids]
    decay = [jnp.exp(jnp.where(incl[c], gcol[c] - grow[c], -jnp.inf)) for c in ids]
    egc = [jnp.exp(gcol[c]) for c in ids]
    kb = [k_ref[c][:, sl[c]].astype(F32) * bcol[c] for c in ids]
    aq = [_dot(jnp.concatenate([kb[c].astype(BF16), q[c]], axis=0), kt[c]) for c in ids]
    nil = [-jnp.where(strict[c], aq[c][:STEP] * decay[c], 0.0) for c in ids]
    intra = [(aq[c][STEP:] * decay[c]).astype(BF16) for c in ids]
    wide = lambda m: m[:CHUNK] + m[CHUNK:]
    blocks = lambda w: jnp.concatenate([jnp.where(left, w, 0.0), jnp.where(left, 0.0, w)], axis=0)
    pw = [wide(nil[c]) for c in ids]
    iw = [wide(eye) + pw[c] for c in ids]
    pw = [_dot(pw[c].astype(BF16), nil[c].astype(BF16)) for c in ids]
    for _ in range(int(math.log2(CHUNK)) - 2):
        pd = [blocks(pw[c]).astype(BF16) for c in ids]
        both = [_dot(jnp.concatenate([iw[c], pw[c]], axis=0).astype(BF16), pd[c]) for c in ids]
        iw = [iw[c] + both[c][:CHUNK] for c in ids]
        pw = [both[c][CHUNK:] for c in ids]
    iw = [iw[c] + _dot(iw[c].astype(BF16), blocks(pw[c]).astype(BF16)) for c in ids]
    inv = [blocks(iw[c]) for c in ids]
    rhs = [jnp.concatenate([(v_ref[c][:, sl[c]].astype(F32) * bcol[c]).astype(BF16),
                            (kb[c] * egc[c]).astype(BF16)], axis=1) for c in ids]
    uw = [_dot(inv[c].astype(BF16), rhs[c]) for c in ids]
    qd = [(q[c].astype(F32) * egc[c]).astype(BF16) for c in ids]
    kdt = [(kt[c].astype(F32) * jnp.exp(glrow[c] - grow[c])).astype(BF16) for c in ids]
    state = [s_ref[c] for c in ids]
    for half in range(2):
        ch = [1 - half if revs[d] else half for d, _ in chains]
        r = [slice(ch[c] * CHUNK, (ch[c] + 1) * CHUNK) for c in ids]
        ws = [_dot(jnp.concatenate([uw[c][r[c], GDN_DV:].astype(BF16), qd[c][r[c]]], axis=0),
                   state[c].astype(BF16)) for c in ids]
        v_new = [uw[c][r[c], :GDN_DV] - ws[c][:CHUNK] for c in ids]
        vpad = [jnp.concatenate([v_new[c], zeros] if ch[c] == 0 else [zeros, v_new[c]], axis=0).astype(BF16)
                for c in ids]
        res = [_dot(jnp.concatenate([intra[c][r[c]], kdt[c]], axis=0), vpad[c]) for c in ids]
        for c in ids:
            outs[chains[c][0]][r[c], sl[c]] = (ws[c][CHUNK:] + res[c][:CHUNK]).astype(outs[0].dtype)
        gl = [jnp.sum(jnp.where(lane == ch[c] * CHUNK, glrow[c], 0.0), axis=-1, keepdims=True) for c in ids]
        state = [state[c] * jnp.exp(gl[c]) + res[c][CHUNK:] for c in ids]
    for c in ids:
        s_ref[c] = state[c]


def _gdn_scan(q, k, v, kt, gc, beta, gct, glt, b, s, revs):
    t = b * s
    n = s // STEP
    d = GDN_HEADS * GDN_DK
    in_specs, operands, out_specs = [], [], []
    for rev in revs:
        blk = (lambda bi, i: bi * n + (n - 1 - i)) if rev else (lambda bi, i: bi * n + i)
        tile = lambda w, blk=blk: pl.BlockSpec((STEP, w), lambda bi, i: (blk(bi, i), 0))
        tile_t = lambda w, blk=blk: pl.BlockSpec((w, STEP), lambda bi, i: (0, blk(bi, i)))
        in_specs += [tile(d), tile(d), tile(d),
                     pl.BlockSpec((GDN_HEADS, GDN_DK, STEP), lambda bi, i, blk=blk: (0, 0, blk(bi, i))),
                     tile(NG), tile(NG), tile_t(NG), tile_t(NG)]
        operands += [q, k, v, kt, gc, beta, gct, glt]
        out_specs.append(tile(d))
    return pl.pallas_call(
        functools.partial(_gdn_scan_kernel, revs=tuple(revs)),
        grid=(b, n),
        in_specs=in_specs,
        out_specs=out_specs,
        out_shape=[jax.ShapeDtypeStruct((t, d), BF16)] * len(revs),
        scratch_shapes=[pltpu.VMEM((len(revs) * GDN_HEADS, GDN_DK, GDN_DV), F32)],
        compiler_params=_params("arbitrary", "arbitrary"),
        name="gdn_scan",
    )(*operands)


def _gdn(qkv, sm, smt, conv_w, a_log, dt_bias, b, s):
    ts = min(GDN_PREP_TILE, s)
    q, k, v, kt, gc, beta, gct, glt = _gdn_prep(qkv, sm, smt, conv_w, a_log, dt_bias, b, s, ts)
    o_f, o_b = _gdn_scan(q, k, v, kt, gc, beta, gct, glt, b, s, (False, True))
    return o_f, o_b


def _outproj_kernel(x_ref, att_ref, of_ref, ob_ref, z_ref, ga_ref, gg_ref, wo_ref, nf_ref, wr_ref, br_ref,
                    y_ref, yn_ref, lg_ref):
    att = _rms(att_ref[...].astype(F32), ga_ref[...]).astype(BF16)
    o = of_ref[...].astype(F32) + ob_ref[...].astype(F32)
    z = z_ref[...].astype(F32)
    gg = gg_ref[...]
    lin = []
    for h in range(GDN_HEADS):
        sl = slice(h * GDN_DV, (h + 1) * GDN_DV)
        zh = z[:, sl]
        lin.append((_rms(o[:, sl], gg) * (zh * jax.nn.sigmoid(zh))).astype(BF16))
    lin = jnp.concatenate(lin, axis=-1)
    y = x_ref[...] + _dot(att, wo_ref[:D_ATTN, :]) + _dot(lin, wo_ref[D_ATTN:, :])
    y_ref[...] = y
    yn = _rms(y, nf_ref[...]).astype(BF16)
    _store_rows(yn_ref, _pack_halves(yn))
    lg_ref[...] = _dot(yn, wr_ref[...]) + br_ref[...]


def _outproj(x2d, att, o_f, o_b, z, g_attn, g_gdn, w_out, norm_ffn, w_r, b_r, tm):
    t, d = x2d.shape
    row = lambda n: pl.BlockSpec((tm, n), lambda i: (i, 0))
    return pl.pallas_call(
        _outproj_kernel,
        grid=(t // tm,),
        in_specs=[row(d), row(D_ATTN), row(D_GDN), row(D_GDN), row(D_GDN), _resident((1, D_ATTN)),
                  _resident((1, GDN_DV)), _resident(w_out.shape), _resident((1, d)), _resident(w_r.shape),
                  _resident((1, LANE))],
        out_specs=[row(d), pl.BlockSpec((tm * ROW_SUB, LANE), lambda i: (i, 0)), row(LANE)],
        out_shape=[jax.ShapeDtypeStruct((t, d), F32), jax.ShapeDtypeStruct((t * ROW_SUB, LANE), jnp.uint32),
                   jax.ShapeDtypeStruct((t, LANE), F32)],
        compiler_params=_params("arbitrary"),
        name="outproj",
    )(x2d, att, o_f, o_b, z, g_attn.reshape(1, -1), g_gdn.reshape(1, -1), w_out, norm_ffn.reshape(1, -1),
      w_r, b_r)


SLOT_ROWS = 8


def _router_kernel(lg_ref, start_ref, gate_ref, slot_ref, cnt_ref, carry_ref):
    @pl.when(pl.program_id(0) == 0)
    def _():
        carry_ref[...] = start_ref[...].astype(F32)

    lg = lg_ref[...]
    lane = lax.broadcasted_iota(jnp.int32, lg.shape, 1)
    neg = -jnp.inf

    def first_max(x):
        mx = jnp.max(x, axis=-1, keepdims=True)
        return mx, jnp.min(jnp.where(x == mx, lane, LANE), axis=-1, keepdims=True)

    gl = jnp.where(lane < N_GROUPS, lg, neg)
    gmax, grp = first_max(gl)
    grp_p = 1.0 / jnp.sum(jnp.exp(gl - gmax), axis=-1, keepdims=True)
    lo = N_GROUPS + grp * EXPERTS_PER_GROUP
    el = jnp.where((lane >= lo) & (lane < lo + EXPERTS_PER_GROUP), lg, neg)
    l1, i1 = first_max(el)
    l2, i2 = first_max(jnp.where(lane == i1, neg, el))
    e = jnp.exp(l2 - l1)
    g1 = grp_p / (1.0 + e)
    e1, e2 = i1 - N_GROUPS, i2 - N_GROUPS
    gate_ref[...] = jnp.where(lane == 0, g1, jnp.where(lane == 1, g1 * e, 0.0))
    tm = lg.shape[0]
    oh1, oh2 = lane == e1, lane == e2
    before = (lax.broadcasted_iota(jnp.int32, (tm, tm), 1)
              < lax.broadcasted_iota(jnp.int32, (tm, tm), 0)).astype(BF16)
    r1 = _dot(before, oh1.astype(BF16))
    r2 = _dot(before, oh2.astype(BF16))
    c1 = jnp.sum(oh1.astype(F32), axis=0, keepdims=True)
    c2 = jnp.sum(oh2.astype(F32), axis=0, keepdims=True)
    carry = carry_ref[...]
    rank1 = jnp.sum(jnp.where(oh1, r1 + carry, 0.0), axis=-1, keepdims=True)
    rank2 = jnp.sum(jnp.where(oh2, r2 + (carry + c1), 0.0), axis=-1, keepdims=True)
    info = jnp.where(lane == 0, e1.astype(F32), jnp.where(lane == 1, e2.astype(F32),
                     jnp.where(lane == 2, rank1, jnp.where(lane == 3, rank2, 0.0))))
    slot_ref[...] = info.T[:SLOT_ROWS, :].astype(jnp.int32)
    carry_ref[...] = carry + c1 + c2
    cnt_ref[...] = (carry + c1 + c2).astype(jnp.int32)


def _router(logits, counts, tm):
    t = logits.shape[0]
    row = pl.BlockSpec((tm, LANE), lambda i: (i, 0))
    return pl.pallas_call(
        _router_kernel,
        grid=(t // tm,),
        in_specs=[row, _resident((1, LANE))],
        out_specs=[row, pl.BlockSpec((SLOT_ROWS, tm), lambda i: (0, i)), pl.BlockSpec((1, LANE), lambda i: (0, 0))],
        out_shape=[jax.ShapeDtypeStruct((t, LANE), F32), jax.ShapeDtypeStruct((SLOT_ROWS, t), jnp.int32),
                   jax.ShapeDtypeStruct((1, LANE), jnp.int32)],
        scratch_shapes=[pltpu.VMEM((1, LANE), F32)],
        compiler_params=_params("arbitrary"),
        name="router",
    )(logits, counts)


ROW_UNROLL = 8


def _start_rows(tm, copy):
    def start(t, c):
        for k in range(TOP_K):
            copy(t, k).start()
        return c

    lax.fori_loop(0, tm, start, 0, unroll=ROW_UNROLL)


def _wait_rows(tm, copy):
    def wait(t, c):
        for k in range(TOP_K):
            copy(t, k).wait()
        return c

    lax.fori_loop(0, tm, wait, 0, unroll=ROW_UNROLL)


def _row_copies(tm, copy):
    _start_rows(tm, copy)
    _wait_rows(tm, copy)


def _dispatch_kernel(*refs):
    row, yn_ref, (xb_ref, sem) = _slot_row(refs[:N_SLOT_REFS]), refs[N_SLOT_REFS], refs[-2:]
    _row_copies(yn_ref.shape[0] // ROW_SUB, lambda t, k: pltpu.make_async_copy(
        yn_ref.at[_row_tile(t)], xb_ref.at[_row_tile(row(t, k))], sem))


N_SLOT_REFS = 2 * TOP_K + 1


def _slot_specs(tm, step=lambda i: i):
    tile = pl.BlockSpec((tm,), lambda i: (step(i),), memory_space=pltpu.SMEM)
    return [tile] * (2 * TOP_K) + [pl.BlockSpec(memory_space=pltpu.SMEM)]


def _slot_row(refs):
    eid, rank, pstart = refs[:TOP_K], refs[TOP_K:2 * TOP_K], refs[2 * TOP_K]
    return lambda t, k: pstart[eid[k][t]] + rank[k][t]


def _dispatch(slots, yn, xb, tm, n_rows):
    t = yn.shape[0] // ROW_SUB
    prev = [] if xb is None else [xb]
    return pl.pallas_call(
        _dispatch_kernel,
        grid=(t // tm,),
        in_specs=_slot_specs(tm) + [pl.BlockSpec((tm * ROW_SUB, LANE), lambda i: (i, 0))]
        + [pl.BlockSpec(memory_space=pl.ANY) for _ in prev],
        out_specs=pl.BlockSpec(memory_space=pl.ANY),
        out_shape=jax.ShapeDtypeStruct((n_rows * ROW_SUB, LANE), jnp.uint32),
        scratch_shapes=[pltpu.SemaphoreType.DMA],
        input_output_aliases={N_SLOT_REFS + 1: 0} if prev else {},
        compiler_params=_params("arbitrary", has_side_effects=True),
        name="moe_dispatch",
    )(*slots, yn, *prev)


MOE_BLOCK = 512


def _moe_kernel(be_ref, nu_ref, nv_ref, xb_ref, wg_ref, wu_ref, wd_ref, yb_ref, wg_bf, wu_bf, wd_bf):
    i = pl.program_id(0)

    @pl.when(i < nu_ref[0])
    def _():
        @pl.when((i == 0) | (be_ref[i] != be_ref[jnp.maximum(i - 1, 0)]))
        def _():
            wg_bf[...] = wg_ref[0].astype(BF16)
            wu_bf[...] = wu_ref[0].astype(BF16)
            wd_bf[...] = wd_ref[0].astype(BF16)

        x = _load_rows(xb_ref)
        valid = lax.broadcasted_iota(jnp.int32, (x.shape[0], 1), 0) < nv_ref[i]
        lo, hi = _unpack_halves(jnp.where(valid, x, jnp.uint32(0)))
        lo, hi = lo.astype(BF16), hi.astype(BF16)
        w = lo.shape[1]
        a = _dot(lo, wg_bf[:w, :]) + _dot(hi, wg_bf[w:, :])
        u = _dot(lo, wu_bf[:w, :]) + _dot(hi, wu_bf[w:, :])
        hdn = (a * jax.nn.sigmoid(a) * u).astype(BF16)
        _store_rows(yb_ref, _pack_halves(_dot(hdn, wd_bf[...]).astype(BF16)))


def _moe_blocks(block_e, n_used, n_valid, xb, w_gate, w_up, w_down):
    p = xb.shape[0] // ROW_SUB
    d = w_gate.shape[1]
    nb = p // MOE_BLOCK
    clamp = lambda i, nu: jnp.minimum(i, nu[0] - 1)
    rows = pl.BlockSpec((MOE_BLOCK * ROW_SUB, LANE), lambda i, be, nu, nv: (clamp(i, nu), 0))
    grid_spec = pltpu.PrefetchScalarGridSpec(
        num_scalar_prefetch=3,
        grid=(nb,),
        in_specs=[rows,
                  pl.BlockSpec((1, d, D_EXPERT), lambda i, be, nu, nv: (be[clamp(i, nu)], 0, 0)),
                  pl.BlockSpec((1, d, D_EXPERT), lambda i, be, nu, nv: (be[clamp(i, nu)], 0, 0)),
                  pl.BlockSpec((1, D_EXPERT, d), lambda i, be, nu, nv: (be[clamp(i, nu)], 0, 0))],
        out_specs=rows,
        scratch_shapes=[pltpu.VMEM((d, D_EXPERT), BF16), pltpu.VMEM((d, D_EXPERT), BF16),
                        pltpu.VMEM((D_EXPERT, d), BF16)],
    )
    return pl.pallas_call(
        _moe_kernel,
        grid_spec=grid_spec,
        out_shape=jax.ShapeDtypeStruct(xb.shape, jnp.uint32),
        compiler_params=_params("arbitrary"),
        name="moe_experts",
    )(block_e, n_used, n_valid, xb, w_gate, w_up, w_down)


def _combine_kernel(*refs):
    row, next_row = _slot_row(refs[:N_SLOT_REFS]), _slot_row(refs[N_SLOT_REFS:2 * N_SLOT_REFS])
    y_ref, gate_ref, g_ref, yb_ref, o_ref, buf, sems = refs[2 * N_SLOT_REFS:]
    i, n, tm = pl.program_id(0), pl.num_programs(0), y_ref.shape[0]
    slot = i % 2

    def gather(row, slot):
        return lambda t, k: pltpu.make_async_copy(
            yb_ref.at[_row_tile(row(t, k))], buf.at[slot, k, _row_tile(t)], sems.at[slot])

    @pl.when(i == 0)
    def _():
        _start_rows(tm, gather(row, 0))

    @pl.when(i + 1 < n)
    def _():
        _start_rows(tm, gather(next_row, 1 - slot))

    _wait_rows(tm, gather(row, slot))
    gate = gate_ref[...]
    acc_lo = acc_hi = None
    for k in range(TOP_K):
        lo, hi = _unpack_halves(_load_rows(buf.at[slot, k]))
        gk = gate[:, k:k + 1]
        acc_lo = gk * lo if acc_lo is None else acc_lo + gk * lo
        acc_hi = gk * hi if acc_hi is None else acc_hi + gk * hi
    y = y_ref[...] + jnp.concatenate([acc_lo, acc_hi], axis=-1)
    o_ref[...] = _rms(y, g_ref[...])


def _combine(slots, y, gate, g, yb, tm):
    t, d = y.shape
    n = t // tm
    return pl.pallas_call(
        _combine_kernel,
        grid=(n,),
        in_specs=_slot_specs(tm) + _slot_specs(tm, lambda i: jnp.minimum(i + 1, n - 1))
        + [pl.BlockSpec((tm, d), lambda i: (i, 0)),
                                    pl.BlockSpec((tm, LANE), lambda i: (i, 0)),
                                    _resident((1, d)), pl.BlockSpec(memory_space=pl.ANY)],
        out_specs=pl.BlockSpec((tm, d), lambda i: (i, 0)),
        out_shape=jax.ShapeDtypeStruct((t, d), F32),
        scratch_shapes=[pltpu.VMEM((2, TOP_K, tm * ROW_SUB, LANE), jnp.uint32), pltpu.SemaphoreType.DMA((2,))],
        compiler_params=_params("arbitrary"),
        name="moe_combine",
    )(*slots, *slots, y, gate, g.reshape(1, -1), yb)


def _route_plan(counts, n_slots):
    padded = (counts + MOE_BLOCK - 1) // MOE_BLOCK * MOE_BLOCK
    pend = jnp.cumsum(padded)
    pstart = pend - padded
    n_blocks = -(-n_slots // MOE_BLOCK) + N_EXPERTS
    first_row = jnp.arange(n_blocks, dtype=jnp.int32) * MOE_BLOCK
    block_e = jnp.minimum(jnp.sum(pend[None, :] <= first_row[:, None], axis=1), N_EXPERTS - 1).astype(jnp.int32)
    n_used = (pend[-1] // MOE_BLOCK).astype(jnp.int32).reshape(1)
    n_valid = jnp.clip(jnp.take(pstart + counts, block_e) - first_row, 0, MOE_BLOCK).astype(jnp.int32)
    return pstart.astype(jnp.int32), block_e, n_used, n_valid, n_blocks


def _encoder_front(x, p, wts):
    b, s, d = x.shape
    x2d = x.reshape(b * s, d)
    tm = min(TOKEN_TILE, b * s)
    lat, sm, smt, qkv, z = _inproj(x2d, p["norm_mix"].reshape(1, -1), wts["w_lat"], wts["w_sm"], wts["w_smt"],
                                   wts["w_qkv"], wts["w_z"], tm)
    att = _mla(lat, sm, p["g_q_lora"], p["g_kv_lora"], p["w_uq"], p["w_ukv"], b, s)
    o_f, o_b = _gdn(qkv, sm, smt, p["conv_w"], p["a_log"], p["dt_bias"], b, s)
    return _outproj(x2d, att, o_f, o_b, z, p["g_attn_out"], p["g_gdn_out"], wts["w_out"], p["norm_ffn"],
                    wts["w_r"], wts["b_r"], tm)


def _prep_weights(p):
    w_in = p["w_in"]
    o = np.cumsum([0, Q_LORA, KV_LORA, QK_ROPE, GDN_QKV, D_GDN, NG, NG])
    d = w_in.shape[0]
    half = QK_ROPE // 2
    w_sm = jnp.concatenate([w_in[:, o[2]:o[3]], w_in[:, o[5]:o[7]], jnp.zeros((d, LANE - QK_ROPE - 2 * NG), F32),
                            w_in[:, o[2] + half:o[3]], w_in[:, o[2]:o[2] + half],
                            jnp.zeros((d, LANE - QK_ROPE), F32)], axis=1).astype(BF16)
    w_r = jnp.concatenate([p["w_router_group"], p["w_router_expert"],
                           jnp.zeros((d, LANE - N_GROUPS - N_EXPERTS), F32)], axis=1).astype(BF16)
    b_r = jnp.concatenate([p["b_router_group"], p["b_router_expert"],
                           jnp.zeros((LANE - N_GROUPS - N_EXPERTS,), F32)]).reshape(1, LANE)
    return dict(w_lat=w_in[:, :o[2]].astype(BF16), w_sm=w_sm, w_smt=w_sm[:, :LANE].T, w_qkv=w_in[:, o[3]:o[4]].astype(BF16),
                w_z=w_in[:, o[4]:o[5]].astype(BF16), w_out=p["w_out"].astype(BF16), w_r=w_r, b_r=b_r)


ROW_TILE = 512


def _encode(xs, p):
    wts = _prep_weights(p)
    fronts = [_encoder_front(x, p, wts) for x in xs]
    counts = jnp.zeros((1, LANE), jnp.int32)
    routed = []
    for f in fronts:
        gate, slots, counts = _router(f[2], counts, min(ROW_TILE, f[2].shape[0]))
        routed.append((gate, [slots[r] for r in range(2 * TOP_K)]))
    t = sum(f[0].shape[0] for f in fronts)
    pstart, block_e, n_used, n_valid, n_blocks = _route_plan(counts[0, :N_EXPERTS], t * TOP_K)
    xb = None
    for f, (_, slots) in zip(fronts, routed):
        xb = _dispatch(slots + [pstart], f[1], xb, min(ROW_TILE, f[0].shape[0]), n_blocks * MOE_BLOCK)
    yb = _moe_blocks(block_e, n_used, n_valid, xb, p["w_gate"], p["w_up"], p["w_down"])
    outs = []
    for x, f, (gate, slots) in zip(xs, fronts, routed):
        out = _combine(slots + [pstart], f[0], gate, p["norm_final"], yb, min(ROW_TILE, f[0].shape[0]))
        outs.append(out.reshape(x.shape))
    return outs


def kernel(x_prompt, x_sample, norm_mix, w_in, g_q_lora, g_kv_lora, w_uq, w_ukv, g_attn_out, conv_w, a_log,
           dt_bias, g_gdn_out, w_out, norm_ffn, w_router_group, b_router_group, w_router_expert,
           b_router_expert, w_gate, w_up, w_down, norm_final):
    p = dict(norm_mix=norm_mix[0], w_in=w_in[0], g_q_lora=g_q_lora[0], g_kv_lora=g_kv_lora[0], w_uq=w_uq[0],
             w_ukv=w_ukv[0], g_attn_out=g_attn_out[0], conv_w=conv_w[0], a_log=a_log[0], dt_bias=dt_bias[0],
             g_gdn_out=g_gdn_out[0], w_out=w_out[0], norm_ffn=norm_ffn[0], w_router_group=w_router_group[0],
             b_router_group=b_router_group[0], w_router_expert=w_router_expert[0],
             b_router_expert=b_router_expert[0], w_gate=w_gate[0], w_up=w_up[0], w_down=w_down[0],
             norm_final=norm_final)
    y_prompt, y_sample = _encode([x_prompt, x_sample], p)
    return (y_prompt, y_sample)
```

```python
import functools
import math

import jax
import jax.numpy as jnp
import numpy as np
from jax import lax
from jax.experimental import pallas as pl
from jax.experimental.pallas import tpu as pltpu

F32 = jnp.float32
BF16 = jnp.bfloat16

D_MODEL = 2048
MLA_HEADS = 8
Q_LORA = 512
KV_LORA = 512
QK_NOPE = 128
QK_ROPE = 64
V_HEAD = 128
ROPE_THETA = 10000.0
GDN_HEADS = 8
GDN_DK = 128
GDN_DV = 128
GDN_QKV = GDN_HEADS * (2 * GDN_DK + GDN_DV)
CONV_K = 5
CHUNK = 64
D_ATTN = MLA_HEADS * V_HEAD
D_GDN = GDN_HEADS * GDN_DV
N_GROUPS = 8
EXPERTS_PER_GROUP = 8
N_EXPERTS = N_GROUPS * EXPERTS_PER_GROUP
TOP_K = 2
D_EXPERT = 512
EPS = 1e-6

LANE = 128
QK_PAD = 256
VMEM_LIMIT = 56 * 1024 * 1024
TOKEN_TILE = 512
GDN_PREP_TILE = 256


def _params(*sem, **kw):
    return pltpu.CompilerParams(dimension_semantics=sem, vmem_limit_bytes=VMEM_LIMIT, **kw)


def _resident(shape):
    return pl.BlockSpec(shape, lambda *_: (0,) * len(shape), pipeline_mode=pl.Buffered(1))


def _rms(x, g):
    return x * lax.rsqrt(jnp.mean(x * x, axis=-1, keepdims=True) + EPS) * g


def _dot(a, b):
    return jnp.dot(a, b, preferred_element_type=F32)


def _dot_nt(a, b):
    return lax.dot_general(a, b, (((1,), (1,)), ((), ())), preferred_element_type=F32)


def _pack_halves(x):
    w = x.shape[1] // 2
    lo = pltpu.bitcast(x[:, :w].astype(F32), jnp.uint32)
    hi = pltpu.bitcast(x[:, w:].astype(F32), jnp.uint32)
    return (hi & jnp.uint32(0xFFFF0000)) | (lo >> 16)


ROW_SUB = 8


def _row_tile(r):
    return pl.ds(pl.multiple_of(r * ROW_SUB, ROW_SUB), ROW_SUB)


def _store_rows(ref, u):
    n = u.shape[0]
    for s in range(ROW_SUB):
        ref[pl.ds(s, n, stride=ROW_SUB), :] = u[:, s * LANE:(s + 1) * LANE]


def _load_rows(ref):
    n = ref.shape[0] // ROW_SUB
    return jnp.concatenate([ref[pl.ds(s, n, stride=ROW_SUB), :] for s in range(ROW_SUB)], axis=1)


def _unpack_halves(u):
    lo = pltpu.bitcast(u << 16, F32)
    hi = pltpu.bitcast(u & jnp.uint32(0xFFFF0000), F32)
    return lo, hi


def _inproj_kernel(x_ref, g_ref, w_lat_ref, w_sm_ref, w_smt_ref, w_qkv_ref, w_z_ref,
                   lat_ref, sm_ref, smt_ref, qkv_ref, z_ref):
    xn = _rms(x_ref[...], g_ref[...]).astype(BF16)
    lat_ref[...] = _dot(xn, w_lat_ref[...]).astype(BF16)
    sm_ref[...] = _dot(xn, w_sm_ref[...])
    smt_ref[...] = _dot_nt(w_smt_ref[...], xn)
    qkv_ref[...] = _dot(xn, w_qkv_ref[...]).astype(BF16)
    z_ref[...] = _dot(xn, w_z_ref[...]).astype(BF16)


def _inproj(x2d, g, w_lat, w_sm, w_smt, w_qkv, w_z, tm):
    t, d = x2d.shape
    n_lat, n_sm, n_qkv, n_z = w_lat.shape[1], w_sm.shape[1], w_qkv.shape[1], w_z.shape[1]
    n_smt = w_smt.shape[0]
    row = lambda n: pl.BlockSpec((tm, n), lambda i: (i, 0))
    return pl.pallas_call(
        _inproj_kernel,
        grid=(t // tm,),
        in_specs=[row(d), _resident((1, d)), _resident(w_lat.shape), _resident(w_sm.shape),
                  _resident(w_smt.shape), _resident(w_qkv.shape), _resident(w_z.shape)],
        out_specs=[row(n_lat), row(n_sm), pl.BlockSpec((n_smt, tm), lambda i: (0, i)), row(n_qkv), row(n_z)],
        out_shape=[jax.ShapeDtypeStruct((t, n_lat), BF16), jax.ShapeDtypeStruct((t, n_sm), F32),
                   jax.ShapeDtypeStruct((n_smt, t), F32), jax.ShapeDtypeStruct((t, n_qkv), BF16),
                   jax.ShapeDtypeStruct((t, n_z), BF16)],
        compiler_params=_params("arbitrary"),
        name="inproj",
    )(x2d, g, w_lat, w_sm, w_smt, w_qkv, w_z)


def _mla_up_kernel(lat_ref, sm_ref, gq_ref, gkv_ref, wqt_ref, wk_ref, wvt_ref,
                   cos_t_ref, sin_t_ref, ck_ref, sk_ref, qt_ref, k_ref, vt_ref):
    lat = lat_ref[...].astype(F32)
    cqn = _rms(lat[:, :Q_LORA], gq_ref[...]).astype(BF16)
    ckvn = _rms(lat[:, Q_LORA:], gkv_ref[...]).astype(BF16)
    half = QK_ROPE // 2
    dqk = QK_NOPE + QK_ROPE
    qt = _dot_nt(wqt_ref[...], cqn)
    cos_t, sin_t = cos_t_ref[...], sin_t_ref[...]
    zero = jnp.zeros((QK_PAD - dqk, qt.shape[1]), BF16)
    for h in range(MLA_HEADS):
        lo = h * dqk
        x1 = qt[lo + QK_NOPE:lo + QK_NOPE + half, :]
        x2 = qt[lo + QK_NOPE + half:lo + dqk, :]
        qt_ref[0, h, 0, 0:QK_NOPE, :] = qt[lo:lo + QK_NOPE, :].astype(BF16)
        qt_ref[0, h, 0, QK_NOPE:QK_NOPE + half, :] = (x1 * cos_t - x2 * sin_t).astype(BF16)
        qt_ref[0, h, 0, QK_NOPE + half:dqk, :] = (x2 * cos_t + x1 * sin_t).astype(BF16)
        qt_ref[0, h, 0, dqk:, :] = zero
    kn = _dot(ckvn, wk_ref[...])
    sm = sm_ref[...]
    pe = (sm[:, :LANE] * ck_ref[...] + sm[:, LANE:] * sk_ref[...]).astype(BF16)
    for h in range(MLA_HEADS):
        k_ref[:, h * QK_PAD:h * QK_PAD + QK_NOPE] = kn[:, h * QK_NOPE:(h + 1) * QK_NOPE].astype(BF16)
        k_ref[:, h * QK_PAD + QK_NOPE:(h + 1) * QK_PAD] = pe
    vt = _dot_nt(wvt_ref[...], ckvn)
    for h in range(MLA_HEADS):
        vt_ref[0, h, 0, 0:V_HEAD, :] = vt[h * V_HEAD:(h + 1) * V_HEAD, :].astype(BF16)
        vt_ref[0, h, 0, V_HEAD:, :] = jnp.ones((V_ROWS - V_HEAD, vt.shape[1]), BF16)


def _mla_up(lat, sm, gq, gkv, wqt, wk, wvt, cos_t, sin_t, ck, sk, b, s, ts):
    t = b * s
    nj = s // ts
    hk = MLA_HEADS * QK_PAD
    return pl.pallas_call(
        _mla_up_kernel,
        grid=(b, nj),
        in_specs=[pl.BlockSpec((ts, lat.shape[1]), lambda bi, j: (bi * nj + j, 0)),
                  pl.BlockSpec((ts, sm.shape[1]), lambda bi, j: (bi * nj + j, 0)),
                  _resident(gq.shape), _resident(gkv.shape), _resident(wqt.shape), _resident(wk.shape),
                  _resident(wvt.shape),
                  pl.BlockSpec((QK_ROPE // 2, ts), lambda bi, j: (0, j)),
                  pl.BlockSpec((QK_ROPE // 2, ts), lambda bi, j: (0, j)),
                  pl.BlockSpec((ts, LANE), lambda bi, j: (j, 0)),
                  pl.BlockSpec((ts, LANE), lambda bi, j: (j, 0))],
        out_specs=[pl.BlockSpec((1, MLA_HEADS, 1, QK_PAD, ts), lambda bi, j: (bi, 0, j, 0, 0)),
                   pl.BlockSpec((ts, hk), lambda bi, j: (bi * nj + j, 0)),
                   pl.BlockSpec((1, MLA_HEADS, 1, V_ROWS, ts), lambda bi, j: (bi, 0, j, 0, 0))],
        out_shape=[jax.ShapeDtypeStruct((b, MLA_HEADS, nj, QK_PAD, ts), BF16),
                   jax.ShapeDtypeStruct((t, hk), BF16),
                   jax.ShapeDtypeStruct((b, MLA_HEADS, nj, V_ROWS, ts), BF16)],
        compiler_params=_params("arbitrary", "arbitrary"),
        name="mla_up",
    )(lat, sm, gq, gkv, wqt, wk, wvt, cos_t, sin_t, ck, sk)


ATTN_UNROLL = 16
ATTN_Q_ROWS = 1024
ATTN_SHORT_SEQ = 2048
V_ROWS = V_HEAD + 16


def _attn_kernel(qt_ref, k_ref, vt_ref, o_ref, m_ref, acc_ref, s_buf, c_buf, *, nk, tk, unroll):
    qt = jnp.concatenate([qt_ref[0, 0, i] for i in range(qt_ref.shape[2])], axis=1)
    m_ref[...] = jnp.full(m_ref.shape, -jnp.inf, F32)
    acc_ref[...] = jnp.zeros(acc_ref.shape, F32)

    def scores(j, buf):
        s = _dot(k_ref[pl.ds(pl.multiple_of(j * tk, tk), tk), :], qt)
        s_buf[buf] = s
        c_buf[buf] = jnp.max(s, axis=0, keepdims=True)

    scores(0, 0)

    def step(j, cur, nxt):
        scores(jnp.minimum(j + 1, nk - 1), nxt)
        m_old = m_ref[...]
        m_new = jnp.maximum(m_old, c_buf[cur])
        alpha = jnp.exp2(m_old - m_new)
        p = jnp.exp2(s_buf[cur] - m_new).astype(BF16)
        acc_ref[...] = alpha * acc_ref[...] + _dot(vt_ref[0, 0, j], p)
        m_ref[...] = m_new

    def body(jj, carry):
        for u in range(unroll):
            step(unroll * jj + u, u % 2, 1 - u % 2)
        return carry

    lax.fori_loop(0, nk // unroll, body, 0)
    acc = acc_ref[...]
    o_ref[...] = (acc[:V_HEAD] / acc[V_HEAD:V_HEAD + 1]).T.astype(o_ref.dtype)


def _attention(qt, k, vt, b, s):
    nk, tk = vt.shape[2], vt.shape[4]
    qtiles = math.gcd(qt.shape[2], max(1, (s if s <= ATTN_SHORT_SEQ else ATTN_Q_ROWS) // qt.shape[4]))
    nq, tq = qt.shape[2] // qtiles, qt.shape[4] * qtiles
    assert nk % 2 == 0, "the score buffers alternate statically"
    unroll = math.gcd(nk, ATTN_UNROLL)
    return pl.pallas_call(
        functools.partial(_attn_kernel, nk=nk, tk=tk, unroll=unroll),
        grid=(b, MLA_HEADS, nq),
        in_specs=[pl.BlockSpec((1, 1, qtiles, QK_PAD, tq // qtiles), lambda bi, h, i: (bi, h, i, 0, 0)),
                  pl.BlockSpec((s, QK_PAD), lambda bi, h, i: (bi, h)),
                  pl.BlockSpec((1, 1, nk, V_ROWS, tk), lambda bi, h, i: (bi, h, 0, 0, 0))],
        out_specs=pl.BlockSpec((tq, V_HEAD), lambda bi, h, i: (bi * nq + i, h)),
        out_shape=jax.ShapeDtypeStruct((b * s, D_ATTN), BF16),
        scratch_shapes=[pltpu.VMEM((1, tq), F32), pltpu.VMEM((V_ROWS, tq), F32),
                        pltpu.VMEM((2, tk, tq), F32), pltpu.VMEM((2, 1, tq), F32)],
        compiler_params=_params("arbitrary", "arbitrary", "arbitrary"),
        name="attention",
    )(qt, k, vt)


def _rope_tables(s):
    inv_freq = np.float32(ROPE_THETA) ** (-np.arange(0, QK_ROPE, 2, dtype=np.float32) / np.float32(QK_ROPE))
    ang = np.arange(s, dtype=np.float32)[:, None] * inv_freq[None, :].astype(np.float32)
    return np.cos(ang.astype(np.float64)).astype(np.float32), np.sin(ang.astype(np.float64)).astype(np.float32)


def _prep_mla_weights(w_uq, w_ukv):
    scale = (QK_NOPE + QK_ROPE) ** -0.5 * math.log2(math.e)
    wqt = (w_uq * scale).T
    wkv = w_ukv.reshape(KV_LORA, MLA_HEADS, QK_NOPE + V_HEAD)
    wk = wkv[..., :QK_NOPE].reshape(KV_LORA, MLA_HEADS * QK_NOPE)
    wvt = wkv[..., QK_NOPE:].reshape(KV_LORA, MLA_HEADS * V_HEAD).T
    return wqt.astype(BF16), wk.astype(BF16), wvt.astype(BF16)


def _mla(lat, sm, g_q, g_kv, w_uq, w_ukv, b, s):
    ts = min(TOKEN_TILE, s)
    wqt, wk, wvt = _prep_mla_weights(w_uq, w_ukv)
    cos, sin = _rope_tables(s)
    zeros = np.zeros((s, LANE - QK_ROPE), np.float32)
    ck = np.concatenate([cos, cos, zeros], axis=-1)
    sk = np.concatenate([-sin, sin, zeros], axis=-1)
    qt, k, vt = _mla_up(lat, sm, g_q.reshape(1, -1), g_kv.reshape(1, -1), wqt, wk, wvt,
                        jnp.asarray(cos.T), jnp.asarray(sin.T), jnp.asarray(ck), jnp.asarray(sk), b, s, ts)
    return _attention(qt, k, vt, b, s)


HALO = 16
GATE_A = QK_ROPE
GATE_BT = QK_ROPE + 2 * GDN_HEADS
NG = 2 * GDN_HEADS


def _chunk_masks(n, rev):
    ri = lax.broadcasted_iota(jnp.int32, (n, n), 0)
    ci = lax.broadcasted_iota(jnp.int32, (n, n), 1)
    same = (ri // CHUNK) == (ci // CHUNK)
    if rev:
        return same, same & (ri <= ci), same & (ri < ci), ri == ci
    return same, same & (ri >= ci), same & (ri > ci), ri == ci


def _dot_exact(a, b):
    return jnp.dot(a, b, preferred_element_type=F32, precision=lax.Precision.HIGHEST)


def _dot_nt_exact(a, b):
    return lax.dot_general(a, b, (((1,), (1,)), ((), ())), preferred_element_type=F32,
                           precision=lax.Precision.HIGHEST)


def _softplus(x):
    return jnp.maximum(x, 0.0) + jnp.log(1.0 + jnp.exp(-jnp.abs(x)))


def _gdn_prep_kernel(qkv_ref, prev_ref, next_ref, sm_ref, smt_ref, cw_ref, alog_ref, dtb_ref,
                     alogt_ref, dtbt_ref, q_ref, k_ref, v_ref, kt_ref, gc_ref, beta_ref, gct_ref, glt_ref,
                     xs_ref, *, ts):
    j = pl.program_id(1)
    nj = pl.num_programs(1)
    xs_ref[0:HALO, :] = jnp.where(j > 0, prev_ref[...].astype(F32), 0.0)
    xs_ref[HALO:HALO + ts, :] = qkv_ref[...].astype(F32)
    xs_ref[HALO + ts:, :] = jnp.where(j < nj - 1, next_ref[...].astype(F32), 0.0)
    acc = None
    for tap in range(CONV_K):
        lo = HALO - CONV_K // 2 + tap
        term = xs_ref[lo:lo + ts, :] * cw_ref[tap:tap + 1, :]
        acc = term if acc is None else acc + term
    act = acc * jax.nn.sigmoid(acc)
    eye = (lax.broadcasted_iota(jnp.int32, (GDN_DK, GDN_DK), 0)
           == lax.broadcasted_iota(jnp.int32, (GDN_DK, GDN_DK), 1)).astype(BF16)
    hk = GDN_HEADS * GDN_DK
    for h in range(GDN_HEADS):
        qh = act[:, h * GDN_DK:(h + 1) * GDN_DK]
        kh = act[:, hk + h * GDN_DK:hk + (h + 1) * GDN_DK]
        qh = qh * (lax.rsqrt(jnp.sum(qh * qh, axis=-1, keepdims=True) + EPS) * GDN_DK ** -0.5)
        kh = (kh * lax.rsqrt(jnp.sum(kh * kh, axis=-1, keepdims=True) + EPS)).astype(BF16)
        q_ref[:, h * GDN_DK:(h + 1) * GDN_DK] = qh.astype(BF16)
        k_ref[:, h * GDN_DK:(h + 1) * GDN_DK] = kh
        kt_ref[h] = _dot_nt(eye, kh).astype(BF16)
    v_ref[...] = act[:, 2 * hk:].astype(BF16)

    sm = sm_ref[...]
    g = -jnp.exp(alog_ref[...]) * _softplus(sm[:, GATE_A:GATE_A + NG] + dtb_ref[...])
    beta_ref[...] = jax.nn.sigmoid(sm[:, GATE_BT:GATE_BT + NG])
    _, incl_f, _, _ = _chunk_masks(ts, False)
    _, incl_b, _, _ = _chunk_masks(ts, True)
    tri_f = incl_f.astype(F32)
    tri_b = incl_b.astype(F32)
    is_fwd = lax.broadcasted_iota(jnp.int32, (ts, NG), 1) < GDN_HEADS
    gc_ref[...] = jnp.where(is_fwd, _dot_exact(tri_f, g), _dot_exact(tri_b, g))
    smt = smt_ref[...]
    gt = -jnp.exp(alogt_ref[...]) * _softplus(smt[GATE_A:GATE_A + NG, :] + dtbt_ref[...])
    is_fwd_t = lax.broadcasted_iota(jnp.int32, (NG, ts), 0) < GDN_HEADS
    gct_ref[...] = jnp.where(is_fwd_t, _dot_nt_exact(gt, tri_f), _dot_nt_exact(gt, tri_b))
    same, _, _, _ = _chunk_masks(ts, False)
    glt_ref[...] = _dot_exact(gt, same.astype(F32))


def _gdn_prep(qkv, sm, smt, conv_w, a_log, dt_bias, b, s, ts):
    t = b * s
    nj = s // ts
    c = qkv.shape[1]
    hb = ts // HALO
    d = GDN_HEADS * GDN_DK
    tile = lambda n: pl.BlockSpec((ts, n), lambda bi, j: (bi * nj + j, 0))
    tile_t = lambda n: pl.BlockSpec((n, ts), lambda bi, j: (0, bi * nj + j))
    return pl.pallas_call(
        functools.partial(_gdn_prep_kernel, ts=ts),
        grid=(b, nj),
        in_specs=[tile(c),
                  pl.BlockSpec((HALO, c), lambda bi, j: (jnp.maximum((bi * nj + j) * hb - 1, 0), 0)),
                  pl.BlockSpec((HALO, c), lambda bi, j: (jnp.minimum((bi * nj + j + 1) * hb, t // HALO - 1), 0)),
                  tile(sm.shape[1]), tile_t(smt.shape[0]),
                  _resident(conv_w.shape), _resident((1, NG)), _resident((1, NG)),
                  _resident((NG, 1)), _resident((NG, 1))],
        out_specs=[tile(d), tile(d), tile(d),
                   pl.BlockSpec((GDN_HEADS, GDN_DK, ts), lambda bi, j: (0, 0, bi * nj + j)),
                   tile(NG), tile(NG), tile_t(NG), tile_t(NG)],
        out_shape=[jax.ShapeDtypeStruct((t, d), BF16)] * 3
        + [jax.ShapeDtypeStruct((GDN_HEADS, GDN_DK, t), BF16),
           jax.ShapeDtypeStruct((t, NG), F32), jax.ShapeDtypeStruct((t, NG), F32),
           jax.ShapeDtypeStruct((NG, t), F32), jax.ShapeDtypeStruct((NG, t), F32)],
        scratch_shapes=[pltpu.VMEM((ts + 2 * HALO, c), F32)],
        compiler_params=_params("arbitrary", "arbitrary"),
        name="gdn_prep",
    )(qkv, qkv, qkv, sm, smt, conv_w, a_log.reshape(1, NG), dt_bias.reshape(1, NG),
      a_log.reshape(NG, 1), dt_bias.reshape(NG, 1))


STEP = 2 * CHUNK


def _gdn_scan_kernel(*refs, revs):
    nd = len(revs)
    ins = [refs[8 * d:8 * d + 8] for d in range(nd)]
    outs = refs[8 * nd:9 * nd]
    s_ref = refs[9 * nd]

    @pl.when(pl.program_id(1) == 0)
    def _():
        s_ref[...] = jnp.zeros(s_ref.shape, F32)

    masks = [_chunk_masks(STEP, rev) for rev in revs]
    lane = lax.broadcasted_iota(jnp.int32, (1, STEP), 1)
    zeros = jnp.zeros((CHUNK, GDN_DV), F32)
    chains = [(d, h) for d in range(nd) for h in range(GDN_HEADS)]
    ids = range(len(chains))
    sl = [slice(h * GDN_DK, (h + 1) * GDN_DK) for _, h in chains]
    col = [(GDN_HEADS if revs[d] else 0) + h for d, h in chains]
    incl = [masks[d][1] for d, _ in chains]
    strict = [masks[d][2] for d, _ in chains]
    eye = masks[0][3].astype(F32)
    left = lax.broadcasted_iota(jnp.int32, (CHUNK, STEP), 1) < CHUNK
    q_ref, k_ref, v_ref, kt_ref, gc_ref, beta_ref, gct_ref, glt_ref = (
        [ins[d][i] for d, _ in chains] for i in range(8))
    q = [q_ref[c][:, sl[c]] for c in ids]
    kt = [kt_ref[c][chains[c][1]] for c in ids]
    gcol = [gc_ref[c][:, col[c]:col[c] + 1] for c in ids]
    bcol = [beta_ref[c][:, col[c]:col[c] + 1] for c in ids]
    grow = [gct_ref[c][col[c]:col[c] + 1, :] for c in ids]
    glrow = [glt_ref[c][col[c]:col[c] + 1, :] for c in ids]
    decay = [jnp.exp(jnp.where(incl[c], gcol[c] - grow[c], -jnp.inf)) for c in ids]
    egc = [jnp.exp(gcol[c]) for c in ids]
    kb = [k_ref[c][:, sl[c]].astype(F32) * bcol[c] for c in ids]
    aq = [_dot(jnp.concatenate([kb[c].astype(BF16), q[c]], axis=0), kt[c]) for c in ids]
    nil = [-jnp.where(strict[c], aq[c][:STEP] * decay[c], 0.0) for c in ids]
    intra = [(aq[c][STEP:] * decay[c]).astype(BF16) for c in ids]
    wide = lambda m: m[:CHUNK] + m[CHUNK:]
    blocks = lambda w: jnp.concatenate([jnp.where(left, w, 0.0), jnp.where(left, 0.0, w)], axis=0)
    pw = [wide(nil[c]) for c in ids]
    iw = [wide(eye) + pw[c] for c in ids]
    pw = [_dot(pw[c].astype(BF16), nil[c].astype(BF16)) for c in ids]
    for _ in range(int(math.log2(CHUNK)) - 2):
        pd = [blocks(pw[c]).astype(BF16) for c in ids]
        both = [_dot(jnp.concatenate([iw[c], pw[c]], axis=0).astype(BF16), pd[c]) for c in ids]
        iw = [iw[c] + both[c][:CHUNK] for c in ids]
        pw = [both[c][CHUNK:] for c in ids]
    iw = [iw[c] + _dot(iw[c].astype(BF16), blocks(pw[c]).astype(BF16)) for c in ids]
    inv = [blocks(iw[c]) for c in ids]
    rhs = [jnp.concatenate([(v_ref[c][:, sl[c]].astype(F32) * bcol[c]).astype(BF16),
                            (kb[c] * egc[c]).astype(BF16)], axis=1) for c in ids]
    uw = [_dot(inv[c].astype(BF16), rhs[c]) for c in ids]
    qd = [(q[c].astype(F32) * egc[c]).astype(BF16) for c in ids]
    kdt = [(kt[c].astype(F32) * jnp.exp(glrow[c] - grow[c])).astype(BF16) for c in ids]
    state = [s_ref[c] for c in ids]
    for half in range(2):
        ch = [1 - half if revs[d] else half for d, _ in chains]
        r = [slice(ch[c] * CHUNK, (ch[c] + 1) * CHUNK) for c in ids]
        ws = [_dot(jnp.concatenate([uw[c][r[c], GDN_DV:].astype(BF16), qd[c][r[c]]], axis=0),
                   state[c].astype(BF16)) for c in ids]
        v_new = [uw[c][r[c], :GDN_DV] - ws[c][:CHUNK] for c in ids]
        vpad = [jnp.concatenate([v_new[c], zeros] if ch[c] == 0 else [zeros, v_new[c]], axis=0).astype(BF16)
                for c in ids]
        res = [_dot(jnp.concatenate([intra[c][r[c]], kdt[c]], axis=0), vpad[c]) for c in ids]
        for c in ids:
            outs[chains[c][0]][r[c], sl[c]] = (ws[c][CHUNK:] + res[c][:CHUNK]).astype(outs[0].dtype)
        gl = [jnp.sum(jnp.where(lane == ch[c] * CHUNK, glrow[c], 0.0), axis=-1, keepdims=True) for c in ids]
        state = [state[c] * jnp.exp(gl[c]) + res[c][CHUNK:] for c in ids]
    for c in ids:
        s_ref[c] = state[c]


def _gdn_scan(q, k, v, kt, gc, beta, gct, glt, b, s, revs):
    t = b * s
    n = s // STEP
    d = GDN_HEADS * GDN_DK
    in_specs, operands, out_specs = [], [], []
    for rev in revs:
        blk = (lambda bi, i: bi * n + (n - 1 - i)) if rev else (lambda bi, i: bi * n + i)
        tile = lambda w, blk=blk: pl.BlockSpec((STEP, w), lambda bi, i: (blk(bi, i), 0))
        tile_t = lambda w, blk=blk: pl.BlockSpec((w, STEP), lambda bi, i: (0, blk(bi, i)))
        in_specs += [tile(d), tile(d), tile(d),
                     pl.BlockSpec((GDN_HEADS, GDN_DK, STEP), lambda bi, i, blk=blk: (0, 0, blk(bi, i))),
                     tile(NG), tile(NG), tile_t(NG), tile_t(NG)]
        operands += [q, k, v, kt, gc, beta, gct, glt]
        out_specs.append(tile(d))
    return pl.pallas_call(
        functools.partial(_gdn_scan_kernel, revs=tuple(revs)),
        grid=(b, n),
        in_specs=in_specs,
        out_specs=out_specs,
        out_shape=[jax.ShapeDtypeStruct((t, d), BF16)] * len(revs),
        scratch_shapes=[pltpu.VMEM((len(revs) * GDN_HEADS, GDN_DK, GDN_DV), F32)],
        compiler_params=_params("arbitrary", "arbitrary"),
        name="gdn_scan",
    )(*operands)


def _gdn(qkv, sm, smt, conv_w, a_log, dt_bias, b, s):
    ts = min(GDN_PREP_TILE, s)
    q, k, v, kt, gc, beta, gct, glt = _gdn_prep(qkv, sm, smt, conv_w, a_log, dt_bias, b, s, ts)
    o_f, o_b = _gdn_scan(q, k, v, kt, gc, beta, gct, glt, b, s, (False, True))
    return o_f, o_b


def _outproj_kernel(x_ref, att_ref, of_ref, ob_ref, z_ref, ga_ref, gg_ref, wo_ref, nf_ref, wr_ref, br_ref,
                    y_ref, yn_ref, lg_ref):
    att = _rms(att_ref[...].astype(F32), ga_ref[...]).astype(BF16)
    o = of_ref[...].astype(F32) + ob_ref[...].astype(F32)
    z = z_ref[...].astype(F32)
    gg = gg_ref[...]
    lin = []
    for h in range(GDN_HEADS):
        sl = slice(h * GDN_DV, (h + 1) * GDN_DV)
        zh = z[:, sl]
        lin.append((_rms(o[:, sl], gg) * (zh * jax.nn.sigmoid(zh))).astype(BF16))
    lin = jnp.concatenate(lin, axis=-1)
    y = x_ref[...] + _dot(att, wo_ref[:D_ATTN, :]) + _dot(lin, wo_ref[D_ATTN:, :])
    y_ref[...] = y
    yn = _rms(y, nf_ref[...]).astype(BF16)
    _store_rows(yn_ref, _pack_halves(yn))
    lg_ref[...] = _dot(yn, wr_ref[...]) + br_ref[...]


def _outproj(x2d, att, o_f, o_b, z, g_attn, g_gdn, w_out, norm_ffn, w_r, b_r, tm):
    t, d = x2d.shape
    row = lambda n: pl.BlockSpec((tm, n), lambda i: (i, 0))
    return pl.pallas_call(
        _outproj_kernel,
        grid=(t // tm,),
        in_specs=[row(d), row(D_ATTN), row(D_GDN), row(D_GDN), row(D_GDN), _resident((1, D_ATTN)),
                  _resident((1, GDN_DV)), _resident(w_out.shape), _resident((1, d)), _resident(w_r.shape),
                  _resident((1, LANE))],
        out_specs=[row(d), pl.BlockSpec((tm * ROW_SUB, LANE), lambda i: (i, 0)), row(LANE)],
        out_shape=[jax.ShapeDtypeStruct((t, d), F32), jax.ShapeDtypeStruct((t * ROW_SUB, LANE), jnp.uint32),
                   jax.ShapeDtypeStruct((t, LANE), F32)],
        compiler_params=_params("arbitrary"),
        name="outproj",
    )(x2d, att, o_f, o_b, z, g_attn.reshape(1, -1), g_gdn.reshape(1, -1), w_out, norm_ffn.reshape(1, -1),
      w_r, b_r)


SLOT_ROWS = 8


def _router_kernel(lg_ref, start_ref, gate_ref, slot_ref, cnt_ref, carry_ref):
    @pl.when(pl.program_id(0) == 0)
    def _():
        carry_ref[...] = start_ref[...].astype(F32)

    lg = lg_ref[...]
    lane = lax.broadcasted_iota(jnp.int32, lg.shape, 1)
    neg = -jnp.inf

    def first_max(x):
        mx = jnp.max(x, axis=-1, keepdims=True)
        return mx, jnp.min(jnp.where(x == mx, lane, LANE), axis=-1, keepdims=True)

    gl = jnp.where(lane < N_GROUPS, lg, neg)
    gmax, grp = first_max(gl)
    grp_p = 1.0 / jnp.sum(jnp.exp(gl - gmax), axis=-1, keepdims=True)
    lo = N_GROUPS + grp * EXPERTS_PER_GROUP
    el = jnp.where((lane >= lo) & (lane < lo + EXPERTS_PER_GROUP), lg, neg)
    l1, i1 = first_max(el)
    l2, i2 = first_max(jnp.where(lane == i1, neg, el))
    e = jnp.exp(l2 - l1)
    g1 = grp_p / (1.0 + e)
    e1, e2 = i1 - N_GROUPS, i2 - N_GROUPS
    gate_ref[...] = jnp.where(lane == 0, g1, jnp.where(lane == 1, g1 * e, 0.0))
    tm = lg.shape[0]
    oh1, oh2 = lane == e1, lane == e2
    before = (lax.broadcasted_iota(jnp.int32, (tm, tm), 1)
              < lax.broadcasted_iota(jnp.int32, (tm, tm), 0)).astype(BF16)
    r1 = _dot(before, oh1.astype(BF16))
    r2 = _dot(before, oh2.astype(BF16))
    c1 = jnp.sum(oh1.astype(F32), axis=0, keepdims=True)
    c2 = jnp.sum(oh2.astype(F32), axis=0, keepdims=True)
    carry = carry_ref[...]
    rank1 = jnp.sum(jnp.where(oh1, r1 + carry, 0.0), axis=-1, keepdims=True)
    rank2 = jnp.sum(jnp.where(oh2, r2 + (carry + c1), 0.0), axis=-1, keepdims=True)
    info = jnp.where(lane == 0, e1.astype(F32), jnp.where(lane == 1, e2.astype(F32),
                     jnp.where(lane == 2, rank1, jnp.where(lane == 3, rank2, 0.0))))
    slot_ref[...] = info.T[:SLOT_ROWS, :].astype(jnp.int32)
    carry_ref[...] = carry + c1 + c2
    cnt_ref[...] = (carry + c1 + c2).astype(jnp.int32)


def _router(logits, counts, tm):
    t = logits.shape[0]
    row = pl.BlockSpec((tm, LANE), lambda i: (i, 0))
    return pl.pallas_call(
        _router_kernel,
        grid=(t // tm,),
        in_specs=[row, _resident((1, LANE))],
        out_specs=[row, pl.BlockSpec((SLOT_ROWS, tm), lambda i: (0, i)), pl.BlockSpec((1, LANE), lambda i: (0, 0))],
        out_shape=[jax.ShapeDtypeStruct((t, LANE), F32), jax.ShapeDtypeStruct((SLOT_ROWS, t), jnp.int32),
                   jax.ShapeDtypeStruct((1, LANE), jnp.int32)],
        scratch_shapes=[pltpu.VMEM((1, LANE), F32)],
        compiler_params=_params("arbitrary"),
        name="router",
    )(logits, counts)


ROW_UNROLL = 8


def _start_rows(tm, copy):
    def start(t, c):
        for k in range(TOP_K):
            copy(t, k).start()
        return c

    lax.fori_loop(0, tm, start, 0, unroll=ROW_UNROLL)


def _wait_rows(tm, copy):
    def wait(t, c):
        for k in range(TOP_K):
            copy(t, k).wait()
        return c

    lax.fori_loop(0, tm, wait, 0, unroll=ROW_UNROLL)


def _row_copies(tm, copy):
    _start_rows(tm, copy)
    _wait_rows(tm, copy)


def _dispatch_kernel(*refs):
    row, yn_ref, (xb_ref, sem) = _slot_row(refs[:N_SLOT_REFS]), refs[N_SLOT_REFS], refs[-2:]
    _row_copies(yn_ref.shape[0] // ROW_SUB, lambda t, k: pltpu.make_async_copy(
        yn_ref.at[_row_tile(t)], xb_ref.at[_row_tile(row(t, k))], sem))


N_SLOT_REFS = 2 * TOP_K + 1


def _slot_specs(tm, step=lambda i: i):
    tile = pl.BlockSpec((tm,), lambda i: (step(i),), memory_space=pltpu.SMEM)
    return [tile] * (2 * TOP_K) + [pl.BlockSpec(memory_space=pltpu.SMEM)]


def _slot_row(refs):
    eid, rank, pstart = refs[:TOP_K], refs[TOP_K:2 * TOP_K], refs[2 * TOP_K]
    return lambda t, k: pstart[eid[k][t]] + rank[k][t]


def _dispatch(slots, yn, xb, tm, n_rows):
    t = yn.shape[0] // ROW_SUB
    prev = [] if xb is None else [xb]
    return pl.pallas_call(
        _dispatch_kernel,
        grid=(t // tm,),
        in_specs=_slot_specs(tm) + [pl.BlockSpec((tm * ROW_SUB, LANE), lambda i: (i, 0))]
        + [pl.BlockSpec(memory_space=pl.ANY) for _ in prev],
        out_specs=pl.BlockSpec(memory_space=pl.ANY),
        out_shape=jax.ShapeDtypeStruct((n_rows * ROW_SUB, LANE), jnp.uint32),
        scratch_shapes=[pltpu.SemaphoreType.DMA],
        input_output_aliases={N_SLOT_REFS + 1: 0} if prev else {},
        compiler_params=_params("arbitrary", has_side_effects=True),
        name="moe_dispatch",
    )(*slots, yn, *prev)


MOE_BLOCK = 512


def _moe_kernel(be_ref, nu_ref, nv_ref, xb_ref, wg_ref, wu_ref, wd_ref, yb_ref, wg_bf, wu_bf, wd_bf):
    i = pl.program_id(0)

    @pl.when(i < nu_ref[0])
    def _():
        @pl.when((i == 0) | (be_ref[i] != be_ref[jnp.maximum(i - 1, 0)]))
        def _():
            wg_bf[...] = wg_ref[0].astype(BF16)
            wu_bf[...] = wu_ref[0].astype(BF16)
            wd_bf[...] = wd_ref[0].astype(BF16)

        x = _load_rows(xb_ref)
        valid = lax.broadcasted_iota(jnp.int32, (x.shape[0], 1), 0) < nv_ref[i]
        lo, hi = _unpack_halves(jnp.where(valid, x, jnp.uint32(0)))
        lo, hi = lo.astype(BF16), hi.astype(BF16)
        w = lo.shape[1]
        a = _dot(lo, wg_bf[:w, :]) + _dot(hi, wg_bf[w:, :])
        u = _dot(lo, wu_bf[:w, :]) + _dot(hi, wu_bf[w:, :])
        hdn = (a * jax.nn.sigmoid(a) * u).astype(BF16)
        _store_rows(yb_ref, _pack_halves(_dot(hdn, wd_bf[...]).astype(BF16)))


def _moe_blocks(block_e, n_used, n_valid, xb, w_gate, w_up, w_down):
    p = xb.shape[0] // ROW_SUB
    d = w_gate.shape[1]
    nb = p // MOE_BLOCK
    clamp = lambda i, nu: jnp.minimum(i, nu[0] - 1)
    rows = pl.BlockSpec((MOE_BLOCK * ROW_SUB, LANE), lambda i, be, nu, nv: (clamp(i, nu), 0))
    grid_spec = pltpu.PrefetchScalarGridSpec(
        num_scalar_prefetch=3,
        grid=(nb,),
        in_specs=[rows,
                  pl.BlockSpec((1, d, D_EXPERT), lambda i, be, nu, nv: (be[clamp(i, nu)], 0, 0)),
                  pl.BlockSpec((1, d, D_EXPERT), lambda i, be, nu, nv: (be[clamp(i, nu)], 0, 0)),
                  pl.BlockSpec((1, D_EXPERT, d), lambda i, be, nu, nv: (be[clamp(i, nu)], 0, 0))],
        out_specs=rows,
        scratch_shapes=[pltpu.VMEM((d, D_EXPERT), BF16), pltpu.VMEM((d, D_EXPERT), BF16),
                        pltpu.VMEM((D_EXPERT, d), BF16)],
    )
    return pl.pallas_call(
        _moe_kernel,
        grid_spec=grid_spec,
        out_shape=jax.ShapeDtypeStruct(xb.shape, jnp.uint32),
        compiler_params=_params("arbitrary"),
        name="moe_experts",
    )(block_e, n_used, n_valid, xb, w_gate, w_up, w_down)


def _combine_kernel(*refs):
    row, next_row = _slot_row(refs[:N_SLOT_REFS]), _slot_row(refs[N_SLOT_REFS:2 * N_SLOT_REFS])
    y_ref, gate_ref, g_ref, yb_ref, o_ref, buf, sems = refs[2 * N_SLOT_REFS:]
    i, n, tm = pl.program_id(0), pl.num_programs(0), y_ref.shape[0]
    slot = i % 2

    def gather(row, slot):
        return lambda t, k: pltpu.make_async_copy(
            yb_ref.at[_row_tile(row(t, k))], buf.at[slot, k, _row_tile(t)], sems.at[slot])

    @pl.when(i == 0)
    def _():
        _start_rows(tm, gather(row, 0))

    @pl.when(i + 1 < n)
    def _():
        _start_rows(tm, gather(next_row, 1 - slot))

    _wait_rows(tm, gather(row, slot))
    gate = gate_ref[...]
    acc_lo = acc_hi = None
    for k in range(TOP_K):
        lo, hi = _unpack_halves(_load_rows(buf.at[slot, k]))
        gk = gate[:, k:k + 1]
        acc_lo = gk * lo if acc_lo is None else acc_lo + gk * lo
        acc_hi = gk * hi if acc_hi is None else acc_hi + gk * hi
    y = y_ref[...] + jnp.concatenate([acc_lo, acc_hi], axis=-1)
    o_ref[...] = _rms(y, g_ref[...])


def _combine(slots, y, gate, g, yb, tm):
    t, d = y.shape
    n = t // tm
    return pl.pallas_call(
        _combine_kernel,
        grid=(n,),
        in_specs=_slot_specs(tm) + _slot_specs(tm, lambda i: jnp.minimum(i + 1, n - 1))
        + [pl.BlockSpec((tm, d), lambda i: (i, 0)),
                                    pl.BlockSpec((tm, LANE), lambda i: (i, 0)),
                                    _resident((1, d)), pl.BlockSpec(memory_space=pl.ANY)],
        out_specs=pl.BlockSpec((tm, d), lambda i: (i, 0)),
        out_shape=jax.ShapeDtypeStruct((t, d), F32),
        scratch_shapes=[pltpu.VMEM((2, TOP_K, tm * ROW_SUB, LANE), jnp.uint32), pltpu.SemaphoreType.DMA((2,))],
        compiler_params=_params("arbitrary"),
        name="moe_combine",
    )(*slots, *slots, y, gate, g.reshape(1, -1), yb)


def _route_plan(counts, n_slots):
    padded = (counts + MOE_BLOCK - 1) // MOE_BLOCK * MOE_BLOCK
    pend = jnp.cumsum(padded)
    pstart = pend - padded
    n_blocks = -(-n_slots // MOE_BLOCK) + N_EXPERTS
    first_row = jnp.arange(n_blocks, dtype=jnp.int32) * MOE_BLOCK
    block_e = jnp.minimum(jnp.sum(pend[None, :] <= first_row[:, None], axis=1), N_EXPERTS - 1).astype(jnp.int32)
    n_used = (pend[-1] // MOE_BLOCK).astype(jnp.int32).reshape(1)
    n_valid = jnp.clip(jnp.take(pstart + counts, block_e) - first_row, 0, MOE_BLOCK).astype(jnp.int32)
    return pstart.astype(jnp.int32), block_e, n_used, n_valid, n_blocks


def _encoder_front(x, p, wts):
    b, s, d = x.shape
    x2d = x.reshape(b * s, d)
    tm = min(TOKEN_TILE, b * s)
    lat, sm, smt, qkv, z = _inproj(x2d, p["norm_mix"].reshape(1, -1), wts["w_lat"], wts["w_sm"], wts["w_smt"],
                                   wts["w_qkv"], wts["w_z"], tm)
    att = _mla(lat, sm, p["g_q_lora"], p["g_kv_lora"], p["w_uq"], p["w_ukv"], b, s)
    o_f, o_b = _gdn(qkv, sm, smt, p["conv_w"], p["a_log"], p["dt_bias"], b, s)
    return _outproj(x2d, att, o_f, o_b, z, p["g_attn_out"], p["g_gdn_out"], wts["w_out"], p["norm_ffn"],
                    wts["w_r"], wts["b_r"], tm)


def _prep_weights(p):
    w_in = p["w_in"]
    o = np.cumsum([0, Q_LORA, KV_LORA, QK_ROPE, GDN_QKV, D_GDN, NG, NG])
    d = w_in.shape[0]
    half = QK_ROPE // 2
    w_sm = jnp.concatenate([w_in[:, o[2]:o[3]], w_in[:, o[5]:o[7]], jnp.zeros((d, LANE - QK_ROPE - 2 * NG), F32),
                            w_in[:, o[2] + half:o[3]], w_in[:, o[2]:o[2] + half],
                            jnp.zeros((d, LANE - QK_ROPE), F32)], axis=1).astype(BF16)
    w_r = jnp.concatenate([p["w_router_group"], p["w_router_expert"],
                           jnp.zeros((d, LANE - N_GROUPS - N_EXPERTS), F32)], axis=1).astype(BF16)
    b_r = jnp.concatenate([p["b_router_group"], p["b_router_expert"],
                           jnp.zeros((LANE - N_GROUPS - N_EXPERTS,), F32)]).reshape(1, LANE)
    return dict(w_lat=w_in[:, :o[2]].astype(BF16), w_sm=w_sm, w_smt=w_sm[:, :LANE].T, w_qkv=w_in[:, o[3]:o[4]].astype(BF16),
                w_z=w_in[:, o[4]:o[5]].astype(BF16), w_out=p["w_out"].astype(BF16), w_r=w_r, b_r=b_r)


ROW_TILE = 512


def _encode(xs, p):
    wts = _prep_weights(p)
    fronts = [_encoder_front(x, p, wts) for x in xs]
    counts = jnp.zeros((1, LANE), jnp.int32)
    routed = []
    for f in fronts:
        gate, slots, counts = _router(f[2], counts, min(ROW_TILE, f[2].shape[0]))
        routed.append((gate, [slots[r] for r in range(2 * TOP_K)]))
    t = sum(f[0].shape[0] for f in fronts)
    pstart, block_e, n_used, n_valid, n_blocks = _route_plan(counts[0, :N_EXPERTS], t * TOP_K)
    xb = None
    for f, (_, slots) in zip(fronts, routed):
        xb = _dispatch(slots + [pstart], f[1], xb, min(ROW_TILE, f[0].shape[0]), n_blocks * MOE_BLOCK)
    yb = _moe_blocks(block_e, n_used, n_valid, xb, p["w_gate"], p["w_up"], p["w_down"])
    outs = []
    for x, f, (gate, slots) in zip(xs, fronts, routed):
        out = _combine(slots + [pstart], f[0], gate, p["norm_final"], yb, min(ROW_TILE, f[0].shape[0]))
        outs.append(out.reshape(x.shape))
    return outs


def kernel(x_prompt, x_sample, norm_mix, w_in, g_q_lora, g_kv_lora, w_uq, w_ukv, g_attn_out, conv_w, a_log,
           dt_bias, g_gdn_out, w_out, norm_ffn, w_router_group, b_router_group, w_router_expert,
           b_router_expert, w_gate, w_up, w_down, norm_final):
    p = dict(norm_mix=norm_mix[0], w_in=w_in[0], g_q_lora=g_q_lora[0], g_kv_lora=g_kv_lora[0], w_uq=w_uq[0],
             w_ukv=w_ukv[0], g_attn_out=g_attn_out[0], conv_w=conv_w[0], a_log=a_log[0], dt_bias=dt_bias[0],
             g_gdn_out=g_gdn_out[0], w_out=w_out[0], norm_ffn=norm_ffn[0], w_router_group=w_router_group[0],
             b_router_group=b_router_group[0], w_router_expert=w_router_expert[0],
             b_router_expert=b_router_expert[0], w_gate=w_gate[0], w_up=w_up[0], w_down=w_down[0],
             norm_final=norm_final)
    y_prompt, y_sample = _encode([x_prompt, x_sample], p)
    return (y_prompt, y_sample)
```
